```python
import math
import jax
import jax.numpy as jnp
from jax import lax
import numpy as np

D_MODEL = 1024
BATCH = 8
SEQ = 4096
DEPTH = 2

N_MIXERS = 2
N_ATTN_LAYERS = (DEPTH + 1) // 2
N_SSD_LAYERS = DEPTH // 2
EPS = 1e-6

ATTN_HEADS = 16
ATTN_HEAD_DIM = D_MODEL // ATTN_HEADS
MOBA_BLOCK = 256
MOBA_TOPK = 3
Q_BLOCK = 128
ROPE_THETA = 10000.0

SSD_EXPAND = 2
SSD_D_INNER = SSD_EXPAND * D_MODEL
SSD_HEAD_DIM = 64
SSD_HEADS = SSD_D_INNER // SSD_HEAD_DIM
SSD_GROUPS = 8
SSD_HEADS_PER_GROUP = SSD_HEADS // SSD_GROUPS
SSD_STATE = 128
SSD_CONV = 4
SSD_CHUNK = 128
SSD_BC_DIM = SSD_GROUPS * SSD_STATE
SSD_CONV_DIM = SSD_D_INNER + 2 * SSD_BC_DIM
SSD_IN_DIM = SSD_D_INNER + SSD_CONV_DIM + SSD_HEADS

MOE_GROUPS = 4
MOE_EXPERTS_PER_GROUP = 8
MOE_EXPERTS = MOE_GROUPS * MOE_EXPERTS_PER_GROUP
MOE_TOPK = 2
MOE_FF = D_MODEL // 2
MOE_ROW_BLOCK = 128

kernel_name = 'hybrid_moba_ssd_hier_moe'


def rmsnorm(x, w):
    xf = x.astype(jnp.float32)
    var = jnp.mean(xf * xf, axis=-1, keepdims=True)
    return (xf * lax.rsqrt(var + EPS)).astype(x.dtype) * w


def rope(x, pos):
    half = ATTN_HEAD_DIM // 2
    inv = ROPE_THETA ** (-jnp.arange(half, dtype=jnp.float32) / half)
    ang = pos.astype(jnp.float32)[:, None] * inv[None, :]
    cos = jnp.cos(ang).astype(x.dtype)
    sin = jnp.sin(ang).astype(x.dtype)
    x1, x2 = x[..., :half], x[..., half:]
    return jnp.concatenate([x1 * cos - x2 * sin, x2 * cos + x1 * sin], axis=-1)


def moba_attention(h, w_qkv, w_o):
    B, S, _ = h.shape
    H, Dh = ATTN_HEADS, ATTN_HEAD_DIM
    q, k, v = jnp.split(h @ w_qkv, 3, axis=-1)
    to_heads = lambda t: t.reshape(B, S, H, Dh).transpose(0, 2, 1, 3)
    pos = jnp.arange(S)
    q = rope(to_heads(q), pos)
    k = rope(to_heads(k), pos)
    v = to_heads(v)

    nb = -(-S // MOBA_BLOCK)
    pad = nb * MOBA_BLOCK - S
    kb = jnp.pad(k, ((0, 0), (0, 0), (0, pad), (0, 0))).reshape(B, H, nb, MOBA_BLOCK, Dh)
    vb = jnp.pad(v, ((0, 0), (0, 0), (0, pad), (0, 0))).reshape(B, H, nb, MOBA_BLOCK, Dh)

    k_mean = jnp.mean(kb.astype(jnp.float32), axis=3)
    gate = jnp.einsum('bhsd,bhnd->bhsn', q.astype(jnp.float32), k_mean)
    q_blk = pos // MOBA_BLOCK
    past = jnp.arange(nb)[None, :] < q_blk[:, None]
    gate = jnp.where(past[None, None], gate, -jnp.inf)
    k_sel = min(MOBA_TOPK, nb)
    _, sel_idx = lax.top_k(gate, k_sel)
    sel_valid = sel_idx < q_blk[None, None, :, None]

    nq = S // Q_BLOCK

    def to_qblocks(t):
        rest = t.shape[3:]
        t = t.reshape((B, H, nq, Q_BLOCK) + rest)
        return jnp.moveaxis(t, 2, 1).reshape((B * nq, H, Q_BLOCK) + rest)

    q_blocks = to_qblocks(q)
    idx_blocks = to_qblocks(sel_idx)
    valid_blocks = to_qblocks(sel_valid)
    b_ids = jnp.repeat(jnp.arange(B), nq)
    qb_ids = jnp.tile(jnp.arange(nq), B)
    head_ix = jnp.arange(H)[:, None, None]
    scale = 1.0 / math.sqrt(Dh)
    n_sel = k_sel * MOBA_BLOCK

    def attend_block(args):
        qblk, idx, valid, b, qb = args
        k_b = kb[b]
        v_b = vb[b]
        k_g = k_b[head_ix, idx]
        v_g = v_b[head_ix, idx]
        own = (qb * Q_BLOCK) // MOBA_BLOCK
        k_own = lax.dynamic_index_in_dim(k_b, own, axis=1, keepdims=False)
        v_own = lax.dynamic_index_in_dim(v_b, own, axis=1, keepdims=False)
        s_sel = jnp.einsum('hqd,hqjkd->hqjk', qblk, k_g).astype(jnp.float32) * scale
        s_sel = jnp.where(valid[..., None], s_sel, -jnp.inf).reshape(H, Q_BLOCK, n_sel)
        q_pos = qb * Q_BLOCK + jnp.arange(Q_BLOCK)
        k_pos = own * MOBA_BLOCK + jnp.arange(MOBA_BLOCK)
        s_own = jnp.einsum('hqd,hkd->hqk', qblk, k_own).astype(jnp.float32) * scale
        s_own = jnp.where(k_pos[None, :] <= q_pos[:, None], s_own, -jnp.inf)
        p = jax.nn.softmax(jnp.concatenate([s_sel, s_own], axis=-1), axis=-1).astype(v.dtype)
        p_sel = p[..., :n_sel].reshape(H, Q_BLOCK, k_sel, MOBA_BLOCK)
        return (jnp.einsum('hqjk,hqjkd->hqd', p_sel, v_g)
                + jnp.einsum('hqk,hkd->hqd', p[..., n_sel:], v_own))

    o = lax.map(attend_block, (q_blocks, idx_blocks, valid_blocks, b_ids, qb_ids))
    o = o.reshape(B, nq, H, Q_BLOCK, Dh).transpose(0, 1, 3, 2, 4).reshape(B, S, H * Dh)
    return o @ w_o


def ssd_mixer(h, w_in, conv_w, conv_b, dt_bias, a_log, d_skip, norm_w, w_out):
    B, S, _ = h.shape
    G, R, P, N, L = SSD_GROUPS, SSD_HEADS_PER_GROUP, SSD_HEAD_DIM, SSD_STATE, SSD_CHUNK
    f32 = jnp.float32
    zxbcdt = h @ w_in
    z, xbc, dt = jnp.split(zxbcdt, [SSD_D_INNER, SSD_D_INNER + SSD_CONV_DIM], axis=-1)
    xbc = lax.conv_general_dilated(xbc, conv_w, window_strides=(1,), padding=[(SSD_CONV - 1, 0)],
                                   dimension_numbers=('NWC', 'WIO', 'NWC'),
                                   feature_group_count=SSD_CONV_DIM) + conv_b
    xbc = jax.nn.silu(xbc)
    xs, Bm, Cm = jnp.split(xbc, [SSD_D_INNER, SSD_D_INNER + SSD_BC_DIM], axis=-1)
    dt = jax.nn.softplus((dt + dt_bias).astype(f32))
    A = -jnp.exp(a_log.astype(f32))

    nc = S // L
    x_c = xs.reshape(B, nc, L, G, R, P).astype(f32)
    B_c = Bm.reshape(B, nc, L, G, N).astype(f32)
    C_c = Cm.reshape(B, nc, L, G, N).astype(f32)
    dt_c = dt.reshape(B, nc, L, G, R)
    xdt = x_c * dt_c[..., None]
    dA = jnp.moveaxis(dt_c * A.reshape(G, R), 2, -1)
    cs = jnp.cumsum(dA, axis=-1)
    tril = jnp.tril(jnp.ones((L, L), dtype=bool))
    seg = cs[..., :, None] - cs[..., None, :]
    decay_in = jnp.exp(jnp.where(tril, seg, -jnp.inf))
    cb = jnp.einsum('bclgn,bcsgn->bcgls', C_c, B_c)
    y_diag = jnp.einsum('bcgls,bcgrls,bcsgrp->bclgrp', cb, decay_in, xdt)
    decay_states = jnp.exp(cs[..., -1:] - cs)
    states = jnp.einsum('bcsgn,bcgrs,bcsgrp->bcgrpn', B_c, decay_states, xdt)
    chunk_decay = jnp.exp(cs[..., -1])

    def step(carry, inp):
        st, dec = inp
        return carry * dec[..., None, None] + st, carry

    init = jnp.zeros((B, G, R, P, N), f32)
    _, prev = lax.scan(step, init, (jnp.moveaxis(states, 1, 0), jnp.moveaxis(chunk_decay, 1, 0)))
    prev = jnp.moveaxis(prev, 0, 1)
    y_off = jnp.einsum('bclgn,bcgrpn,bcgrl->bclgrp', C_c, prev, jnp.exp(cs))
    y = y_diag + y_off + x_c * d_skip.astype(f32).reshape(G, R, 1)
    y = y.reshape(B, S, SSD_D_INNER)
    yg = (y * jax.nn.silu(z.astype(f32))).reshape(B, S, G, SSD_D_INNER // G)
    yg = yg * lax.rsqrt(jnp.mean(yg * yg, axis=-1, keepdims=True) + EPS)
    y = yg.reshape(B, S, SSD_D_INNER).astype(h.dtype) * norm_w
    return y @ w_out


def hier_moe(h, w_group, b_group, w_router, b_router, w_up, w_down):
    B, S, D = h.shape
    T = B * S
    E, RB = MOE_EXPERTS, MOE_ROW_BLOCK
    xt = h.reshape(T, D)
    g_prob = jax.nn.softmax((xt @ w_group).astype(jnp.float32) + b_group, axis=-1)
    g_p, g_idx = lax.top_k(g_prob, 1)
    e_logits_all = jnp.einsum('td,gde->tge', xt, w_router).astype(jnp.float32) + b_router
    e_logits = jnp.take_along_axis(e_logits_all, g_idx[:, :, None], axis=1)[:, 0]
    e_p, e_idx = lax.top_k(jax.nn.softmax(e_logits, axis=-1), MOE_TOPK)
    e_p = e_p / jnp.sum(e_p, axis=-1, keepdims=True)
    gates = (g_p * e_p).reshape(-1)
    flat_e = (g_idx * MOE_EXPERTS_PER_GROUP + e_idx).reshape(-1)
    flat_tok = jnp.repeat(jnp.arange(T), MOE_TOPK)

    n_assign = T * MOE_TOPK
    order = jnp.argsort(flat_e)
    sorted_e = flat_e[order]
    tok_sorted = flat_tok[order]
    counts = jnp.zeros((E,), jnp.int32).at[flat_e].add(1)
    starts = jnp.cumsum(counts) - counts
    padded = ((counts + RB - 1) // RB) * RB
    pends = jnp.cumsum(padded)
    pstarts = pends - padded
    dest = pstarts[sorted_e] + (jnp.arange(n_assign) - starts[sorted_e])
    n_blocks = n_assign // RB + E
    cap = n_blocks * RB
    rows = jnp.zeros((cap, D), h.dtype).at[dest].set(xt[tok_sorted])
    block_start = jnp.arange(n_blocks) * RB
    block_expert = jnp.minimum(jnp.sum(block_start[:, None] >= pends[None, :], axis=1), E - 1)

    def expert_block(args):
        xb, e = args
        gu = xb @ w_up[e]
        g, u = jnp.split(gu, 2, axis=-1)
        return (jax.nn.silu(g) * u) @ w_down[e]

    y_rows = lax.map(expert_block, (rows.reshape(n_blocks, RB, D), block_expert)).reshape(cap, D)
    y_assign = y_rows[dest] * gates[order][:, None].astype(h.dtype)
    out = jnp.zeros((T, D), h.dtype).at[tok_sorted].add(y_assign)
    return out.reshape(B, S, D)


def setup_inputs(seed: int = 0) -> dict:
    key = jax.random.key(seed)
    ks = jax.random.split(key, 20)
    nrm = lambda k, shape, scale: jax.random.normal(k, shape, jnp.float32) * scale
    dt0 = jnp.exp(jax.random.uniform(ks[9], (N_SSD_LAYERS, SSD_HEADS), jnp.float32,
                                     math.log(1e-3), math.log(1e-1)))
    return {
        'x': nrm(ks[0], (BATCH, SEQ, D_MODEL), 1.0),
        'mix_norm': 1.0 + nrm(ks[1], (DEPTH, D_MODEL), 0.01),
        'ffn_norm': 1.0 + nrm(ks[2], (DEPTH, D_MODEL), 0.01),
        'final_norm': 1.0 + nrm(ks[3], (D_MODEL,), 0.01),
        'attn_w_qkv': nrm(ks[4], (N_ATTN_LAYERS, D_MODEL, 3 * D_MODEL), D_MODEL ** -0.5),
        'attn_w_o': nrm(ks[5], (N_ATTN_LAYERS, D_MODEL, D_MODEL), D_MODEL ** -0.5),
        'ssd_w_in': nrm(ks[6], (N_SSD_LAYERS, D_MODEL, SSD_IN_DIM), D_MODEL ** -0.5),
        'ssd_conv_w': nrm(ks[7], (N_SSD_LAYERS, SSD_CONV, 1, SSD_CONV_DIM), SSD_CONV ** -0.5),
        'ssd_conv_b': nrm(ks[8], (N_SSD_LAYERS, SSD_CONV_DIM), 0.01),
        'ssd_dt_bias': dt0 + jnp.log(-jnp.expm1(-dt0)),
        'ssd_a_log': jnp.log(jax.random.uniform(ks[10], (N_SSD_LAYERS, SSD_HEADS), jnp.float32, 1.0, 16.0)),
        'ssd_d': 1.0 + nrm(ks[11], (N_SSD_LAYERS, SSD_HEADS), 0.01),
        'ssd_norm': 1.0 + nrm(ks[12], (N_SSD_LAYERS, SSD_D_INNER), 0.01),
        'ssd_w_out': nrm(ks[13], (N_SSD_LAYERS, SSD_D_INNER, D_MODEL), SSD_D_INNER ** -0.5),
        'moe_w_group': nrm(ks[14], (DEPTH, D_MODEL, MOE_GROUPS), D_MODEL ** -0.5),
        'moe_b_group': nrm(ks[15], (DEPTH, MOE_GROUPS), 0.01),
        'moe_w_router': nrm(ks[16], (DEPTH, MOE_GROUPS, D_MODEL, MOE_EXPERTS_PER_GROUP), D_MODEL ** -0.5),
        'moe_b_router': nrm(ks[17], (DEPTH, MOE_GROUPS, MOE_EXPERTS_PER_GROUP), 0.01),
        'moe_w_up': nrm(ks[18], (DEPTH, MOE_EXPERTS, D_MODEL, 2 * MOE_FF), D_MODEL ** -0.5),
        'moe_w_down': nrm(ks[19], (DEPTH, MOE_EXPERTS, MOE_FF, D_MODEL), MOE_FF ** -0.5),
    }


def reference(x, mix_norm, ffn_norm, final_norm, attn_w_qkv, attn_w_o, ssd_w_in, ssd_conv_w,
              ssd_conv_b, ssd_dt_bias, ssd_a_log, ssd_d, ssd_norm, ssd_w_out, moe_w_group,
              moe_b_group, moe_w_router, moe_b_router, moe_w_up, moe_w_down):
    for i in range(DEPTH):
        h = rmsnorm(x, mix_norm[i])
        j = i // N_MIXERS
        if i % N_MIXERS == 0:
            x = x + moba_attention(h, attn_w_qkv[j], attn_w_o[j])
        else:
            x = x + ssd_mixer(h, ssd_w_in[j], ssd_conv_w[j], ssd_conv_b[j], ssd_dt_bias[j],
                              ssd_a_log[j], ssd_d[j], ssd_norm[j], ssd_w_out[j])
        h = rmsnorm(x, ffn_norm[i])
        x = x + hier_moe(h, moe_w_group[i], moe_b_group[i], moe_w_router[i], moe_b_router[i],
                         moe_w_up[i], moe_w_down[i])
    return rmsnorm(x, final_norm)
```

```python
import functools
import math

import jax
import jax.numpy as jnp
from jax import lax
from jax.experimental import pallas as pl
from jax.experimental.pallas import tpu as pltpu

F32 = jnp.float32
BF16 = jnp.bfloat16
I32 = jnp.int32

EPS = 1e-6
LANES = 128
SUBLANES = 8
VMEM_LIMIT = 56 * 1024 * 1024

ATTN_HEADS = 16
ATTN_HEAD_DIM = 64
MOBA_BLOCK = 256
MOBA_TOPK = 3
ROPE_THETA = 10000.0
HEADS_PER_TILE = LANES // ATTN_HEAD_DIM

SSD_HEAD_DIM = 64
SSD_GROUPS = 8
SSD_STATE = 128
SSD_CONV = 4
SSD_CHUNK = 128

MOE_GROUPS = 4
MOE_EXPERTS_PER_GROUP = 8
MOE_EXPERTS = MOE_GROUPS * MOE_EXPERTS_PER_GROUP
MOE_ROW_BLOCK = 256
ROUTER_ROWS = SUBLANES + MOE_EXPERTS
ROUTER_TILE = 256
DISPATCH_TILE = 256
COMBINE_TILE = 128

NT_DIMS = (((1,), (1,)), ((), ()))
NN_DIMS = (((1,), (0,)), ((), ()))
TN_DIMS = (((0,), (0,)), ((), ()))


def _params(*sem):
    return pltpu.CompilerParams(dimension_semantics=sem, vmem_limit_bytes=VMEM_LIMIT)


def _rmsnorm_rows(x, w):
    var = jnp.mean(x * x, axis=-1, keepdims=True)
    return (x * lax.rsqrt(var + EPS)) * w


def _split3(x):
    hi = x.astype(BF16)
    r = x - hi.astype(F32)
    mid = r.astype(BF16)
    lo = (r - mid.astype(F32)).astype(BF16)
    return hi, mid, lo


def _dot(a, b, dims=NN_DIMS):
    return lax.dot_general(a, b, dims, preferred_element_type=F32)


def _dot_f32_lhs(x, e, dims=NN_DIMS):
    hi, mid, lo = _split3(x)
    return _dot(hi, e, dims) + _dot(mid, e, dims) + _dot(lo, e, dims)


def _dot_f32_rhs(e, x, dims=NN_DIMS):
    hi, mid, lo = _split3(x)
    return _dot(e, hi, dims) + _dot(e, mid, dims) + _dot(e, lo, dims)


def _dot_f32(a, b, dims=NN_DIMS):
    ah, am, al = _split3(a)
    bh, bm, bl = _split3(b)
    return (_dot(ah, bh, dims) + _dot(ah, bm, dims) + _dot(am, bh, dims)
            + _dot(ah, bl, dims) + _dot(am, bm, dims) + _dot(al, bh, dims))


def _silu(x):
    return x / (1.0 + jnp.exp(-x))


def _softplus(x):
    return jnp.maximum(x, 0.0) + jnp.log(1.0 + jnp.exp(-jnp.abs(x)))


def _qkv_kernel(x_ref, nw_ref, w_ref, cos_ref, sin_ref, q_ref, k_ref, v_ref, km_ref, *, d, scale):
    h = _rmsnorm_rows(x_ref[...], nw_ref[...]).astype(BF16)
    cos = cos_ref[...]
    sin = sin_ref[...]
    lane = lax.broadcasted_iota(I32, (1, LANES), 1)
    first_half = (lane % ATTN_HEAD_DIM) < (ATTN_HEAD_DIM // 2)

    def rope_tile(xc):
        lo_partner = pltpu.roll(xc, LANES - ATTN_HEAD_DIM // 2, 1)
        hi_partner = pltpu.roll(xc, ATTN_HEAD_DIM // 2, 1)
        return xc * cos + jnp.where(first_half, lo_partner, hi_partner) * sin

    q = _dot(h, w_ref[:, 0:d])
    k = _dot(h, w_ref[:, d:2 * d])
    for c in range(d // LANES):
        sl = slice(c * LANES, (c + 1) * LANES)
        q_ref[:, sl] = (rope_tile(q[:, sl]) * scale).astype(BF16)
        kc = rope_tile(k[:, sl])
        k_ref[:, sl] = kc.astype(BF16)
        km_ref[0, :, sl] = jnp.mean(kc, axis=0, keepdims=True)
    v_ref[...] = _dot(h, w_ref[:, 2 * d:3 * d]).astype(BF16)


def _qkv_rope(x, norm_w, w_qkv, cos_t, sin_t, seq):
    t, d = x.shape
    tm = MOBA_BLOCK
    n_tiles = t // tm
    seq_tiles = seq // tm
    kern = functools.partial(_qkv_kernel, d=d, scale=1.0 / math.sqrt(ATTN_HEAD_DIM))
    return pl.pallas_call(
        kern,
        grid=(n_tiles,),
        in_specs=[
            pl.BlockSpec((tm, d), lambda i: (i, 0)),
            pl.BlockSpec((1, d), lambda i: (0, 0)),
            pl.BlockSpec((d, 3 * d), lambda i: (0, 0), pipeline_mode=pl.Buffered(1)),
            pl.BlockSpec((tm, LANES), lambda i: (i % seq_tiles, 0)),
            pl.BlockSpec((tm, LANES), lambda i: (i % seq_tiles, 0)),
        ],
        out_specs=[
            pl.BlockSpec((tm, d), lambda i: (i, 0)),
            pl.BlockSpec((tm, d), lambda i: (i, 0)),
            pl.BlockSpec((tm, d), lambda i: (i, 0)),
            pl.BlockSpec((1, 1, d), lambda i: (i, 0, 0)),
        ],
        out_shape=[
            jax.ShapeDtypeStruct((t, d), BF16),
            jax.ShapeDtypeStruct((t, d), BF16),
            jax.ShapeDtypeStruct((t, d), BF16),
            jax.ShapeDtypeStruct((n_tiles, 1, d), F32),
        ],
        compiler_params=_params("parallel"),
        name="qkv_rope",
    )(x, norm_w, w_qkv, cos_t, sin_t)


def _attn_kernel(q_ref, k_ref, v_ref, km_ref, o_ref, *, nb):
    qi = pl.program_id(2)
    tq = q_ref.shape[0]
    q2 = q_ref[...]
    km = km_ref[...]
    lane = lax.broadcasted_iota(I32, (1, LANES), 1)
    row = lax.broadcasted_iota(I32, (tq, MOBA_BLOCK), 0)
    col = lax.broadcasted_iota(I32, (tq, MOBA_BLOCK), 1)
    causal = col <= row
    blk = lax.broadcasted_iota(I32, (tq, nb), 1)
    neg_inf = jnp.float32(-jnp.inf)

    outs = []
    for h in range(HEADS_PER_TILE):
        head_lanes = (lane // ATTN_HEAD_DIM) == h
        qh = jnp.where(head_lanes, q2, jnp.zeros_like(q2))

        gate = _dot_f32_rhs(qh, km, NT_DIMS)
        alive = (blk < qi).astype(I32)
        sel = jnp.zeros((tq, nb), I32)
        for _ in range(min(MOBA_TOPK, nb)):
            gm = jnp.where(alive > 0, gate, neg_inf)
            top = jnp.max(gm, axis=1, keepdims=True)
            cand = jnp.where((alive > 0) & (gm == top), blk, nb)
            idx = jnp.min(cand, axis=1, keepdims=True)
            pick = (blk == idx).astype(I32)
            sel = sel | pick
            alive = alive & (1 - pick)
        bits = jnp.sum(sel << blk, axis=1, keepdims=True)

        def step(j, carry, diag):
            m_i, l_i, acc = carry
            start = pl.multiple_of(j * MOBA_BLOCK, MOBA_BLOCK)
            kb = k_ref[pl.ds(start, MOBA_BLOCK), :]
            vb = v_ref[pl.ds(start, MOBA_BLOCK), :]
            s = _dot(qh, kb, NT_DIMS)
            if diag:
                s = jnp.where(causal, s, neg_inf)
            else:
                keep = ((bits >> j) & 1) == 1
                s = jnp.where(keep, s, neg_inf)
            m_new = jnp.maximum(m_i, jnp.max(s, axis=1, keepdims=True))
            alpha = jnp.exp(m_i - m_new)
            p = jnp.exp(s - m_new)
            l_new = alpha * l_i + jnp.sum(p, axis=1, keepdims=True)
            acc_new = alpha * acc + _dot(p.astype(BF16), vb)
            return m_new, l_new, acc_new

        carry = (jnp.full((tq, 1), neg_inf, F32), jnp.zeros((tq, 1), F32), jnp.zeros((tq, LANES), F32))
        carry = step(qi, carry, True)
        carry = lax.fori_loop(0, qi, lambda j, c: step(j, c, False), carry)
        _, l_i, acc = carry
        outs.append(acc / l_i)

    o = outs[0]
    for h in range(1, HEADS_PER_TILE):
        o = jnp.where((lane // ATTN_HEAD_DIM) == h, outs[h], o)
    o_ref[...] = o.astype(BF16)


def _moba_attention(q, k, v, kmean, batch, seq):
    t, d = q.shape
    nb = seq // MOBA_BLOCK
    tq = MOBA_BLOCK
    nq = seq // tq
    col_tiles = d // LANES
    kern = functools.partial(_attn_kernel, nb=nb)
    return pl.pallas_call(
        kern,
        grid=(batch, col_tiles, nq),
        in_specs=[
            pl.BlockSpec((tq, LANES), lambda b, c, i: (b * nq + i, c)),
            pl.BlockSpec((seq, LANES), lambda b, c, i: (b, c)),
            pl.BlockSpec((seq, LANES), lambda b, c, i: (b, c)),
            pl.BlockSpec((None, nb, LANES), lambda b, c, i: (b, 0, c)),
        ],
        out_specs=pl.BlockSpec((tq, LANES), lambda b, c, i: (b * nq + i, c)),
        out_shape=jax.ShapeDtypeStruct((t, d), BF16),
        compiler_params=_params("parallel", "parallel", "arbitrary"),
        name="moba_attention",
    )(q, k, v, kmean)


def _proj_res_kernel(a_ref, w_ref, res_ref, o_ref):
    o_ref[...] = res_ref[...] + _dot(a_ref[...], w_ref[...])


def _proj_residual(a, w, res, tm=512):
    t, kdim = a.shape
    d = w.shape[1]
    return pl.pallas_call(
        _proj_res_kernel,
        grid=(t // tm,),
        in_specs=[
            pl.BlockSpec((tm, kdim), lambda i: (i, 0)),
            pl.BlockSpec((kdim, d), lambda i: (0, 0), pipeline_mode=pl.Buffered(1)),
            pl.BlockSpec((tm, d), lambda i: (i, 0)),
        ],
        out_specs=pl.BlockSpec((tm, d), lambda i: (i, 0)),
        out_shape=jax.ShapeDtypeStruct((t, d), F32),
        compiler_params=_params("parallel"),
        name="proj_residual",
    )(a, w, res)


def _router_kernel(x_ref, nw_ref, wr_ref, br_ref, idx_ref, gate_ref, cnt_ref, carry_ref):
    tm = x_ref.shape[0]

    @pl.when(pl.program_id(0) == 0)
    def _():
        carry_ref[...] = jnp.zeros_like(carry_ref)

    hn = _rmsnorm_rows(x_ref[...], nw_ref[...])
    logits = _dot_f32(wr_ref[...], hn, NT_DIMS) + br_ref[...]
    sub = lax.broadcasted_iota(I32, (SUBLANES, tm), 0)
    neg_inf = jnp.float32(-jnp.inf)

    gl = logits[0:SUBLANES]
    gmax = jnp.max(gl, axis=0, keepdims=True)
    gidx = jnp.min(jnp.where(gl == gmax, sub, SUBLANES), axis=0, keepdims=True)
    gprob = 1.0 / jnp.sum(jnp.exp(gl - gmax), axis=0, keepdims=True)

    el = jnp.zeros((SUBLANES, tm), F32)
    for g in range(MOE_GROUPS):
        el = jnp.where(gidx == g, logits[SUBLANES * (g + 1):SUBLANES * (g + 2)], el)
    m1 = jnp.max(el, axis=0, keepdims=True)
    i1 = jnp.min(jnp.where(el == m1, sub, SUBLANES), axis=0, keepdims=True)
    el2 = jnp.where(sub == i1, neg_inf, el)
    m2 = jnp.max(el2, axis=0, keepdims=True)
    i2 = jnp.min(jnp.where(el2 == m2, sub, SUBLANES), axis=0, keepdims=True)
    ratio = jnp.exp(m2 - m1)
    den = 1.0 + ratio
    gate_ref[...] = jnp.zeros_like(gate_ref)
    gate_ref[0:1, :] = gprob / den
    gate_ref[1:2, :] = gprob * ratio / den

    e1 = gidx * MOE_EXPERTS_PER_GROUP + i1
    e2 = gidx * MOE_EXPERTS_PER_GROUP + i2

    eiota = lax.broadcasted_iota(I32, (MOE_EXPERTS, tm), 0)
    oh1 = eiota == e1
    oh2 = eiota == e2
    onehot = (oh1 | oh2).astype(BF16)
    src = lax.broadcasted_iota(I32, (tm, tm), 0)
    dst = lax.broadcasted_iota(I32, (tm, tm), 1)
    before = (src < dst).astype(BF16)
    prefix = _dot(onehot, before)
    total = _dot(onehot, jnp.ones((tm, tm), BF16))
    base = carry_ref[...] + prefix
    r1 = jnp.sum(jnp.where(oh1, base, 0.0), axis=0, keepdims=True)
    r2 = jnp.sum(jnp.where(oh2, base, 0.0), axis=0, keepdims=True)
    carry_ref[...] = carry_ref[...] + total
    cnt_ref[...] = carry_ref[...]

    idx_ref[...] = jnp.zeros_like(idx_ref)
    idx_ref[0:1, :] = e1
    idx_ref[1:2, :] = e2
    idx_ref[2:3, :] = r1.astype(I32)
    idx_ref[3:4, :] = r2.astype(I32)


def _router(x, norm_w, wr_t, br):
    t, d = x.shape
    tm = ROUTER_TILE
    return pl.pallas_call(
        _router_kernel,
        grid=(t // tm,),
        in_specs=[
            pl.BlockSpec((tm, d), lambda i: (i, 0)),
            pl.BlockSpec((1, d), lambda i: (0, 0)),
            pl.BlockSpec((ROUTER_ROWS, d), lambda i: (0, 0)),
            pl.BlockSpec((ROUTER_ROWS, 1), lambda i: (0, 0)),
        ],
        out_specs=[
            pl.BlockSpec((SUBLANES, tm), lambda i: (0, i)),
            pl.BlockSpec((SUBLANES, tm), lambda i: (0, i)),
            pl.BlockSpec((MOE_EXPERTS, tm), lambda i: (0, 0)),
        ],
        out_shape=[
            jax.ShapeDtypeStruct((SUBLANES, t), I32),
            jax.ShapeDtypeStruct((SUBLANES, t), F32),
            jax.ShapeDtypeStruct((MOE_EXPERTS, tm), F32),
        ],
        scratch_shapes=[pltpu.VMEM((MOE_EXPERTS, tm), F32)],
        compiler_params=_params("arbitrary"),
        name="moe_router",
    )(x, norm_w, wr_t, br)


def _row_copy(src_ref, src_row, dst_ref, dst_row, sem):
    return pltpu.make_async_copy(src_ref.at[pl.ds(src_row, 1)], dst_ref.at[pl.ds(dst_row, 1)], sem)


def _dispatch_kernel(pstart_ref, x_ref, nw_ref, idx_ref, rows_in_ref, rows_ref, hn_ref, sem):
    del rows_in_ref
    tm = x_ref.shape[0]
    hn_ref[...] = _rmsnorm_rows(x_ref[...], nw_ref[...])

    def slot_row(t, s):
        return pstart_ref[idx_ref[s, t]] + idx_ref[2 + s, t]

    def issue(t, c):
        for s in range(2):
            _row_copy(hn_ref, t, rows_ref, slot_row(t, s), sem.at[s]).start()
        return c

    lax.fori_loop(0, tm, issue, 0)

    def drain(t, c):
        for s in range(2):
            _row_copy(hn_ref, t, rows_ref, slot_row(t, s), sem.at[s]).wait()
        return c

    lax.fori_loop(0, tm, drain, 0)


def _dispatch(pstart, x, norm_w, idx, cap):
    t, d = x.shape
    tm = DISPATCH_TILE
    rows0 = jnp.zeros((cap, d), F32)
    grid_spec = pltpu.PrefetchScalarGridSpec(
        num_scalar_prefetch=1,
        grid=(t // tm,),
        in_specs=[
            pl.BlockSpec((tm, d), lambda i, ps: (i, 0)),
            pl.BlockSpec((1, d), lambda i, ps: (0, 0)),
            pl.BlockSpec((SUBLANES, tm), lambda i, ps: (0, i), memory_space=pltpu.SMEM),
            pl.BlockSpec(memory_space=pl.ANY),
        ],
        out_specs=pl.BlockSpec(memory_space=pl.ANY),
        scratch_shapes=[pltpu.VMEM((tm, d), F32), pltpu.SemaphoreType.DMA((2,))],
    )
    return pl.pallas_call(
        _dispatch_kernel,
        grid_spec=grid_spec,
        out_shape=jax.ShapeDtypeStruct((cap, d), F32),
        input_output_aliases={4: 0},
        compiler_params=_params("arbitrary"),
        name="moe_dispatch",
    )(pstart, x, norm_w, idx, rows0)


def _expert_kernel(be_ref, rows_ref, wu_ref, wd_ref, y_ref):
    del be_ref
    ff = wd_ref.shape[0]
    xb = rows_ref[...].astype(BF16)
    gu = _dot(xb, wu_ref[...])
    act = _silu(gu[:, 0:ff]) * gu[:, ff:2 * ff]
    y_ref[...] = _dot(act.astype(BF16), wd_ref[...])


def _experts(block_expert, rows, w_up, w_down):
    cap, d = rows.shape
    rb = MOE_ROW_BLOCK
    ff = w_down.shape[1]
    grid_spec = pltpu.PrefetchScalarGridSpec(
        num_scalar_prefetch=1,
        grid=(cap // rb,),
        in_specs=[
            pl.BlockSpec((rb, d), lambda i, be: (i, 0)),
            pl.BlockSpec((None, d, 2 * ff), lambda i, be: (be[i], 0, 0)),
            pl.BlockSpec((None, ff, d), lambda i, be: (be[i], 0, 0)),
        ],
        out_specs=pl.BlockSpec((rb, d), lambda i, be: (i, 0)),
    )
    return pl.pallas_call(
        _expert_kernel,
        grid_spec=grid_spec,
        out_shape=jax.ShapeDtypeStruct((cap, d), F32),
        compiler_params=_params("arbitrary"),
        name="moe_experts",
    )(block_expert, rows, w_up, w_down)


def _combine_kernel(pstart_ref, x_ref, gate_ref, idx_ref, fw_ref, y_ref, o_ref, buf_ref, sem, *, final_norm):
    tm = x_ref.shape[0]

    def slot_row(t, s):
        return pstart_ref[idx_ref[s, t]] + idx_ref[2 + s, t]

    def issue(t, c):
        for s in range(2):
            _row_copy(y_ref, slot_row(t, s), buf_ref.at[s], t, sem.at[s]).start()
        return c

    lax.fori_loop(0, tm, issue, 0)

    def drain(t, c):
        for s in range(2):
            _row_copy(y_ref, slot_row(t, s), buf_ref.at[s], t, sem.at[s]).wait()
        return c

    lax.fori_loop(0, tm, drain, 0)

    g = gate_ref[...]
    gsq = jnp.concatenate([g, jnp.zeros((tm - SUBLANES, tm), F32)], axis=0).T
    out = x_ref[...] + gsq[:, 0:1] * buf_ref[0] + gsq[:, 1:2] * buf_ref[1]
    if final_norm:
        out = _rmsnorm_rows(out, fw_ref[...])
    o_ref[...] = out


def _combine(pstart, x, gates, idx, y_rows, final_w, final_norm):
    t, d = x.shape
    tm = COMBINE_TILE
    grid_spec = pltpu.PrefetchScalarGridSpec(
        num_scalar_prefetch=1,
        grid=(t // tm,),
        in_specs=[
            pl.BlockSpec((tm, d), lambda i, ps: (i, 0)),
            pl.BlockSpec((SUBLANES, tm), lambda i, ps: (0, i)),
            pl.BlockSpec((SUBLANES, tm), lambda i, ps: (0, i), memory_space=pltpu.SMEM),
            pl.BlockSpec((1, d), lambda i, ps: (0, 0)),
            pl.BlockSpec(memory_space=pl.ANY),
        ],
        out_specs=pl.BlockSpec((tm, d), lambda i, ps: (i, 0)),
        scratch_shapes=[pltpu.VMEM((2, tm, d), F32), pltpu.SemaphoreType.DMA((2,))],
    )
    return pl.pallas_call(
        functools.partial(_combine_kernel, final_norm=final_norm),
        grid_spec=grid_spec,
        out_shape=jax.ShapeDtypeStruct((t, d), F32),
        compiler_params=_params("arbitrary"),
        name="moe_combine",
    )(pstart, x, gates, idx, final_w, y_rows)


def _hier_moe(x, norm_w, w_group, b_group, w_router, b_router, w_up, w_down, final_w, final_norm):
    t, d = x.shape
    g, _, epg = w_router.shape
    pad_rows = SUBLANES - g
    wr_t = jnp.concatenate([w_group.T, jnp.zeros((pad_rows, d), F32),
                            w_router.transpose(0, 2, 1).reshape(g * epg, d)], axis=0)
    br = jnp.concatenate([b_group, jnp.full((pad_rows,), -1e30, F32), b_router.reshape(-1)])[:, None]
    idx, gates, cnt = _router(x, norm_w, wr_t, br)

    rb = MOE_ROW_BLOCK
    n_exp = g * epg
    counts = cnt[:, 0].astype(I32)
    padded = ((counts + rb - 1) // rb) * rb
    pends = jnp.cumsum(padded)
    pstart = (pends - padded).astype(I32)
    n_blocks = (2 * t) // rb + n_exp
    block_start = jnp.arange(n_blocks, dtype=I32) * rb
    block_expert = jnp.minimum(jnp.sum(block_start[:, None] >= pends[None, :], axis=1), n_exp - 1).astype(I32)

    rows = _dispatch(pstart, x, norm_w, idx, n_blocks * rb)
    y_rows = _experts(block_expert, rows, w_up.astype(BF16), w_down.astype(BF16))
    return _combine(pstart, x, gates, idx, y_rows, final_w, final_norm)


def _ssd_in_kernel(x_ref, nw_ref, wz_ref, wx_ref, wdt_ref, z_ref, xbc_ref, dt_ref):
    h = _rmsnorm_rows(x_ref[...], nw_ref[...]).astype(BF16)
    z_ref[...] = _dot(h, wz_ref[...])
    xbc_ref[...] = _dot(h, wx_ref[...])
    dt_ref[...] = _dot(h, wdt_ref[...])


def _ssd_in_proj(x, norm_w, w_z, w_xbc, w_dt, tm=256):
    t, d = x.shape
    dz, dx, dh = w_z.shape[1], w_xbc.shape[1], w_dt.shape[1]
    resident = lambda n: pl.BlockSpec((d, n), lambda i: (0, 0), pipeline_mode=pl.Buffered(1))
    return pl.pallas_call(
        _ssd_in_kernel,
        grid=(t // tm,),
        in_specs=[
            pl.BlockSpec((tm, d), lambda i: (i, 0)),
            pl.BlockSpec((1, d), lambda i: (0, 0)),
            resident(dz), resident(dx), resident(dh),
        ],
        out_specs=[
            pl.BlockSpec((tm, dz), lambda i: (i, 0)),
            pl.BlockSpec((tm, dx), lambda i: (i, 0)),
            pl.BlockSpec((tm, dh), lambda i: (i, 0)),
        ],
        out_shape=[
            jax.ShapeDtypeStruct((t, dz), F32),
            jax.ShapeDtypeStruct((t, dx), F32),
            jax.ShapeDtypeStruct((t, dh), F32),
        ],
        compiler_params=_params("parallel"),
        name="ssd_in_proj",
    )(x, norm_w, w_z, w_xbc, w_dt)


def _ssd_kernel(xbc_ref, z_ref, dt_ref, cw_ref, cb_ref, dtb_ref, alog_ref, dskip_ref, nw_ref,
                y_ref, ext_ref, state_ref, *, d_inner, n_heads):
    L = SSD_CHUNK
    G = SSD_GROUPS
    N = SSD_STATE
    P = SSD_HEAD_DIM
    R = n_heads // G
    GW = R * P
    halo = SUBLANES

    @pl.when(pl.program_id(1) == 0)
    def _():
        ext_ref[0:halo, :] = jnp.zeros((halo, ext_ref.shape[1]), F32)
        state_ref[...] = jnp.zeros_like(state_ref)

    ext_ref[halo:halo + L, :] = xbc_ref[...]

    def conv_silu(c0, width):
        acc = jnp.broadcast_to(cb_ref[:, c0:c0 + width], (L, width))
        for k in range(SSD_CONV):
            back = SSD_CONV - 1 - k
            acc = acc + ext_ref[halo - back:halo - back + L, c0:c0 + width] * cw_ref[k:k + 1, c0:c0 + width]
        return _silu(acc)

    dt = _softplus(dt_ref[...] + dtb_ref[...])
    a_neg = -jnp.exp(alog_ref[...])
    da = dt * a_neg
    rr = lax.broadcasted_iota(I32, (L, L), 0)
    cc = lax.broadcasted_iota(I32, (L, L), 1)
    lower = rr >= cc
    cs = _dot_f32_rhs(lower.astype(BF16), da)
    cs_last = cs[L - 1:L, :]
    cs_sq = jnp.concatenate([cs, jnp.zeros((L, L - n_heads), F32)], axis=1).T
    decay_to_end = jnp.exp(cs_last - cs)
    decay_from_start = jnp.exp(cs)
    chunk_decay = jnp.exp(cs_last)

    head_of_lane = lax.broadcasted_iota(I32, (n_heads, GW), 1) // P
    head_row = lax.broadcasted_iota(I32, (n_heads, GW), 0)
    lane_head = lax.broadcasted_iota(I32, (1, GW), 1) // P

    for g in range(G):
        spread = (head_row == head_of_lane + g * R).astype(BF16)
        dt_g = _dot_f32_lhs(dt, spread)
        to_end_g = _dot_f32_lhs(decay_to_end, spread)
        from_start_g = _dot_f32_lhs(decay_from_start, spread)
        rows8 = lambda v: jnp.broadcast_to(v, (SUBLANES, n_heads))
        chunk_decay_g = _dot_f32_lhs(rows8(chunk_decay), spread)[0:1]
        dskip_g = _dot_f32_lhs(rows8(dskip_ref[...]), spread)[0:1]

        x_g = conv_silu(g * GW, GW)
        b_g = conv_silu(d_inner + g * N, N)
        c_g = conv_silu(d_inner + G * N + g * N, N)
        xdt = x_g * dt_g
        b_bf = b_g.astype(BF16)
        c_bf = c_g.astype(BF16)

        cb = _dot(c_bf, b_bf, NT_DIMS)
        y_g = jnp.zeros((L, GW), F32)
        for r in range(R):
            hd = g * R + r
            seg = cs[:, hd:hd + 1] - cs_sq[hd:hd + 1, :]
            m = jnp.where(lower, cb * jnp.exp(seg), 0.0)
            x_r = jnp.where(lane_head == r, xdt, 0.0)
            y_g = y_g + _dot(m.astype(BF16), x_r.astype(BF16))
        prev = state_ref[g]
        y_g = y_g + from_start_g * _dot(c_bf, prev.astype(BF16))
        new_part = _dot(b_bf, (xdt * to_end_g).astype(BF16), TN_DIMS)
        state_ref[g] = prev * chunk_decay_g + new_part
        y_g = y_g + x_g * dskip_g

        yz = y_g * _silu(z_ref[:, g * GW:(g + 1) * GW])
        yn = yz * lax.rsqrt(jnp.mean(yz * yz, axis=-1, keepdims=True) + EPS)
        y_ref[:, g * GW:(g + 1) * GW] = (yn * nw_ref[:, g * GW:(g + 1) * GW]).astype(BF16)

    ext_ref[0:halo, :] = ext_ref[L:L + halo, :]


def _ssd_scan(xbc, z, dt, conv_w, conv_b, dt_bias, a_log, d_skip, norm_w, batch, seq):
    t, dx = xbc.shape
    d_inner = z.shape[1]
    n_heads = dt.shape[1]
    L = SSD_CHUNK
    nc = seq // L
    row = lambda n: pl.BlockSpec((1, n), lambda b, c: (0, 0))
    kern = functools.partial(_ssd_kernel, d_inner=d_inner, n_heads=n_heads)
    return pl.pallas_call(
        kern,
        grid=(batch, nc),
        in_specs=[
            pl.BlockSpec((L, dx), lambda b, c: (b * nc + c, 0)),
            pl.BlockSpec((L, d_inner), lambda b, c: (b * nc + c, 0)),
            pl.BlockSpec((L, n_heads), lambda b, c: (b * nc + c, 0)),
            pl.BlockSpec((SSD_CONV, dx), lambda b, c: (0, 0)),
            row(dx), row(n_heads), row(n_heads), row(n_heads), row(d_inner),
        ],
        out_specs=pl.BlockSpec((L, d_inner), lambda b, c: (b * nc + c, 0)),
        out_shape=jax.ShapeDtypeStruct((t, d_inner), BF16),
        scratch_shapes=[
            pltpu.VMEM((L + 2 * SUBLANES, dx), F32),
            pltpu.VMEM((SSD_GROUPS, SSD_STATE, d_inner // SSD_GROUPS), F32),
        ],
        compiler_params=_params("arbitrary", "arbitrary"),
        name="ssd_scan",
    )(xbc, z, dt, conv_w, conv_b, dt_bias, a_log, d_skip, norm_w)


def _rope_tables(seq):
    half = ATTN_HEAD_DIM // 2
    inv = ROPE_THETA ** (-jnp.arange(half, dtype=F32) / half)
    ang = jnp.arange(seq).astype(F32)[:, None] * inv[None, :]
    cos = jnp.cos(ang)
    sin = jnp.sin(ang)
    reps = LANES // ATTN_HEAD_DIM
    cos_t = jnp.tile(jnp.concatenate([cos, cos], axis=1), (1, reps))
    sin_t = jnp.tile(jnp.concatenate([-sin, sin], axis=1), (1, reps))
    return cos_t, sin_t


def kernel(x, mix_norm, ffn_norm, final_norm, attn_w_qkv, attn_w_o, ssd_w_in, ssd_conv_w, ssd_conv_b,
           ssd_dt_bias, ssd_a_log, ssd_d, ssd_norm, ssd_w_out, moe_w_group, moe_b_group, moe_w_router,
           moe_b_router, moe_w_up, moe_w_down):
    batch, seq, d = x.shape
    assert seq % MOBA_BLOCK == 0 and seq % SSD_CHUNK == 0 and d % LANES == 0
    depth = mix_norm.shape[0]
    t = batch * seq
    xt = x.reshape(t, d)
    cos_t, sin_t = _rope_tables(seq)
    final_w = final_norm[None, :]

    for i in range(depth):
        j = i // 2
        nw = mix_norm[i][None, :]
        if i % 2 == 0:
            q, k, v, kmean = _qkv_rope(xt, nw, attn_w_qkv[j].astype(BF16), cos_t, sin_t, seq)
            kmean = kmean.reshape(batch, seq // MOBA_BLOCK, d)
            o = _moba_attention(q, k, v, kmean, batch, seq)
            xt = _proj_residual(o, attn_w_o[j].astype(BF16), xt)
        else:
            w_in = ssd_w_in[j].astype(BF16)
            d_inner = ssd_norm.shape[1]
            dx = ssd_conv_b.shape[1]
            z, xbc, dt = _ssd_in_proj(xt, nw, w_in[:, 0:d_inner], w_in[:, d_inner:d_inner + dx],
                                      w_in[:, d_inner + dx:])
            y = _ssd_scan(xbc, z, dt, ssd_conv_w[j][:, 0, :], ssd_conv_b[j][None, :], ssd_dt_bias[j][None, :],
                          ssd_a_log[j][None, :], ssd_d[j][None, :], ssd_norm[j][None, :], batch, seq)
            xt = _proj_residual(y, ssd_w_out[j].astype(BF16), xt)
        xt = _hier_moe(xt, ffn_norm[i][None, :], moe_w_group[i], moe_b_group[i], moe_w_router[i],
                       moe_b_router[i], moe_w_up[i], moe_w_down[i], final_w, final_norm=(i == depth - 1))
    return xt.reshape(batch, seq, d)
```

```python
import functools
import math

import jax
import jax.numpy as jnp
from jax import lax
from jax.experimental import pallas as pl
from jax.experimental.pallas import tpu as pltpu

F32 = jnp.float32
BF16 = jnp.bfloat16
I32 = jnp.int32

EPS = 1e-6
LANES = 128
SUBLANES = 8
VMEM_LIMIT = 56 * 1024 * 1024

ATTN_HEADS = 16
ATTN_HEAD_DIM = 64
MOBA_BLOCK = 256
MOBA_TOPK = 3
ROPE_THETA = 10000.0
HEADS_PER_TILE = LANES // ATTN_HEAD_DIM

SSD_HEAD_DIM = 64
SSD_GROUPS = 8
SSD_STATE = 128
SSD_CONV = 4
SSD_CHUNK = 128

MOE_GROUPS = 4
MOE_EXPERTS_PER_GROUP = 8
MOE_EXPERTS = MOE_GROUPS * MOE_EXPERTS_PER_GROUP
MOE_ROW_BLOCK = 256
ROUTER_ROWS = SUBLANES + MOE_EXPERTS
ROUTER_TILE = 256
DISPATCH_TILE = 256
COMBINE_TILE = 128

NT_DIMS = (((1,), (1,)), ((), ()))
NN_DIMS = (((1,), (0,)), ((), ()))
TN_DIMS = (((0,), (0,)), ((), ()))


def _params(*sem):
    return pltpu.CompilerParams(dimension_semantics=sem, vmem_limit_bytes=VMEM_LIMIT)


def _rmsnorm_rows(x, w):
    var = jnp.mean(x * x, axis=-1, keepdims=True)
    return (x * lax.rsqrt(var + EPS)) * w


def _split3(x):
    hi = x.astype(BF16)
    r = x - hi.astype(F32)
    mid = r.astype(BF16)
    lo = (r - mid.astype(F32)).astype(BF16)
    return hi, mid, lo


def _dot(a, b, dims=NN_DIMS):
    return lax.dot_general(a, b, dims, preferred_element_type=F32)


def _dot_f32_lhs(x, e, dims=NN_DIMS):
    hi, mid, lo = _split3(x)
    return _dot(hi, e, dims) + _dot(mid, e, dims) + _dot(lo, e, dims)


def _dot_f32_rhs(e, x, dims=NN_DIMS):
    hi, mid, lo = _split3(x)
    return _dot(e, hi, dims) + _dot(e, mid, dims) + _dot(e, lo, dims)


def _dot_f32(a, b, dims=NN_DIMS):
    ah, am, al = _split3(a)
    bh, bm, bl = _split3(b)
    return (_dot(ah, bh, dims) + _dot(ah, bm, dims) + _dot(am, bh, dims)
            + _dot(ah, bl, dims) + _dot(am, bm, dims) + _dot(al, bh, dims))


def _silu(x):
    return x / (1.0 + jnp.exp(-x))


def _softplus(x):
    return jnp.maximum(x, 0.0) + jnp.log(1.0 + jnp.exp(-jnp.abs(x)))


def _qkv_kernel(x_ref, nw_ref, wqk_ref, wvt_ref, cos_ref, sin_ref, q_ref, k_ref, vt_ref, km_ref, *, d, scale):
    h = _rmsnorm_rows(x_ref[...], nw_ref[...]).astype(BF16)
    cos = cos_ref[...]
    sin = sin_ref[...]
    lane = lax.broadcasted_iota(I32, (1, LANES), 1)
    first_half = (lane % ATTN_HEAD_DIM) < (ATTN_HEAD_DIM // 2)

    def rope_tile(xc):
        lo_partner = pltpu.roll(xc, LANES - ATTN_HEAD_DIM // 2, 1)
        hi_partner = pltpu.roll(xc, ATTN_HEAD_DIM // 2, 1)
        return xc * cos + jnp.where(first_half, lo_partner, hi_partner) * sin

    q = _dot(h, wqk_ref[:, 0:d])
    k = _dot(h, wqk_ref[:, d:2 * d])
    for c in range(d // LANES):
        sl = slice(c * LANES, (c + 1) * LANES)
        q_ref[:, sl] = (rope_tile(q[:, sl]) * scale).astype(BF16)
        kc = rope_tile(k[:, sl])
        k_ref[:, sl] = kc.astype(BF16)
        km_ref[0, :, sl] = jnp.mean(kc, axis=0, keepdims=True)
    vt_ref[0] = _dot(wvt_ref[...], h, NT_DIMS).astype(BF16)


def _qkv_rope(x, norm_w, w_qk, w_vt, cos_t, sin_t, seq):
    t, d = x.shape
    tm = MOBA_BLOCK
    n_tiles = t // tm
    seq_tiles = seq // tm
    kern = functools.partial(_qkv_kernel, d=d, scale=math.log2(math.e) / math.sqrt(ATTN_HEAD_DIM))
    return pl.pallas_call(
        kern,
        grid=(n_tiles,),
        in_specs=[
            pl.BlockSpec((tm, d), lambda i: (i, 0)),
            pl.BlockSpec((1, d), lambda i: (0, 0)),
            pl.BlockSpec((d, 2 * d), lambda i: (0, 0), pipeline_mode=pl.Buffered(1)),
            pl.BlockSpec((d, d), lambda i: (0, 0), pipeline_mode=pl.Buffered(1)),
            pl.BlockSpec((tm, LANES), lambda i: (i % seq_tiles, 0)),
            pl.BlockSpec((tm, LANES), lambda i: (i % seq_tiles, 0)),
        ],
        out_specs=[
            pl.BlockSpec((tm, d), lambda i: (i, 0)),
            pl.BlockSpec((tm, d), lambda i: (i, 0)),
            pl.BlockSpec((1, d, tm), lambda i: (i, 0, 0)),
            pl.BlockSpec((1, 1, d), lambda i: (i, 0, 0)),
        ],
        out_shape=[
            jax.ShapeDtypeStruct((t, d), BF16),
            jax.ShapeDtypeStruct((t, d), BF16),
            jax.ShapeDtypeStruct((n_tiles, d, tm), BF16),
            jax.ShapeDtypeStruct((n_tiles, 1, d), F32),
        ],
        compiler_params=_params("parallel"),
        name="qkv_rope",
    )(x, norm_w, w_qk, w_vt, cos_t, sin_t)


def _attn_kernel(q_ref, k_ref, vt_ref, km_ref, o_ref, *, nb):
    qi = pl.program_id(2)
    tq = q_ref.shape[0]
    q2 = q_ref[...]
    km = km_ref[...]
    lane = lax.broadcasted_iota(I32, (1, LANES), 1)
    key = lax.broadcasted_iota(I32, (MOBA_BLOCK, tq), 0)
    qry = lax.broadcasted_iota(I32, (MOBA_BLOCK, tq), 1)
    causal = key <= qry
    blk = lax.broadcasted_iota(I32, (nb, tq), 0)
    neg_inf = jnp.float32(-jnp.inf)
    heads = range(HEADS_PER_TILE)

    qh = [jnp.where((lane // ATTN_HEAD_DIM) == h, q2, jnp.zeros_like(q2)) for h in heads]

    bits = []
    for h in heads:
        gate = _dot_f32_lhs(km, qh[h], NT_DIMS)
        alive = (blk < qi).astype(I32)
        sel = jnp.zeros((nb, tq), I32)
        for _ in range(min(MOBA_TOPK, nb)):
            gm = jnp.where(alive > 0, gate, neg_inf)
            top = jnp.max(gm, axis=0, keepdims=True)
            cand = jnp.where((alive > 0) & (gm == top), blk, nb)
            idx = jnp.min(cand, axis=0, keepdims=True)
            pick = (blk == idx).astype(I32)
            sel = sel | pick
            alive = alive & (1 - pick)
        bits.append(jnp.sum(sel << blk, axis=0, keepdims=True))

    def step(j, n_blocks, carry, diag):
        start = pl.multiple_of(j * MOBA_BLOCK, MOBA_BLOCK)
        kb = k_ref[pl.ds(start, n_blocks * MOBA_BLOCK), :]
        scores = [_dot(kb, qh[h], NT_DIMS) for h in heads]
        probs = []
        stats = []
        for h in heads:
            m_i, l_i, _ = carry[h]
            parts = []
            for b in range(n_blocks):
                s = scores[h][b * MOBA_BLOCK:(b + 1) * MOBA_BLOCK]
                keep = causal if diag else ((bits[h] >> (j + b)) & 1) == 1
                parts.append(jnp.where(keep, s, neg_inf))
            m_new = m_i
            for s in parts:
                m_new = jnp.maximum(m_new, jnp.max(s, axis=0, keepdims=True))
            alpha = jnp.exp2(m_i - m_new)
            l_new = alpha * l_i
            pb = []
            for s in parts:
                p = jnp.exp2(s - m_new)
                l_new = l_new + jnp.sum(p, axis=0, keepdims=True)
                pb.append(p.astype(BF16))
            probs.append(pb)
            stats.append((m_new, l_new, alpha))
        out = []
        for h in heads:
            m_new, l_new, alpha = stats[h]
            acc = alpha * carry[h][2]
            for b in range(n_blocks):
                vth = vt_ref[j + b, h * ATTN_HEAD_DIM:(h + 1) * ATTN_HEAD_DIM, :]
                acc = acc + _dot(vth, probs[h][b])
            out.append((m_new, l_new, acc))
        return tuple(out)

    init = tuple((jnp.full((1, tq), neg_inf, F32), jnp.zeros((1, tq), F32),
                  jnp.zeros((ATTN_HEAD_DIM, tq), F32)) for _ in heads)
    carry = step(qi, 1, init, True)
    carry = lax.fori_loop(0, (qi + 1) // 2, lambda jj, c: step(2 * jj, 2, c, False), carry)
    o_t = jnp.concatenate([acc / l_i for (_, l_i, acc) in carry], axis=0)
    o_ref[...] = o_t.T.astype(BF16)


def _moba_attention(q, k, vt, kmean, batch, seq):
    t, d = q.shape
    nb = seq // MOBA_BLOCK
    tq = MOBA_BLOCK
    nq = seq // tq
    col_tiles = d // LANES
    kern = functools.partial(_attn_kernel, nb=nb)
    return pl.pallas_call(
        kern,
        grid=(batch, col_tiles, nq),
        in_specs=[
            pl.BlockSpec((tq, LANES), lambda b, c, i: (b * nq + i, c)),
            pl.BlockSpec((seq, LANES), lambda b, c, i: (b, c)),
            pl.BlockSpec((nb, LANES, MOBA_BLOCK), lambda b, c, i: (b, c, 0)),
            pl.BlockSpec((None, nb, LANES), lambda b, c, i: (b, 0, c)),
        ],
        out_specs=pl.BlockSpec((tq, LANES), lambda b, c, i: (b * nq + i, c)),
        out_shape=jax.ShapeDtypeStruct((t, d), BF16),
        compiler_params=_params("parallel", "parallel", "arbitrary"),
        name="moba_attention",
    )(q, k, vt, kmean)


def _proj_res_kernel(a_ref, w_ref, res_ref, o_ref):
    o_ref[...] = res_ref[...] + _dot(a_ref[...], w_ref[...])


def _proj_residual(a, w, res, tm=512):
    t, kdim = a.shape
    d = w.shape[1]
    return pl.pallas_call(
        _proj_res_kernel,
        grid=(t // tm,),
        in_specs=[
            pl.BlockSpec((tm, kdim), lambda i: (i, 0)),
            pl.BlockSpec((kdim, d), lambda i: (0, 0), pipeline_mode=pl.Buffered(1)),
            pl.BlockSpec((tm, d), lambda i: (i, 0)),
        ],
        out_specs=pl.BlockSpec((tm, d), lambda i: (i, 0)),
        out_shape=jax.ShapeDtypeStruct((t, d), F32),
        compiler_params=_params("parallel"),
        name="proj_residual",
    )(a, w, res)


def _router_kernel(x_ref, nw_ref, wr_ref, br_ref, idx_ref, gate_ref, cnt_ref, carry_ref):
    tm = x_ref.shape[0]

    @pl.when(pl.program_id(0) == 0)
    def _():
        carry_ref[...] = jnp.zeros_like(carry_ref)

    hn = _rmsnorm_rows(x_ref[...], nw_ref[...])
    logits = _dot_f32(wr_ref[...], hn, NT_DIMS) + br_ref[...]
    sub = lax.broadcasted_iota(I32, (SUBLANES, tm), 0)
    neg_inf = jnp.float32(-jnp.inf)

    gl = logits[0:SUBLANES]
    gmax = jnp.max(gl, axis=0, keepdims=True)
    gidx = jnp.min(jnp.where(gl == gmax, sub, SUBLANES), axis=0, keepdims=True)
    gprob = 1.0 / jnp.sum(jnp.exp(gl - gmax), axis=0, keepdims=True)

    el = jnp.zeros((SUBLANES, tm), F32)
    for g in range(MOE_GROUPS):
        el = jnp.where(gidx == g, logits[SUBLANES * (g + 1):SUBLANES * (g + 2)], el)
    m1 = jnp.max(el, axis=0, keepdims=True)
    i1 = jnp.min(jnp.where(el == m1, sub, SUBLANES), axis=0, keepdims=True)
    el2 = jnp.where(sub == i1, neg_inf, el)
    m2 = jnp.max(el2, axis=0, keepdims=True)
    i2 = jnp.min(jnp.where(el2 == m2, sub, SUBLANES), axis=0, keepdims=True)
    ratio = jnp.exp(m2 - m1)
    den = 1.0 + ratio
    gate_ref[...] = jnp.zeros_like(gate_ref)
    gate_ref[0:1, :] = gprob / den
    gate_ref[1:2, :] = gprob * ratio / den

    e1 = gidx * MOE_EXPERTS_PER_GROUP + i1
    e2 = gidx * MOE_EXPERTS_PER_GROUP + i2

    eiota = lax.broadcasted_iota(I32, (MOE_EXPERTS, tm), 0)
    oh1 = eiota == e1
    oh2 = eiota == e2
    onehot = (oh1 | oh2).astype(BF16)
    src = lax.broadcasted_iota(I32, (tm, tm), 0)
    dst = lax.broadcasted_iota(I32, (tm, tm), 1)
    before = (src < dst).astype(BF16)
    prefix = _dot(onehot, before)
    total = _dot(onehot, jnp.ones((tm, tm), BF16))
    base = carry_ref[...] + prefix
    r1 = jnp.sum(jnp.where(oh1, base, 0.0), axis=0, keepdims=True)
    r2 = jnp.sum(jnp.where(oh2, base, 0.0), axis=0, keepdims=True)
    carry_ref[...] = carry_ref[...] + total
    cnt_ref[...] = carry_ref[...]

    idx_ref[...] = jnp.zeros_like(idx_ref)
    idx_ref[0:1, :] = e1
    idx_ref[1:2, :] = e2
    idx_ref[2:3, :] = r1.astype(I32)
    idx_ref[3:4, :] = r2.astype(I32)


def _router(x, norm_w, wr_t, br):
    t, d = x.shape
    tm = ROUTER_TILE
    return pl.pallas_call(
        _router_kernel,
        grid=(t // tm,),
        in_specs=[
            pl.BlockSpec((tm, d), lambda i: (i, 0)),
            pl.BlockSpec((1, d), lambda i: (0, 0)),
            pl.BlockSpec((ROUTER_ROWS, d), lambda i: (0, 0)),
            pl.BlockSpec((ROUTER_ROWS, 1), lambda i: (0, 0)),
        ],
        out_specs=[
            pl.BlockSpec((SUBLANES, tm), lambda i: (0, i)),
            pl.BlockSpec((SUBLANES, tm), lambda i: (0, i)),
            pl.BlockSpec((MOE_EXPERTS, tm), lambda i: (0, 0)),
        ],
        out_shape=[
            jax.ShapeDtypeStruct((SUBLANES, t), I32),
            jax.ShapeDtypeStruct((SUBLANES, t), F32),
            jax.ShapeDtypeStruct((MOE_EXPERTS, tm), F32),
        ],
        scratch_shapes=[pltpu.VMEM((MOE_EXPERTS, tm), F32)],
        compiler_params=_params("arbitrary"),
        name="moe_router",
    )(x, norm_w, wr_t, br)


def _row_copy(src_ref, src_row, dst_ref, dst_row, sem):
    return pltpu.make_async_copy(src_ref.at[pl.ds(src_row, 1)], dst_ref.at[pl.ds(dst_row, 1)], sem)


def _dispatch_kernel(pstart_ref, x_ref, nw_ref, idx_ref, rows_in_ref, rows_ref, hn_ref, sem):
    del rows_in_ref
    tm = x_ref.shape[0]
    hn_ref[...] = _rmsnorm_rows(x_ref[...], nw_ref[...])

    def slot_row(t, s):
        return pstart_ref[idx_ref[s, t]] + idx_ref[2 + s, t]

    def issue(t, c):
        for s in range(2):
            _row_copy(hn_ref, t, rows_ref, slot_row(t, s), sem.at[s]).start()
        return c

    lax.fori_loop(0, tm, issue, 0)

    def drain(t, c):
        for s in range(2):
            _row_copy(hn_ref, t, rows_ref, slot_row(t, s), sem.at[s]).wait()
        return c

    lax.fori_loop(0, tm, drain, 0)


def _dispatch(pstart, x, norm_w, idx, cap):
    t, d = x.shape
    tm = DISPATCH_TILE
    rows0 = jnp.zeros((cap, d), F32)
    grid_spec = pltpu.PrefetchScalarGridSpec(
        num_scalar_prefetch=1,
        grid=(t // tm,),
        in_specs=[
            pl.BlockSpec((tm, d), lambda i, ps: (i, 0)),
            pl.BlockSpec((1, d), lambda i, ps: (0, 0)),
            pl.BlockSpec((SUBLANES, tm), lambda i, ps: (0, i), memory_space=pltpu.SMEM),
            pl.BlockSpec(memory_space=pl.ANY),
        ],
        out_specs=pl.BlockSpec(memory_space=pl.ANY),
        scratch_shapes=[pltpu.VMEM((tm, d), F32), pltpu.SemaphoreType.DMA((2,))],
    )
    return pl.pallas_call(
        _dispatch_kernel,
        grid_spec=grid_spec,
        out_shape=jax.ShapeDtypeStruct((cap, d), F32),
        input_output_aliases={4: 0},
        compiler_params=_params("arbitrary"),
        name="moe_dispatch",
    )(pstart, x, norm_w, idx, rows0)


def _expert_kernel(be_ref, rows_ref, wu_ref, wd_ref, y_ref):
    del be_ref
    ff = wd_ref.shape[0]
    xb = rows_ref[...].astype(BF16)
    gu = _dot(xb, wu_ref[...])
    act = _silu(gu[:, 0:ff]) * gu[:, ff:2 * ff]
    y_ref[...] = _dot(act.astype(BF16), wd_ref[...])


def _experts(block_expert, rows, w_up, w_down):
    cap, d = rows.shape
    rb = MOE_ROW_BLOCK
    ff = w_down.shape[1]
    grid_spec = pltpu.PrefetchScalarGridSpec(
        num_scalar_prefetch=1,
        grid=(cap // rb,),
        in_specs=[
            pl.BlockSpec((rb, d), lambda i, be: (i, 0)),
            pl.BlockSpec((None, d, 2 * ff), lambda i, be: (be[i], 0, 0)),
            pl.BlockSpec((None, ff, d), lambda i, be: (be[i], 0, 0)),
        ],
        out_specs=pl.BlockSpec((rb, d), lambda i, be: (i, 0)),
    )
    return pl.pallas_call(
        _expert_kernel,
        grid_spec=grid_spec,
        out_shape=jax.ShapeDtypeStruct((cap, d), F32),
        compiler_params=_params("arbitrary"),
        name="moe_experts",
    )(block_expert, rows, w_up, w_down)


def _combine_kernel(pstart_ref, x_ref, gate_ref, idx_ref, fw_ref, y_ref, o_ref, buf_ref, sem, *, final_norm):
    tm = x_ref.shape[0]

    def slot_row(t, s):
        return pstart_ref[idx_ref[s, t]] + idx_ref[2 + s, t]

    def issue(t, c):
        for s in range(2):
            _row_copy(y_ref, slot_row(t, s), buf_ref.at[s], t, sem.at[s]).start()
        return c

    lax.fori_loop(0, tm, issue, 0)

    def drain(t, c):
        for s in range(2):
            _row_copy(y_ref, slot_row(t, s), buf_ref.at[s], t, sem.at[s]).wait()
        return c

    lax.fori_loop(0, tm, drain, 0)

    g = gate_ref[...]
    gsq = jnp.concatenate([g, jnp.zeros((tm - SUBLANES, tm), F32)], axis=0).T
    out = x_ref[...] + gsq[:, 0:1] * buf_ref[0] + gsq[:, 1:2] * buf_ref[1]
    if final_norm:
        out = _rmsnorm_rows(out, fw_ref[...])
    o_ref[...] = out


def _combine(pstart, x, gates, idx, y_rows, final_w, final_norm):
    t, d = x.shape
    tm = COMBINE_TILE
    grid_spec = pltpu.PrefetchScalarGridSpec(
        num_scalar_prefetch=1,
        grid=(t // tm,),
        in_specs=[
            pl.BlockSpec((tm, d), lambda i, ps: (i, 0)),
            pl.BlockSpec((SUBLANES, tm), lambda i, ps: (0, i)),
            pl.BlockSpec((SUBLANES, tm), lambda i, ps: (0, i), memory_space=pltpu.SMEM),
            pl.BlockSpec((1, d), lambda i, ps: (0, 0)),
            pl.BlockSpec(memory_space=pl.ANY),
        ],
        out_specs=pl.BlockSpec((tm, d), lambda i, ps: (i, 0)),
        scratch_shapes=[pltpu.VMEM((2, tm, d), F32), pltpu.SemaphoreType.DMA((2,))],
    )
    return pl.pallas_call(
        functools.partial(_combine_kernel, final_norm=final_norm),
        grid_spec=grid_spec,
        out_shape=jax.ShapeDtypeStruct((t, d), F32),
        compiler_params=_params("arbitrary"),
        name="moe_combine",
    )(pstart, x, gates, idx, final_w, y_rows)


def _hier_moe(x, norm_w, w_group, b_group, w_router, b_router, w_up, w_down, final_w, final_norm):
    t, d = x.shape
    g, _, epg = w_router.shape
    pad_rows = SUBLANES - g
    wr_t = jnp.concatenate([w_group.T, jnp.zeros((pad_rows, d), F32),
                            w_router.transpose(0, 2, 1).reshape(g * epg, d)], axis=0)
    br = jnp.concatenate([b_group, jnp.full((pad_rows,), -1e30, F32), b_router.reshape(-1)])[:, None]
    idx, gates, cnt = _router(x, norm_w, wr_t, br)

    rb = MOE_ROW_BLOCK
    n_exp = g * epg
    counts = cnt[:, 0].astype(I32)
    padded = ((counts + rb - 1) // rb) * rb
    pends = jnp.cumsum(padded)
    pstart = (pends - padded).astype(I32)
    n_blocks = (2 * t) // rb + n_exp
    block_start = jnp.arange(n_blocks, dtype=I32) * rb
    block_expert = jnp.minimum(jnp.sum(block_start[:, None] >= pends[None, :], axis=1), n_exp - 1).astype(I32)

    rows = _dispatch(pstart, x, norm_w, idx, n_blocks * rb)
    y_rows = _experts(block_expert, rows, w_up.astype(BF16), w_down.astype(BF16))
    return _combine(pstart, x, gates, idx, y_rows, final_w, final_norm)


def _ssd_in_kernel(x_ref, nw_ref, wz_ref, wx_ref, wdt_ref, z_ref, xbc_ref, dt_ref):
    h = _rmsnorm_rows(x_ref[...], nw_ref[...]).astype(BF16)
    z_ref[...] = _dot(h, wz_ref[...])
    xbc_ref[...] = _dot(h, wx_ref[...])
    dt_ref[...] = _dot(h, wdt_ref[...])


def _ssd_in_proj(x, norm_w, w_z, w_xbc, w_dt, tm=256):
    t, d = x.shape
    dz, dx, dh = w_z.shape[1], w_xbc.shape[1], w_dt.shape[1]
    resident = lambda n: pl.BlockSpec((d, n), lambda i: (0, 0), pipeline_mode=pl.Buffered(1))
    return pl.pallas_call(
        _ssd_in_kernel,
        grid=(t // tm,),
        in_specs=[
            pl.BlockSpec((tm, d), lambda i: (i, 0)),
            pl.BlockSpec((1, d), lambda i: (0, 0)),
            resident(dz), resident(dx), resident(dh),
        ],
        out_specs=[
            pl.BlockSpec((tm, dz), lambda i: (i, 0)),
            pl.BlockSpec((tm, dx), lambda i: (i, 0)),
            pl.BlockSpec((tm, dh), lambda i: (i, 0)),
        ],
        out_shape=[
            jax.ShapeDtypeStruct((t, dz), F32),
            jax.ShapeDtypeStruct((t, dx), F32),
            jax.ShapeDtypeStruct((t, dh), F32),
        ],
        compiler_params=_params("parallel"),
        name="ssd_in_proj",
    )(x, norm_w, w_z, w_xbc, w_dt)


def _ssd_kernel(xbc_ref, z_ref, dt_ref, cw_ref, cb_ref, dtb_ref, alog_ref, dskip_ref, nw_ref,
                y_ref, ext_ref, state_ref, *, d_inner, n_heads):
    L = SSD_CHUNK
    G = SSD_GROUPS
    N = SSD_STATE
    P = SSD_HEAD_DIM
    R = n_heads // G
    GW = R * P
    halo = SUBLANES

    @pl.when(pl.program_id(1) == 0)
    def _():
        ext_ref[0:halo, :] = jnp.zeros((halo, ext_ref.shape[1]), F32)
        state_ref[...] = jnp.zeros_like(state_ref)

    ext_ref[halo:halo + L, :] = xbc_ref[...]

    def conv_silu(c0, width):
        acc = jnp.broadcast_to(cb_ref[:, c0:c0 + width], (L, width))
        for k in range(SSD_CONV):
            back = SSD_CONV - 1 - k
            acc = acc + ext_ref[halo - back:halo - back + L, c0:c0 + width] * cw_ref[k:k + 1, c0:c0 + width]
        return _silu(acc)

    dt = _softplus(dt_ref[...] + dtb_ref[...])
    a_neg = -jnp.exp(alog_ref[...])
    da = dt * a_neg
    rr = lax.broadcasted_iota(I32, (L, L), 0)
    cc = lax.broadcasted_iota(I32, (L, L), 1)
    lower = rr >= cc
    cs = _dot_f32_rhs(lower.astype(BF16), da)
    cs_last = cs[L - 1:L, :]
    cs_sq = jnp.concatenate([cs, jnp.zeros((L, L - n_heads), F32)], axis=1).T
    decay_to_end = jnp.exp(cs_last - cs)
    decay_from_start = jnp.exp(cs)
    chunk_decay = jnp.exp(cs_last)

    head_of_lane = lax.broadcasted_iota(I32, (n_heads, GW), 1) // P
    head_row = lax.broadcasted_iota(I32, (n_heads, GW), 0)
    lane_head = lax.broadcasted_iota(I32, (1, GW), 1) // P

    for g in range(G):
        spread = (head_row == head_of_lane + g * R).astype(BF16)
        dt_g = _dot_f32_lhs(dt, spread)
        to_end_g = _dot_f32_lhs(decay_to_end, spread)
        from_start_g = _dot_f32_lhs(decay_from_start, spread)
        rows8 = lambda v: jnp.broadcast_to(v, (SUBLANES, n_heads))
        chunk_decay_g = _dot_f32_lhs(rows8(chunk_decay), spread)[0:1]
        dskip_g = _dot_f32_lhs(rows8(dskip_ref[...]), spread)[0:1]

        x_g = conv_silu(g * GW, GW)
        b_g = conv_silu(d_inner + g * N, N)
        c_g = conv_silu(d_inner + G * N + g * N, N)
        xdt = x_g * dt_g
        b_bf = b_g.astype(BF16)
        c_bf = c_g.astype(BF16)

        cb = _dot(c_bf, b_bf, NT_DIMS)
        y_g = jnp.zeros((L, GW), F32)
        for r in range(R):
            hd = g * R + r
            seg = cs[:, hd:hd + 1] - cs_sq[hd:hd + 1, :]
            m = jnp.where(lower, cb * jnp.exp(seg), 0.0)
            x_r = jnp.where(lane_head == r, xdt, 0.0)
            y_g = y_g + _dot(m.astype(BF16), x_r.astype(BF16))
        prev = state_ref[g]
        y_g = y_g + from_start_g * _dot(c_bf, prev.astype(BF16))
        new_part = _dot(b_bf, (xdt * to_end_g).astype(BF16), TN_DIMS)
        state_ref[g] = prev * chunk_decay_g + new_part
        y_g = y_g + x_g * dskip_g

        yz = y_g * _silu(z_ref[:, g * GW:(g + 1) * GW])
        yn = yz * lax.rsqrt(jnp.mean(yz * yz, axis=-1, keepdims=True) + EPS)
        y_ref[:, g * GW:(g + 1) * GW] = (yn * nw_ref[:, g * GW:(g + 1) * GW]).astype(BF16)

    ext_ref[0:halo, :] = ext_ref[L:L + halo, :]


def _ssd_scan(xbc, z, dt, conv_w, conv_b, dt_bias, a_log, d_skip, norm_w, batch, seq):
    t, dx = xbc.shape
    d_inner = z.shape[1]
    n_heads = dt.shape[1]
    L = SSD_CHUNK
    nc = seq // L
    row = lambda n: pl.BlockSpec((1, n), lambda b, c: (0, 0))
    kern = functools.partial(_ssd_kernel, d_inner=d_inner, n_heads=n_heads)
    return pl.pallas_call(
        kern,
        grid=(batch, nc),
        in_specs=[
            pl.BlockSpec((L, dx), lambda b, c: (b * nc + c, 0)),
            pl.BlockSpec((L, d_inner), lambda b, c: (b * nc + c, 0)),
            pl.BlockSpec((L, n_heads), lambda b, c: (b * nc + c, 0)),
            pl.BlockSpec((SSD_CONV, dx), lambda b, c: (0, 0)),
            row(dx), row(n_heads), row(n_heads), row(n_heads), row(d_inner),
        ],
        out_specs=pl.BlockSpec((L, d_inner), lambda b, c: (b * nc + c, 0)),
        out_shape=jax.ShapeDtypeStruct((t, d_inner), BF16),
        scratch_shapes=[
            pltpu.VMEM((L + 2 * SUBLANES, dx), F32),
            pltpu.VMEM((SSD_GROUPS, SSD_STATE, d_inner // SSD_GROUPS), F32),
        ],
        compiler_params=_params("arbitrary", "arbitrary"),
        name="ssd_scan",
    )(xbc, z, dt, conv_w, conv_b, dt_bias, a_log, d_skip, norm_w)


def _rope_tables(seq):
    half = ATTN_HEAD_DIM // 2
    inv = ROPE_THETA ** (-jnp.arange(half, dtype=F32) / half)
    ang = jnp.arange(seq).astype(F32)[:, None] * inv[None, :]
    cos = jnp.cos(ang)
    sin = jnp.sin(ang)
    reps = LANES // ATTN_HEAD_DIM
    cos_t = jnp.tile(jnp.concatenate([cos, cos], axis=1), (1, reps))
    sin_t = jnp.tile(jnp.concatenate([-sin, sin], axis=1), (1, reps))
    return cos_t, sin_t


def kernel(x, mix_norm, ffn_norm, final_norm, attn_w_qkv, attn_w_o, ssd_w_in, ssd_conv_w, ssd_conv_b,
           ssd_dt_bias, ssd_a_log, ssd_d, ssd_norm, ssd_w_out, moe_w_group, moe_b_group, moe_w_router,
           moe_b_router, moe_w_up, moe_w_down):
    batch, seq, d = x.shape
    assert seq % MOBA_BLOCK == 0 and seq % SSD_CHUNK == 0 and d % LANES == 0
    depth = mix_norm.shape[0]
    t = batch * seq
    xt = x.reshape(t, d)
    cos_t, sin_t = _rope_tables(seq)
    final_w = final_norm[None, :]

    for i in range(depth):
        j = i // 2
        nw = mix_norm[i][None, :]
        if i % 2 == 0:
            w_qkv = attn_w_qkv[j].astype(BF16)
            q, k, vt, kmean = _qkv_rope(xt, nw, w_qkv[:, 0:2 * d], w_qkv[:, 2 * d:].T, cos_t, sin_t, seq)
            kmean = kmean.reshape(batch, seq // MOBA_BLOCK, d)
            o = _moba_attention(q, k, vt, kmean, batch, seq)
            xt = _proj_residual(o, attn_w_o[j].astype(BF16), xt)
        else:
            w_in = ssd_w_in[j].astype(BF16)
            d_inner = ssd_norm.shape[1]
            dx = ssd_conv_b.shape[1]
            z, xbc, dt = _ssd_in_proj(xt, nw, w_in[:, 0:d_inner], w_in[:, d_inner:d_inner + dx],
                                      w_in[:, d_inner + dx:])
            y = _ssd_scan(xbc, z, dt, ssd_conv_w[j][:, 0, :], ssd_conv_b[j][None, :], ssd_dt_bias[j][None, :],
                          ssd_a_log[j][None, :], ssd_d[j][None, :], ssd_norm[j][None, :], batch, seq)
            xt = _proj_residual(y, ssd_w_out[j].astype(BF16), xt)
        xt = _hier_moe(xt, ffn_norm[i][None, :], moe_w_group[i], moe_b_group[i], moe_w_router[i],
                       moe_b_router[i], moe_w_up[i], moe_w_down[i], final_w, final_norm=(i == depth - 1))
    return xt.reshape(batch, seq, d)
```

```python
import functools
import math

import jax
import jax.numpy as jnp
from jax import lax
from jax.experimental import pallas as pl
from jax.experimental.pallas import tpu as pltpu

F32 = jnp.float32
BF16 = jnp.bfloat16
I32 = jnp.int32

EPS = 1e-6
LANES = 128
SUBLANES = 8
VMEM_LIMIT = 56 * 1024 * 1024

ATTN_HEADS = 16
ATTN_HEAD_DIM = 64
MOBA_BLOCK = 256
MOBA_TOPK = 3
ROPE_THETA = 10000.0
HEADS_PER_TILE = LANES // ATTN_HEAD_DIM

SSD_HEAD_DIM = 64
SSD_GROUPS = 8
SSD_STATE = 128
SSD_CONV = 4
SSD_CHUNK = 128

MOE_GROUPS = 4
MOE_EXPERTS_PER_GROUP = 8
MOE_EXPERTS = MOE_GROUPS * MOE_EXPERTS_PER_GROUP
MOE_ROW_BLOCK = 256
ROUTER_ROWS = SUBLANES + MOE_EXPERTS
MOE_TILE = 256
RUN_ALIGN = SUBLANES
RUN_SIZES = (16, 8)
MOE_TILE_BUF = 2 * MOE_TILE + MOE_EXPERTS * RUN_ALIGN

NT_DIMS = (((1,), (1,)), ((), ()))
NN_DIMS = (((1,), (0,)), ((), ()))
TN_DIMS = (((0,), (0,)), ((), ()))


def _params(*sem):
    return pltpu.CompilerParams(dimension_semantics=sem, vmem_limit_bytes=VMEM_LIMIT)


def _rmsnorm_rows(x, w):
    var = jnp.mean(x * x, axis=-1, keepdims=True)
    return (x * lax.rsqrt(var + EPS)) * w


def _split3(x):
    hi = x.astype(BF16)
    r = x - hi.astype(F32)
    mid = r.astype(BF16)
    lo = (r - mid.astype(F32)).astype(BF16)
    return hi, mid, lo


def _dot(a, b, dims=NN_DIMS):
    return lax.dot_general(a, b, dims, preferred_element_type=F32)


def _dot_f32_lhs(x, e, dims=NN_DIMS):
    hi, mid, lo = _split3(x)
    return _dot(hi, e, dims) + _dot(mid, e, dims) + _dot(lo, e, dims)


def _dot_f32_rhs(e, x, dims=NN_DIMS):
    hi, mid, lo = _split3(x)
    return _dot(e, hi, dims) + _dot(e, mid, dims) + _dot(e, lo, dims)


def _dot_f32(a, b, dims=NN_DIMS):
    ah, am, al = _split3(a)
    bh, bm, bl = _split3(b)
    return (_dot(ah, bh, dims) + _dot(ah, bm, dims) + _dot(am, bh, dims)
            + _dot(ah, bl, dims) + _dot(am, bm, dims) + _dot(al, bh, dims))


def _silu(x):
    return x * (0.5 * jnp.tanh(0.5 * x) + 0.5)


def _softplus(x):
    return jnp.maximum(x, 0.0) + jnp.log(1.0 + jnp.exp(-jnp.abs(x)))


def _qkv_kernel(x_ref, nw_ref, wqk_ref, wvt_ref, cos_ref, sin_ref, q_ref, k_ref, vt_ref, km_ref, *, d, scale):
    h = _rmsnorm_rows(x_ref[...], nw_ref[...]).astype(BF16)
    cos = cos_ref[...]
    sin = sin_ref[...]
    lane = lax.broadcasted_iota(I32, (1, LANES), 1)
    first_half = (lane % ATTN_HEAD_DIM) < (ATTN_HEAD_DIM // 2)

    def rope_tile(xc):
        lo_partner = pltpu.roll(xc, LANES - ATTN_HEAD_DIM // 2, 1)
        hi_partner = pltpu.roll(xc, ATTN_HEAD_DIM // 2, 1)
        return xc * cos + jnp.where(first_half, lo_partner, hi_partner) * sin

    q = _dot(h, wqk_ref[:, 0:d])
    k = _dot(h, wqk_ref[:, d:2 * d])
    for c in range(d // LANES):
        sl = slice(c * LANES, (c + 1) * LANES)
        q_ref[:, sl] = (rope_tile(q[:, sl]) * scale).astype(BF16)
        kc = rope_tile(k[:, sl])
        k_ref[:, sl] = kc.astype(BF16)
        km_ref[0, :, sl] = jnp.mean(kc, axis=0, keepdims=True)
    vt_ref[0] = _dot(wvt_ref[...], h, NT_DIMS).astype(BF16)


def _qkv_rope(x, norm_w, w_qk, w_vt, cos_t, sin_t, seq):
    t, d = x.shape
    tm = MOBA_BLOCK
    n_tiles = t // tm
    seq_tiles = seq // tm
    kern = functools.partial(_qkv_kernel, d=d, scale=math.log2(math.e) / math.sqrt(ATTN_HEAD_DIM))
    return pl.pallas_call(
        kern,
        grid=(n_tiles,),
        in_specs=[
            pl.BlockSpec((tm, d), lambda i: (i, 0)),
            pl.BlockSpec((1, d), lambda i: (0, 0)),
            pl.BlockSpec((d, 2 * d), lambda i: (0, 0), pipeline_mode=pl.Buffered(1)),
            pl.BlockSpec((d, d), lambda i: (0, 0), pipeline_mode=pl.Buffered(1)),
            pl.BlockSpec((tm, LANES), lambda i: (i % seq_tiles, 0)),
            pl.BlockSpec((tm, LANES), lambda i: (i % seq_tiles, 0)),
        ],
        out_specs=[
            pl.BlockSpec((tm, d), lambda i: (i, 0)),
            pl.BlockSpec((tm, d), lambda i: (i, 0)),
            pl.BlockSpec((1, d, tm), lambda i: (i, 0, 0)),
            pl.BlockSpec((1, 1, d), lambda i: (i, 0, 0)),
        ],
        out_shape=[
            jax.ShapeDtypeStruct((t, d), BF16),
            jax.ShapeDtypeStruct((t, d), BF16),
            jax.ShapeDtypeStruct((n_tiles, d, tm), BF16),
            jax.ShapeDtypeStruct((n_tiles, 1, d), F32),
        ],
        compiler_params=_params("parallel"),
        name="qkv_rope",
    )(x, norm_w, w_qk, w_vt, cos_t, sin_t)


def _attn_kernel(q_ref, k_ref, vt_ref, km_ref, o_ref, *, nb):
    qi = pl.program_id(2)
    tq = q_ref.shape[0]
    q2 = q_ref[...]
    km = km_ref[...]
    lane = lax.broadcasted_iota(I32, (1, LANES), 1)
    key = lax.broadcasted_iota(I32, (MOBA_BLOCK, tq), 0)
    qry = lax.broadcasted_iota(I32, (MOBA_BLOCK, tq), 1)
    causal = key <= qry
    blk = lax.broadcasted_iota(I32, (nb, tq), 0)
    neg_inf = jnp.float32(-jnp.inf)
    heads = range(HEADS_PER_TILE)

    qh = [jnp.where((lane // ATTN_HEAD_DIM) == h, q2, jnp.zeros_like(q2)) for h in heads]

    bits = []
    for h in heads:
        gate = _dot_f32_lhs(km, qh[h], NT_DIMS)
        alive = (blk < qi).astype(I32)
        sel = jnp.zeros((nb, tq), I32)
        for _ in range(min(MOBA_TOPK, nb)):
            gm = jnp.where(alive > 0, gate, neg_inf)
            top = jnp.max(gm, axis=0, keepdims=True)
            cand = jnp.where((alive > 0) & (gm == top), blk, nb)
            idx = jnp.min(cand, axis=0, keepdims=True)
            pick = (blk == idx).astype(I32)
            sel = sel | pick
            alive = alive & (1 - pick)
        bits.append(jnp.sum(sel << blk, axis=0, keepdims=True))

    def step(j, n_blocks, carry, diag):
        start = pl.multiple_of(j * MOBA_BLOCK, MOBA_BLOCK)
        kb = k_ref[pl.ds(start, n_blocks * MOBA_BLOCK), :]
        scores = [_dot(kb, qh[h], NT_DIMS) for h in heads]
        probs = []
        stats = []
        for h in heads:
            m_i, l_i, _ = carry[h]
            parts = []
            for b in range(n_blocks):
                s = scores[h][b * MOBA_BLOCK:(b + 1) * MOBA_BLOCK]
                keep = causal if diag else ((bits[h] >> (j + b)) & 1) == 1
                parts.append(jnp.where(keep, s, neg_inf))
            m_new = m_i
            for s in parts:
                m_new = jnp.maximum(m_new, jnp.max(s, axis=0, keepdims=True))
            alpha = jnp.exp2(m_i - m_new)
            l_new = alpha * l_i
            pb = []
            for s in parts:
                p = jnp.exp2(s - m_new)
                l_new = l_new + jnp.sum(p, axis=0, keepdims=True)
                pb.append(p.astype(BF16))
            probs.append(pb)
            stats.append((m_new, l_new, alpha))
        out = []
        for h in heads:
            m_new, l_new, alpha = stats[h]
            acc = alpha * carry[h][2]
            for b in range(n_blocks):
                vth = vt_ref[j + b, h * ATTN_HEAD_DIM:(h + 1) * ATTN_HEAD_DIM, :]
                acc = acc + _dot(vth, probs[h][b])
            out.append((m_new, l_new, acc))
        return tuple(out)

    init = tuple((jnp.full((1, tq), neg_inf, F32), jnp.zeros((1, tq), F32),
                  jnp.zeros((ATTN_HEAD_DIM, tq), F32)) for _ in heads)
    carry = step(qi, 1, init, True)
    carry = lax.fori_loop(0, (qi + 1) // 2, lambda jj, c: step(2 * jj, 2, c, False), carry)
    o_t = jnp.concatenate([acc / l_i for (_, l_i, acc) in carry], axis=0)
    o_ref[...] = o_t.T.astype(BF16)


def _moba_attention(q, k, vt, kmean, batch, seq):
    t, d = q.shape
    nb = seq // MOBA_BLOCK
    tq = MOBA_BLOCK
    nq = seq // tq
    col_tiles = d // LANES
    kern = functools.partial(_attn_kernel, nb=nb)
    return pl.pallas_call(
        kern,
        grid=(batch, col_tiles, nq),
        in_specs=[
            pl.BlockSpec((tq, LANES), lambda b, c, i: (b * nq + i, c)),
            pl.BlockSpec((seq, LANES), lambda b, c, i: (b, c)),
            pl.BlockSpec((nb, LANES, MOBA_BLOCK), lambda b, c, i: (b, c, 0)),
            pl.BlockSpec((None, nb, LANES), lambda b, c, i: (b, 0, c)),
        ],
        out_specs=pl.BlockSpec((tq, LANES), lambda b, c, i: (b * nq + i, c)),
        out_shape=jax.ShapeDtypeStruct((t, d), BF16),
        compiler_params=_params("parallel", "parallel", "arbitrary"),
        name="moba_attention",
    )(q, k, vt, kmean)


def _proj_res_kernel(a_ref, w_ref, res_ref, o_ref):
    o_ref[...] = res_ref[...] + _dot(a_ref[...], w_ref[...])


def _proj_residual(a, w, res, tm=512):
    t, kdim = a.shape
    d = w.shape[1]
    return pl.pallas_call(
        _proj_res_kernel,
        grid=(t // tm,),
        in_specs=[
            pl.BlockSpec((tm, kdim), lambda i: (i, 0)),
            pl.BlockSpec((kdim, d), lambda i: (0, 0), pipeline_mode=pl.Buffered(1)),
            pl.BlockSpec((tm, d), lambda i: (i, 0)),
        ],
        out_specs=pl.BlockSpec((tm, d), lambda i: (i, 0)),
        out_shape=jax.ShapeDtypeStruct((t, d), F32),
        compiler_params=_params("parallel"),
        name="proj_residual",
    )(a, w, res)


def _router_kernel(x_ref, nw_ref, wr_ref, br_ref, gate_ref, tinfo_ref, cnt_ref, carry_ref):
    tm = x_ref.shape[0]

    @pl.when(pl.program_id(0) == 0)
    def _():
        carry_ref[...] = jnp.zeros_like(carry_ref)

    hn = _rmsnorm_rows(x_ref[...], nw_ref[...])
    logits = _dot_f32(wr_ref[...], hn, NT_DIMS) + br_ref[...]
    sub = lax.broadcasted_iota(I32, (SUBLANES, tm), 0)
    neg_inf = jnp.float32(-jnp.inf)

    gl = logits[0:SUBLANES]
    gmax = jnp.max(gl, axis=0, keepdims=True)
    gidx = jnp.min(jnp.where(gl == gmax, sub, SUBLANES), axis=0, keepdims=True)
    gprob = 1.0 / jnp.sum(jnp.exp(gl - gmax), axis=0, keepdims=True)

    el = jnp.zeros((SUBLANES, tm), F32)
    for g in range(MOE_GROUPS):
        el = jnp.where(gidx == g, logits[SUBLANES * (g + 1):SUBLANES * (g + 2)], el)
    m1 = jnp.max(el, axis=0, keepdims=True)
    i1 = jnp.min(jnp.where(el == m1, sub, SUBLANES), axis=0, keepdims=True)
    el2 = jnp.where(sub == i1, neg_inf, el)
    m2 = jnp.max(el2, axis=0, keepdims=True)
    i2 = jnp.min(jnp.where(el2 == m2, sub, SUBLANES), axis=0, keepdims=True)
    ratio = jnp.exp(m2 - m1)
    den = 1.0 + ratio
    gate_ref[...] = jnp.zeros_like(gate_ref)
    gate_ref[0:1, :] = gprob / den
    gate_ref[1:2, :] = gprob * ratio / den

    e1 = gidx * MOE_EXPERTS_PER_GROUP + i1
    e2 = gidx * MOE_EXPERTS_PER_GROUP + i2

    eiota = lax.broadcasted_iota(I32, (MOE_EXPERTS, tm), 0)
    oh1 = eiota == e1
    oh2 = eiota == e2
    onehot = (oh1 | oh2).astype(BF16)
    src = lax.broadcasted_iota(I32, (tm, tm), 0)
    dst = lax.broadcasted_iota(I32, (tm, tm), 1)
    before = (src < dst).astype(BF16)
    prefix = _dot(onehot, before)
    count = _dot(onehot, jnp.ones((tm, tm), BF16))
    total = jnp.floor((count + (RUN_ALIGN - 1)) * (1.0 / RUN_ALIGN)) * RUN_ALIGN
    erow = lax.broadcasted_iota(I32, (MOE_EXPERTS, MOE_EXPERTS), 0)
    ecol = lax.broadcasted_iota(I32, (MOE_EXPERTS, MOE_EXPERTS), 1)
    run_start = _dot_f32_rhs((ecol < erow).astype(BF16), total)
    slot = run_start + prefix
    gate_ref[2:3, :] = jnp.sum(jnp.where(oh1, slot, 0.0), axis=0, keepdims=True)
    gate_ref[3:4, :] = jnp.sum(jnp.where(oh2, slot, 0.0), axis=0, keepdims=True)

    lane = lax.broadcasted_iota(I32, (MOE_EXPERTS, tm), 1)
    on_lanes = lambda v: jnp.sum(jnp.where(eiota == lane, v, 0.0), axis=0, keepdims=True)[:, 0:LANES].astype(I32)
    tinfo_ref[...] = jnp.zeros_like(tinfo_ref)
    tinfo_ref[0, 0:1, :] = on_lanes(total)
    tinfo_ref[0, 1:2, :] = on_lanes(carry_ref[...])
    tinfo_ref[0, 2:3, :] = on_lanes(run_start)
    carry_ref[...] = carry_ref[...] + total
    cnt_ref[...] = carry_ref[...]


def _router(x, norm_w, wr_t, br):
    t, d = x.shape
    tm = MOE_TILE
    return pl.pallas_call(
        _router_kernel,
        grid=(t // tm,),
        in_specs=[
            pl.BlockSpec((tm, d), lambda i: (i, 0)),
            pl.BlockSpec((1, d), lambda i: (0, 0)),
            pl.BlockSpec((ROUTER_ROWS, d), lambda i: (0, 0)),
            pl.BlockSpec((ROUTER_ROWS, 1), lambda i: (0, 0)),
        ],
        out_specs=[
            pl.BlockSpec((SUBLANES, tm), lambda i: (0, i)),
            pl.BlockSpec((1, SUBLANES, LANES), lambda i: (i, 0, 0)),
            pl.BlockSpec((MOE_EXPERTS, tm), lambda i: (0, 0)),
        ],
        out_shape=[
            jax.ShapeDtypeStruct((SUBLANES, t), F32),
            jax.ShapeDtypeStruct((t // tm, SUBLANES, LANES), I32),
            jax.ShapeDtypeStruct((MOE_EXPERTS, tm), F32),
        ],
        scratch_shapes=[pltpu.VMEM((MOE_EXPERTS, tm), F32)],
        compiler_params=_params("arbitrary"),
        name="moe_router",
    )(x, norm_w, wr_t, br)


def _for_each_run_piece(tinfo_ref, pstart_ref, fn):
    full = RUN_SIZES[0]
    aligned = lambda v: pl.multiple_of(v, RUN_ALIGN)

    def per_expert(e, c):
        cnt = tinfo_ref[0, 0, e]
        sorted_row = pstart_ref[e] + tinfo_ref[0, 1, e]
        buf_row = tinfo_ref[0, 2, e]

        def whole(k, c2):
            fn(aligned(buf_row + k * full), aligned(sorted_row + k * full), full, 0)
            return c2

        lax.fori_loop(0, cnt // full, whole, 0)
        for cls, size in enumerate(RUN_SIZES[1:], 1):
            done = cnt & ~(2 * size - 1)

            @pl.when((cnt & size) != 0)
            def _():
                fn(aligned(buf_row + done), aligned(sorted_row + done), size, cls)
        return c

    lax.fori_loop(0, MOE_EXPERTS, per_expert, 0)


def _dispatch_kernel(pstart_ref, x_ref, nw_ref, slot_ref, tinfo_ref, rows_in_ref, rows_ref, buf_ref, sem):
    del rows_in_ref
    tm = x_ref.shape[0]
    nbuf = buf_ref.shape[0]
    hn = _rmsnorm_rows(x_ref[...], nw_ref[...]).astype(BF16)
    r = lax.broadcasted_iota(I32, (nbuf, tm), 0)
    slots = slot_ref[...].astype(I32)
    place = ((r == slots[2:3, :]) | (r == slots[3:4, :])).astype(BF16)
    buf_ref[...] = _dot(place, hn)

    def piece(buf_row, sorted_row, n, cls):
        return pltpu.make_async_copy(buf_ref.at[pl.ds(buf_row, n)], rows_ref.at[pl.ds(sorted_row, n)], sem.at[cls])

    _for_each_run_piece(tinfo_ref, pstart_ref, lambda *a: piece(*a).start())
    _for_each_run_piece(tinfo_ref, pstart_ref, lambda *a: piece(*a).wait())


def _dispatch(pstart, x, norm_w, gates, tinfo, cap):
    t, d = x.shape
    tm = MOE_TILE
    rows0 = jnp.zeros((cap, d), F32)
    grid_spec = pltpu.PrefetchScalarGridSpec(
        num_scalar_prefetch=1,
        grid=(t // tm,),
        in_specs=[
            pl.BlockSpec((tm, d), lambda i, ps: (i, 0)),
            pl.BlockSpec((1, d), lambda i, ps: (0, 0)),
            pl.BlockSpec((SUBLANES, tm), lambda i, ps: (0, i)),
            pl.BlockSpec((1, SUBLANES, LANES), lambda i, ps: (i, 0, 0), memory_space=pltpu.SMEM),
            pl.BlockSpec(memory_space=pl.ANY),
        ],
        out_specs=pl.BlockSpec(memory_space=pl.ANY),
        scratch_shapes=[pltpu.VMEM((MOE_TILE_BUF, d), F32), pltpu.SemaphoreType.DMA((len(RUN_SIZES),))],
    )
    return pl.pallas_call(
        _dispatch_kernel,
        grid_spec=grid_spec,
        out_shape=jax.ShapeDtypeStruct((cap, d), F32),
        input_output_aliases={5: 0},
        compiler_params=_params("arbitrary"),
        name="moe_dispatch",
    )(pstart, x, norm_w, gates, tinfo, rows0)


def _expert_kernel(be_ref, nused_ref, rows_ref, wu_ref, wd_ref, y_ref):
    del be_ref
    ff = wd_ref.shape[0]
    used = pl.program_id(0) < nused_ref[0]

    @pl.when(used)
    def _():
        xb = rows_ref[...].astype(BF16)
        gu = _dot(xb, wu_ref[...])
        act = _silu(gu[:, 0:ff]) * gu[:, ff:2 * ff]
        y_ref[...] = _dot(act.astype(BF16), wd_ref[...])

    @pl.when(jnp.logical_not(used))
    def _():
        y_ref[...] = jnp.zeros_like(y_ref)


def _experts(block_expert, n_used, rows, w_up, w_down):
    cap, d = rows.shape
    rb = MOE_ROW_BLOCK
    ff = w_down.shape[1]
    grid_spec = pltpu.PrefetchScalarGridSpec(
        num_scalar_prefetch=2,
        grid=(cap // rb,),
        in_specs=[
            pl.BlockSpec((rb, d), lambda i, be, nu: (i, 0)),
            pl.BlockSpec((None, d, 2 * ff), lambda i, be, nu: (be[i], 0, 0)),
            pl.BlockSpec((None, ff, d), lambda i, be, nu: (be[i], 0, 0)),
        ],
        out_specs=pl.BlockSpec((rb, d), lambda i, be, nu: (i, 0)),
    )
    return pl.pallas_call(
        _expert_kernel,
        grid_spec=grid_spec,
        out_shape=jax.ShapeDtypeStruct((cap, d), F32),
        compiler_params=_params("arbitrary"),
        name="moe_experts",
    )(block_expert, n_used, rows, w_up, w_down)


def _combine_kernel(pstart_ref, x_ref, gate_ref, tinfo_ref, fw_ref, y_ref, o_ref, buf_ref, sem, *, final_norm):
    tm = x_ref.shape[0]
    nbuf = buf_ref.shape[0]

    @pl.when(pl.program_id(0) == 0)
    def _():
        buf_ref[...] = jnp.zeros_like(buf_ref)

    def piece(buf_row, sorted_row, n, cls):
        return pltpu.make_async_copy(y_ref.at[pl.ds(sorted_row, n)], buf_ref.at[pl.ds(buf_row, n)], sem.at[cls])

    _for_each_run_piece(tinfo_ref, pstart_ref, lambda *a: piece(*a).start())

    g = gate_ref[...]
    cols = []
    for c in range(tm // LANES):
        sq = jnp.concatenate([g[:, c * LANES:(c + 1) * LANES], jnp.zeros((LANES - SUBLANES, LANES), F32)], axis=0)
        cols.append(sq.T)
    cols = jnp.concatenate(cols, axis=0)
    r = lax.broadcasted_iota(I32, (tm, nbuf), 1)
    weights = (jnp.where(r == cols[:, 2:3].astype(I32), cols[:, 0:1], 0.0)
               + jnp.where(r == cols[:, 3:4].astype(I32), cols[:, 1:2], 0.0))
    w_hi = weights.astype(BF16)
    w_lo = (weights - w_hi.astype(F32)).astype(BF16)

    _for_each_run_piece(tinfo_ref, pstart_ref, lambda *a: piece(*a).wait())

    y = buf_ref[...]
    y_hi = y.astype(BF16)
    y_lo = (y - y_hi.astype(F32)).astype(BF16)
    out = x_ref[...] + (_dot(w_hi, y_hi) + _dot(w_lo, y_hi) + _dot(w_hi, y_lo))
    if final_norm:
        out = _rmsnorm_rows(out, fw_ref[...])
    o_ref[...] = out


def _combine(pstart, x, gates, tinfo, y_rows, final_w, final_norm):
    t, d = x.shape
    tm = MOE_TILE
    grid_spec = pltpu.PrefetchScalarGridSpec(
        num_scalar_prefetch=1,
        grid=(t // tm,),
        in_specs=[
            pl.BlockSpec((tm, d), lambda i, ps: (i, 0)),
            pl.BlockSpec((SUBLANES, tm), lambda i, ps: (0, i)),
            pl.BlockSpec((1, SUBLANES, LANES), lambda i, ps: (i, 0, 0), memory_space=pltpu.SMEM),
            pl.BlockSpec((1, d), lambda i, ps: (0, 0)),
            pl.BlockSpec(memory_space=pl.ANY),
        ],
        out_specs=pl.BlockSpec((tm, d), lambda i, ps: (i, 0)),
        scratch_shapes=[pltpu.VMEM((MOE_TILE_BUF, d), F32), pltpu.SemaphoreType.DMA((len(RUN_SIZES),))],
    )
    return pl.pallas_call(
        functools.partial(_combine_kernel, final_norm=final_norm),
        grid_spec=grid_spec,
        out_shape=jax.ShapeDtypeStruct((t, d), F32),
        compiler_params=_params("arbitrary"),
        name="moe_combine",
    )(pstart, x, gates, tinfo, final_w, y_rows)


def _hier_moe(x, norm_w, w_group, b_group, w_router, b_router, w_up, w_down, final_w, final_norm):
    t, d = x.shape
    g, _, epg = w_router.shape
    pad_rows = SUBLANES - g
    wr_t = jnp.concatenate([w_group.T, jnp.zeros((pad_rows, d), F32),
                            w_router.transpose(0, 2, 1).reshape(g * epg, d)], axis=0)
    br = jnp.concatenate([b_group, jnp.full((pad_rows,), -1e30, F32), b_router.reshape(-1)])[:, None]
    gates, tinfo, cnt = _router(x, norm_w, wr_t, br)

    rb = MOE_ROW_BLOCK
    n_exp = g * epg
    counts = cnt[:, 0].astype(I32)
    padded = ((counts + rb - 1) // rb) * rb
    pends = jnp.cumsum(padded)
    pstart = (pends - padded).astype(I32)
    max_rows = 2 * t + (t // MOE_TILE) * n_exp * (RUN_ALIGN - 1) + n_exp * (rb - 1)
    n_blocks = -(-max_rows // rb)
    block_start = jnp.arange(n_blocks, dtype=I32) * rb
    block_expert = jnp.minimum(jnp.sum(block_start[:, None] >= pends[None, :], axis=1), n_exp - 1).astype(I32)
    n_used = (pends[-1:] // rb).astype(I32)

    rows = _dispatch(pstart, x, norm_w, gates, tinfo, n_blocks * rb)
    y_rows = _experts(block_expert, n_used, rows, w_up.astype(BF16), w_down.astype(BF16))
    return _combine(pstart, x, gates, tinfo, y_rows, final_w, final_norm)


def _ssd_in_kernel(x_ref, nw_ref, wz_ref, wx_ref, wdt_ref, z_ref, xbc_ref, dt_ref):
    h = _rmsnorm_rows(x_ref[...], nw_ref[...]).astype(BF16)
    z_ref[...] = _dot(h, wz_ref[...])
    xbc_ref[...] = _dot(h, wx_ref[...])
    dt_ref[...] = _dot(h, wdt_ref[...])


def _ssd_in_proj(x, norm_w, w_z, w_xbc, w_dt, tm=256):
    t, d = x.shape
    dz, dx, dh = w_z.shape[1], w_xbc.shape[1], w_dt.shape[1]
    resident = lambda n: pl.BlockSpec((d, n), lambda i: (0, 0), pipeline_mode=pl.Buffered(1))
    return pl.pallas_call(
        _ssd_in_kernel,
        grid=(t // tm,),
        in_specs=[
            pl.BlockSpec((tm, d), lambda i: (i, 0)),
            pl.BlockSpec((1, d), lambda i: (0, 0)),
            resident(dz), resident(dx), resident(dh),
        ],
        out_specs=[
            pl.BlockSpec((tm, dz), lambda i: (i, 0)),
            pl.BlockSpec((tm, dx), lambda i: (i, 0)),
            pl.BlockSpec((tm, dh), lambda i: (i, 0)),
        ],
        out_shape=[
            jax.ShapeDtypeStruct((t, dz), F32),
            jax.ShapeDtypeStruct((t, dx), F32),
            jax.ShapeDtypeStruct((t, dh), F32),
        ],
        compiler_params=_params("parallel"),
        name="ssd_in_proj",
    )(x, norm_w, w_z, w_xbc, w_dt)


def _ssd_kernel(xbc_ref, z_ref, dt_ref, cw_ref, cb_ref, dtb_ref, alog_ref, dskip_ref, nw_ref,
                y_ref, ext_ref, state_ref, *, d_inner, n_heads):
    L = SSD_CHUNK
    G = SSD_GROUPS
    N = SSD_STATE
    P = SSD_HEAD_DIM
    R = n_heads // G
    GW = R * P
    halo = SUBLANES

    @pl.when(pl.program_id(1) == 0)
    def _():
        ext_ref[0:halo, :] = jnp.zeros((halo, ext_ref.shape[1]), F32)
        state_ref[...] = jnp.zeros_like(state_ref)

    ext_ref[halo:halo + L, :] = xbc_ref[...]

    def conv_silu(c0, width):
        acc = jnp.broadcast_to(cb_ref[:, c0:c0 + width], (L, width))
        for k in range(SSD_CONV):
            back = SSD_CONV - 1 - k
            acc = acc + ext_ref[halo - back:halo - back + L, c0:c0 + width] * cw_ref[k:k + 1, c0:c0 + width]
        return _silu(acc)

    dt = _softplus(dt_ref[...] + dtb_ref[...])
    a_neg = -jnp.exp(alog_ref[...])
    da = dt * a_neg
    rr = lax.broadcasted_iota(I32, (L, L), 0)
    cc = lax.broadcasted_iota(I32, (L, L), 1)
    lower = rr >= cc
    cs = _dot_f32_rhs(lower.astype(BF16), da)
    cs_last = cs[L - 1:L, :]
    cs_sq = jnp.concatenate([cs, jnp.zeros((L, L - n_heads), F32)], axis=1).T
    decay_to_end = jnp.exp(cs_last - cs)
    decay_from_start = jnp.exp(cs)
    chunk_decay = jnp.exp(cs_last)

    head_of_lane = lax.broadcasted_iota(I32, (n_heads, d_inner), 1) // P
    head_row = lax.broadcasted_iota(I32, (n_heads, d_inner), 0)
    spread = (head_row == head_of_lane).astype(BF16)
    rows8 = lambda v: jnp.broadcast_to(v, (SUBLANES, n_heads))
    per_head = jnp.concatenate([dt, decay_to_end, decay_from_start, rows8(chunk_decay), rows8(dskip_ref[...])],
                               axis=0)
    wide = _dot_f32_lhs(per_head, spread)
    lane_head = lax.broadcasted_iota(I32, (1, GW), 1) // P

    for g in range(G):
        cols = slice(g * GW, (g + 1) * GW)
        dt_g = wide[0:L, cols]
        to_end_g = wide[L:2 * L, cols]
        from_start_g = wide[2 * L:3 * L, cols]
        chunk_decay_g = wide[3 * L:3 * L + 1, cols]
        dskip_g = wide[3 * L + SUBLANES:3 * L + SUBLANES + 1, cols]

        x_g = conv_silu(g * GW, GW)
        b_g = conv_silu(d_inner + g * N, N)
        c_g = conv_silu(d_inner + G * N + g * N, N)
        xdt = x_g * dt_g
        b_bf = b_g.astype(BF16)
        c_bf = c_g.astype(BF16)

        cb = _dot(c_bf, b_bf, NT_DIMS)
        y_g = jnp.zeros((L, GW), F32)
        for r in range(R):
            hd = g * R + r
            seg = cs[:, hd:hd + 1] - cs_sq[hd:hd + 1, :]
            m = jnp.where(lower, cb * jnp.exp(seg), 0.0)
            x_r = jnp.where(lane_head == r, xdt, 0.0)
            y_g = y_g + _dot(m.astype(BF16), x_r.astype(BF16))
        prev = state_ref[g]
        y_g = y_g + from_start_g * _dot(c_bf, prev.astype(BF16))
        new_part = _dot(b_bf, (xdt * to_end_g).astype(BF16), TN_DIMS)
        state_ref[g] = prev * chunk_decay_g + new_part
        y_g = y_g + x_g * dskip_g

        yz = y_g * _silu(z_ref[:, g * GW:(g + 1) * GW])
        yn = yz * lax.rsqrt(jnp.mean(yz * yz, axis=-1, keepdims=True) + EPS)
        y_ref[:, g * GW:(g + 1) * GW] = (yn * nw_ref[:, g * GW:(g + 1) * GW]).astype(BF16)

    ext_ref[0:halo, :] = ext_ref[L:L + halo, :]


def _ssd_scan(xbc, z, dt, conv_w, conv_b, dt_bias, a_log, d_skip, norm_w, batch, seq):
    t, dx = xbc.shape
    d_inner = z.shape[1]
    n_heads = dt.shape[1]
    L = SSD_CHUNK
    nc = seq // L
    row = lambda n: pl.BlockSpec((1, n), lambda b, c: (0, 0))
    kern = functools.partial(_ssd_kernel, d_inner=d_inner, n_heads=n_heads)
    return pl.pallas_call(
        kern,
        grid=(batch, nc),
        in_specs=[
            pl.BlockSpec((L, dx), lambda b, c: (b * nc + c, 0)),
            pl.BlockSpec((L, d_inner), lambda b, c: (b * nc + c, 0)),
            pl.BlockSpec((L, n_heads), lambda b, c: (b * nc + c, 0)),
            pl.BlockSpec((SSD_CONV, dx), lambda b, c: (0, 0)),
            row(dx), row(n_heads), row(n_heads), row(n_heads), row(d_inner),
        ],
        out_specs=pl.BlockSpec((L, d_inner), lambda b, c: (b * nc + c, 0)),
        out_shape=jax.ShapeDtypeStruct((t, d_inner), BF16),
        scratch_shapes=[
            pltpu.VMEM((L + 2 * SUBLANES, dx), F32),
            pltpu.VMEM((SSD_GROUPS, SSD_STATE, d_inner // SSD_GROUPS), F32),
        ],
        compiler_params=_params("arbitrary", "arbitrary"),
        name="ssd_scan",
    )(xbc, z, dt, conv_w, conv_b, dt_bias, a_log, d_skip, norm_w)


def _rope_tables(seq):
    half = ATTN_HEAD_DIM // 2
    inv = ROPE_THETA ** (-jnp.arange(half, dtype=F32) / half)
    ang = jnp.arange(seq).astype(F32)[:, None] * inv[None, :]
    cos = jnp.cos(ang)
    sin = jnp.sin(ang)
    reps = LANES // ATTN_HEAD_DIM
    cos_t = jnp.tile(jnp.concatenate([cos, cos], axis=1), (1, reps))
    sin_t = jnp.tile(jnp.concatenate([-sin, sin], axis=1), (1, reps))
    return cos_t, sin_t


def kernel(x, mix_norm, ffn_norm, final_norm, attn_w_qkv, attn_w_o, ssd_w_in, ssd_conv_w, ssd_conv_b,
           ssd_dt_bias, ssd_a_log, ssd_d, ssd_norm, ssd_w_out, moe_w_group, moe_b_group, moe_w_router,
           moe_b_router, moe_w_up, moe_w_down):
    batch, seq, d = x.shape
    assert seq % MOBA_BLOCK == 0 and seq % SSD_CHUNK == 0 and d % LANES == 0
    depth = mix_norm.shape[0]
    t = batch * seq
    xt = x.reshape(t, d)
    cos_t, sin_t = _rope_tables(seq)
    final_w = final_norm[None, :]

    for i in range(depth):
        j = i // 2
        nw = mix_norm[i][None, :]
        if i % 2 == 0:
            w_qkv = attn_w_qkv[j].astype(BF16)
            q, k, vt, kmean = _qkv_rope(xt, nw, w_qkv[:, 0:2 * d], w_qkv[:, 2 * d:].T, cos_t, sin_t, seq)
            kmean = kmean.reshape(batch, seq // MOBA_BLOCK, d)
            o = _moba_attention(q, k, vt, kmean, batch, seq)
            xt = _proj_residual(o, attn_w_o[j].astype(BF16), xt)
        else:
            w_in = ssd_w_in[j].astype(BF16)
            d_inner = ssd_norm.shape[1]
            dx = ssd_conv_b.shape[1]
            z, xbc, dt = _ssd_in_proj(xt, nw, w_in[:, 0:d_inner], w_in[:, d_inner:d_inner + dx],
                                      w_in[:, d_inner + dx:])
            y = _ssd_scan(xbc, z, dt, ssd_conv_w[j][:, 0, :], ssd_conv_b[j][None, :], ssd_dt_bias[j][None, :],
                          ssd_a_log[j][None, :], ssd_d[j][None, :], ssd_norm[j][None, :], batch, seq)
            xt = _proj_residual(y, ssd_w_out[j].astype(BF16), xt)
        xt = _hier_moe(xt, ffn_norm[i][None, :], moe_w_group[i], moe_b_group[i], moe_w_router[i],
                       moe_b_router[i], moe_w_up[i], moe_w_down[i], final_w, final_norm=(i == depth - 1))
    return xt.reshape(batch, seq, d)
```

```python
import functools
import math

import jax
import jax.numpy as jnp
from jax import lax
from jax.experimental import pallas as pl
from jax.experimental.pallas import tpu as pltpu

F32 = jnp.float32
BF16 = jnp.bfloat16
I32 = jnp.int32

EPS = 1e-6
LANES = 128
SUBLANES = 8
VMEM_LIMIT = 56 * 1024 * 1024

ATTN_HEADS = 16
ATTN_HEAD_DIM = 64
MOBA_BLOCK = 256
MOBA_TOPK = 3
ROPE_THETA = 10000.0
HEADS_PER_TILE = LANES // ATTN_HEAD_DIM

SSD_HEAD_DIM = 64
SSD_GROUPS = 8
SSD_STATE = 128
SSD_CONV = 4
SSD_CHUNK = 128

MOE_GROUPS = 4
MOE_EXPERTS_PER_GROUP = 8
MOE_EXPERTS = MOE_GROUPS * MOE_EXPERTS_PER_GROUP
MOE_ROW_BLOCK = 256
ROUTER_ROWS = SUBLANES + MOE_EXPERTS
MOE_TILE = 256
RUN_ALIGN = SUBLANES
RUN_SIZES = (16, 8)
MOE_TILE_BUF = 2 * MOE_TILE + MOE_EXPERTS * RUN_ALIGN

NT_DIMS = (((1,), (1,)), ((), ()))
NN_DIMS = (((1,), (0,)), ((), ()))
TN_DIMS = (((0,), (0,)), ((), ()))


def _params(*sem):
    return pltpu.CompilerParams(dimension_semantics=sem, vmem_limit_bytes=VMEM_LIMIT)


def _rmsnorm_rows(x, w):
    var = jnp.mean(x * x, axis=-1, keepdims=True)
    return (x * lax.rsqrt(var + EPS)) * w


def _split3(x):
    hi = x.astype(BF16)
    r = x - hi.astype(F32)
    mid = r.astype(BF16)
    lo = (r - mid.astype(F32)).astype(BF16)
    return hi, mid, lo


def _dot(a, b, dims=NN_DIMS):
    return lax.dot_general(a, b, dims, preferred_element_type=F32)


def _dot_f32_lhs(x, e, dims=NN_DIMS):
    hi, mid, lo = _split3(x)
    return _dot(hi, e, dims) + _dot(mid, e, dims) + _dot(lo, e, dims)


def _dot_f32_rhs(e, x, dims=NN_DIMS):
    hi, mid, lo = _split3(x)
    return _dot(e, hi, dims) + _dot(e, mid, dims) + _dot(e, lo, dims)


def _dot_f32(a, b, dims=NN_DIMS):
    ah, am, al = _split3(a)
    bh, bm, bl = _split3(b)
    return (_dot(ah, bh, dims) + _dot(ah, bm, dims) + _dot(am, bh, dims)
            + _dot(ah, bl, dims) + _dot(am, bm, dims) + _dot(al, bh, dims))


def _silu(x):
    return x * (0.5 * jnp.tanh(0.5 * x) + 0.5)


def _softplus(x):
    return jnp.maximum(x, 0.0) + jnp.log(1.0 + jnp.exp(-jnp.abs(x)))


def _qkv_kernel(x_ref, nw_ref, wqk_ref, wvt_ref, cos_ref, sin_ref, q_ref, k_ref, vt_ref, km_ref, *, d, scale):
    h = _rmsnorm_rows(x_ref[...], nw_ref[...]).astype(BF16)
    cos = cos_ref[...]
    sin = sin_ref[...]
    lane = lax.broadcasted_iota(I32, (1, LANES), 1)
    first_half = (lane % ATTN_HEAD_DIM) < (ATTN_HEAD_DIM // 2)

    def rope_tile(xc):
        lo_partner = pltpu.roll(xc, LANES - ATTN_HEAD_DIM // 2, 1)
        hi_partner = pltpu.roll(xc, ATTN_HEAD_DIM // 2, 1)
        return xc * cos + jnp.where(first_half, lo_partner, hi_partner) * sin

    q = _dot(h, wqk_ref[:, 0:d])
    k = _dot(h, wqk_ref[:, d:2 * d])
    for c in range(d // LANES):
        sl = slice(c * LANES, (c + 1) * LANES)
        q_ref[:, sl] = (rope_tile(q[:, sl]) * scale).astype(BF16)
        kc = rope_tile(k[:, sl])
        k_ref[:, sl] = kc.astype(BF16)
        km_ref[0, :, sl] = jnp.mean(kc, axis=0, keepdims=True)
    vt_ref[0] = _dot(wvt_ref[...], h, NT_DIMS).astype(BF16)


def _qkv_rope(x, norm_w, w_qk, w_vt, cos_t, sin_t, seq):
    t, d = x.shape
    tm = MOBA_BLOCK
    n_tiles = t // tm
    seq_tiles = seq // tm
    kern = functools.partial(_qkv_kernel, d=d, scale=math.log2(math.e) / math.sqrt(ATTN_HEAD_DIM))
    return pl.pallas_call(
        kern,
        grid=(n_tiles,),
        in_specs=[
            pl.BlockSpec((tm, d), lambda i: (i, 0)),
            pl.BlockSpec((1, d), lambda i: (0, 0)),
            pl.BlockSpec((d, 2 * d), lambda i: (0, 0), pipeline_mode=pl.Buffered(1)),
            pl.BlockSpec((d, d), lambda i: (0, 0), pipeline_mode=pl.Buffered(1)),
            pl.BlockSpec((tm, LANES), lambda i: (i % seq_tiles, 0)),
            pl.BlockSpec((tm, LANES), lambda i: (i % seq_tiles, 0)),
        ],
        out_specs=[
            pl.BlockSpec((tm, d), lambda i: (i, 0)),
            pl.BlockSpec((tm, d), lambda i: (i, 0)),
            pl.BlockSpec((1, d, tm), lambda i: (i, 0, 0)),
            pl.BlockSpec((1, 1, d), lambda i: (i, 0, 0)),
        ],
        out_shape=[
            jax.ShapeDtypeStruct((t, d), BF16),
            jax.ShapeDtypeStruct((t, d), BF16),
            jax.ShapeDtypeStruct((n_tiles, d, tm), BF16),
            jax.ShapeDtypeStruct((n_tiles, 1, d), F32),
        ],
        compiler_params=_params("parallel"),
        name="qkv_rope",
    )(x, norm_w, w_qk, w_vt, cos_t, sin_t)


def _attn_kernel(q_ref, k_ref, vt_ref, km_ref, o_ref, sa_ref, sb_ref, p_ref, *, nb):
    qi = pl.program_id(2)
    tq = q_ref.shape[0]
    q2 = q_ref[...]
    km = km_ref[...]
    lane = lax.broadcasted_iota(I32, (1, LANES), 1)
    key = lax.broadcasted_iota(I32, (MOBA_BLOCK, tq), 0)
    qry = lax.broadcasted_iota(I32, (MOBA_BLOCK, tq), 1)
    causal = key <= qry
    blk = lax.broadcasted_iota(I32, (nb, tq), 0)
    neg_inf = jnp.float32(-jnp.inf)
    heads = range(HEADS_PER_TILE)

    qh = [jnp.where((lane // ATTN_HEAD_DIM) == h, q2, jnp.zeros_like(q2)) for h in heads]

    bits = []
    for h in heads:
        gate = _dot_f32_lhs(km, qh[h], NT_DIMS)
        alive = (blk < qi).astype(I32)
        sel = jnp.zeros((nb, tq), I32)
        for _ in range(min(MOBA_TOPK, nb)):
            gm = jnp.where(alive > 0, gate, neg_inf)
            top = jnp.max(gm, axis=0, keepdims=True)
            cand = jnp.where((alive > 0) & (gm == top), blk, nb)
            idx = jnp.min(cand, axis=0, keepdims=True)
            pick = (blk == idx).astype(I32)
            sel = sel | pick
            alive = alive & (1 - pick)
        bits.append(jnp.sum(sel << blk, axis=0, keepdims=True))

    def block_of(k):
        return jnp.where(k == 0, qi, jnp.minimum(k - 1, qi))

    def issue_scores(k, s_out):
        start = pl.multiple_of(block_of(k) * MOBA_BLOCK, MOBA_BLOCK)
        kb = k_ref[pl.ds(start, MOBA_BLOCK), :]
        for h in heads:
            s_out[h] = _dot(kb, qh[h], NT_DIMS)

    def issue_values(k):
        blk_idx = block_of(k)
        return [_dot(vt_ref[blk_idx, h * ATTN_HEAD_DIM:(h + 1) * ATTN_HEAD_DIM, :], p_ref[h]) for h in heads]

    def softmax(k, s_in, state, diag):
        out = []
        for h in heads:
            m_i, l_i = state[h]
            if diag:
                s = jnp.where(causal, s_in[h], neg_inf)
                m_new = jnp.maximum(m_i, jnp.max(s, axis=0, keepdims=True))
                shift = m_new
            else:
                s = s_in[h]
                keep = ((bits[h] >> (k - 1)) & 1) == 1
                m_new = jnp.maximum(m_i, jnp.where(keep, jnp.max(s, axis=0, keepdims=True), neg_inf))
                shift = jnp.where(keep, m_new, -neg_inf)
            alpha = jnp.exp2(m_i - m_new)
            p = jnp.exp2(s - shift)
            l_new = alpha * l_i + jnp.sum(p, axis=0, keepdims=True)
            p_ref[h] = p.astype(BF16)
            out.append((m_new, l_new, alpha))
        return out

    def phase(k, carry, s_cur, s_nxt):
        issue_scores(k + 1, s_nxt)
        pv = issue_values(k - 1)
        stats = softmax(k, s_cur, [(c[0], c[1]) for c in carry], False)
        return tuple((stats[h][0], stats[h][1], carry[h][3] * carry[h][2] + pv[h], stats[h][2]) for h in heads)

    issue_scores(0, sa_ref)
    issue_scores(1, sb_ref)
    first = softmax(0, sa_ref, [(jnp.full((1, tq), neg_inf, F32), jnp.zeros((1, tq), F32)) for _ in heads], True)
    carry = tuple((m, l, jnp.zeros((ATTN_HEAD_DIM, tq), F32), a) for (m, l, a) in first)

    def two_phases(u, c):
        c = phase(2 * u + 1, c, sb_ref, sa_ref)
        return phase(2 * u + 2, c, sa_ref, sb_ref)

    n_pairs = (qi + 1) // 2
    carry = lax.fori_loop(0, n_pairs, two_phases, carry)
    pv = issue_values(2 * n_pairs)
    o_t = jnp.concatenate([(alpha * acc + pv[h]) / l_i for h, (_, l_i, acc, alpha) in enumerate(carry)],
                          axis=0)
    o_ref[...] = o_t.T.astype(BF16)


def _moba_attention(q, k, vt, kmean, batch, seq):
    t, d = q.shape
    nb = seq // MOBA_BLOCK
    tq = MOBA_BLOCK
    nq = seq // tq
    col_tiles = d // LANES
    kern = functools.partial(_attn_kernel, nb=nb)
    return pl.pallas_call(
        kern,
        grid=(batch, col_tiles, nq),
        in_specs=[
            pl.BlockSpec((tq, LANES), lambda b, c, i: (b * nq + i, c)),
            pl.BlockSpec((seq, LANES), lambda b, c, i: (b, c)),
            pl.BlockSpec((nb, LANES, MOBA_BLOCK), lambda b, c, i: (b, c, 0)),
            pl.BlockSpec((None, nb, LANES), lambda b, c, i: (b, 0, c)),
        ],
        out_specs=pl.BlockSpec((tq, LANES), lambda b, c, i: (b * nq + i, c)),
        out_shape=jax.ShapeDtypeStruct((t, d), BF16),
        scratch_shapes=[
            pltpu.VMEM((HEADS_PER_TILE, MOBA_BLOCK, tq), F32),
            pltpu.VMEM((HEADS_PER_TILE, MOBA_BLOCK, tq), F32),
            pltpu.VMEM((HEADS_PER_TILE, MOBA_BLOCK, tq), BF16),
        ],
        compiler_params=_params("parallel", "parallel", "arbitrary"),
        name="moba_attention",
    )(q, k, vt, kmean)


def _proj_res_kernel(a_ref, w_ref, res_ref, o_ref):
    o_ref[...] = res_ref[...] + _dot(a_ref[...], w_ref[...])


def _proj_residual(a, w, res, tm=512):
    t, kdim = a.shape
    d = w.shape[1]
    return pl.pallas_call(
        _proj_res_kernel,
        grid=(t // tm,),
        in_specs=[
            pl.BlockSpec((tm, kdim), lambda i: (i, 0)),
            pl.BlockSpec((kdim, d), lambda i: (0, 0), pipeline_mode=pl.Buffered(1)),
            pl.BlockSpec((tm, d), lambda i: (i, 0)),
        ],
        out_specs=pl.BlockSpec((tm, d), lambda i: (i, 0)),
        out_shape=jax.ShapeDtypeStruct((t, d), F32),
        compiler_params=_params("parallel"),
        name="proj_residual",
    )(a, w, res)


def _router_kernel(x_ref, nw_ref, wr_ref, br_ref, gate_ref, tinfo_ref, cnt_ref, carry_ref):
    tm = x_ref.shape[0]

    @pl.when(pl.program_id(0) == 0)
    def _():
        carry_ref[...] = jnp.zeros_like(carry_ref)

    hn = _rmsnorm_rows(x_ref[...], nw_ref[...])
    logits = _dot_f32(wr_ref[...], hn, NT_DIMS) + br_ref[...]
    sub = lax.broadcasted_iota(I32, (SUBLANES, tm), 0)
    neg_inf = jnp.float32(-jnp.inf)

    gl = logits[0:SUBLANES]
    gmax = jnp.max(gl, axis=0, keepdims=True)
    gidx = jnp.min(jnp.where(gl == gmax, sub, SUBLANES), axis=0, keepdims=True)
    gprob = 1.0 / jnp.sum(jnp.exp(gl - gmax), axis=0, keepdims=True)

    el = jnp.zeros((SUBLANES, tm), F32)
    for g in range(MOE_GROUPS):
        el = jnp.where(gidx == g, logits[SUBLANES * (g + 1):SUBLANES * (g + 2)], el)
    m1 = jnp.max(el, axis=0, keepdims=True)
    i1 = jnp.min(jnp.where(el == m1, sub, SUBLANES), axis=0, keepdims=True)
    el2 = jnp.where(sub == i1, neg_inf, el)
    m2 = jnp.max(el2, axis=0, keepdims=True)
    i2 = jnp.min(jnp.where(el2 == m2, sub, SUBLANES), axis=0, keepdims=True)
    ratio = jnp.exp(m2 - m1)
    den = 1.0 + ratio
    gate_ref[...] = jnp.zeros_like(gate_ref)
    gate_ref[0:1, :] = gprob / den
    gate_ref[1:2, :] = gprob * ratio / den

    e1 = gidx * MOE_EXPERTS_PER_GROUP + i1
    e2 = gidx * MOE_EXPERTS_PER_GROUP + i2

    eiota = lax.broadcasted_iota(I32, (MOE_EXPERTS, tm), 0)
    oh1 = eiota == e1
    oh2 = eiota == e2
    onehot = (oh1 | oh2).astype(BF16)
    src = lax.broadcasted_iota(I32, (tm, tm), 0)
    dst = lax.broadcasted_iota(I32, (tm, tm), 1)
    before = (src < dst).astype(BF16)
    prefix = _dot(onehot, before)
    count = _dot(onehot, jnp.ones((tm, tm), BF16))
    total = jnp.floor((count + (RUN_ALIGN - 1)) * (1.0 / RUN_ALIGN)) * RUN_ALIGN
    erow = lax.broadcasted_iota(I32, (MOE_EXPERTS, MOE_EXPERTS), 0)
    ecol = lax.broadcasted_iota(I32, (MOE_EXPERTS, MOE_EXPERTS), 1)
    run_start = _dot_f32_rhs((ecol < erow).astype(BF16), total)
    slot = run_start + prefix
    gate_ref[2:3, :] = jnp.sum(jnp.where(oh1, slot, 0.0), axis=0, keepdims=True)
    gate_ref[3:4, :] = jnp.sum(jnp.where(oh2, slot, 0.0), axis=0, keepdims=True)

    lane = lax.broadcasted_iota(I32, (MOE_EXPERTS, tm), 1)
    on_lanes = lambda v: jnp.sum(jnp.where(eiota == lane, v, 0.0), axis=0, keepdims=True)[:, 0:LANES].astype(I32)
    tinfo_ref[...] = jnp.zeros_like(tinfo_ref)
    tinfo_ref[0, 0:1, :] = on_lanes(total)
    tinfo_ref[0, 1:2, :] = on_lanes(carry_ref[...])
    tinfo_ref[0, 2:3, :] = on_lanes(run_start)
    carry_ref[...] = carry_ref[...] + total
    cnt_ref[...] = carry_ref[...]


def _router(x, norm_w, wr_t, br):
    t, d = x.shape
    tm = MOE_TILE
    return pl.pallas_call(
        _router_kernel,
        grid=(t // tm,),
        in_specs=[
            pl.BlockSpec((tm, d), lambda i: (i, 0)),
            pl.BlockSpec((1, d), lambda i: (0, 0)),
            pl.BlockSpec((ROUTER_ROWS, d), lambda i: (0, 0)),
            pl.BlockSpec((ROUTER_ROWS, 1), lambda i: (0, 0)),
        ],
        out_specs=[
            pl.BlockSpec((SUBLANES, tm), lambda i: (0, i)),
            pl.BlockSpec((1, SUBLANES, LANES), lambda i: (i, 0, 0)),
            pl.BlockSpec((MOE_EXPERTS, tm), lambda i: (0, 0)),
        ],
        out_shape=[
            jax.ShapeDtypeStruct((SUBLANES, t), F32),
            jax.ShapeDtypeStruct((t // tm, SUBLANES, LANES), I32),
            jax.ShapeDtypeStruct((MOE_EXPERTS, tm), F32),
        ],
        scratch_shapes=[pltpu.VMEM((MOE_EXPERTS, tm), F32)],
        compiler_params=_params("arbitrary"),
        name="moe_router",
    )(x, norm_w, wr_t, br)


def _for_each_run_piece(tinfo_ref, pstart_ref, fn):
    full = RUN_SIZES[0]
    aligned = lambda v: pl.multiple_of(v, RUN_ALIGN)

    def per_expert(e, c):
        cnt = tinfo_ref[0, 0, e]
        sorted_row = pstart_ref[e] + tinfo_ref[0, 1, e]
        buf_row = tinfo_ref[0, 2, e]

        def whole(k, c2):
            fn(aligned(buf_row + k * full), aligned(sorted_row + k * full), full, 0)
            return c2

        lax.fori_loop(0, cnt // full, whole, 0)
        for cls, size in enumerate(RUN_SIZES[1:], 1):
            done = cnt & ~(2 * size - 1)

            @pl.when((cnt & size) != 0)
            def _():
                fn(aligned(buf_row + done), aligned(sorted_row + done), size, cls)
        return c

    lax.fori_loop(0, MOE_EXPERTS, per_expert, 0)


def _dispatch_kernel(pstart_ref, x_ref, nw_ref, slot_ref, tinfo_ref, rows_in_ref, rows_ref, buf_ref, sem):
    del rows_in_ref
    tm = x_ref.shape[0]
    nbuf = buf_ref.shape[0]
    hn = _rmsnorm_rows(x_ref[...], nw_ref[...]).astype(BF16)
    r = lax.broadcasted_iota(I32, (nbuf, tm), 0)
    slots = slot_ref[...].astype(I32)
    place = ((r == slots[2:3, :]) | (r == slots[3:4, :])).astype(BF16)
    buf_ref[...] = _dot(place, hn)

    def piece(buf_row, sorted_row, n, cls):
        return pltpu.make_async_copy(buf_ref.at[pl.ds(buf_row, n)], rows_ref.at[pl.ds(sorted_row, n)], sem.at[cls])

    _for_each_run_piece(tinfo_ref, pstart_ref, lambda *a: piece(*a).start())
    _for_each_run_piece(tinfo_ref, pstart_ref, lambda *a: piece(*a).wait())


def _dispatch(pstart, x, norm_w, gates, tinfo, cap):
    t, d = x.shape
    tm = MOE_TILE
    rows0 = jnp.zeros((cap, d), F32)
    grid_spec = pltpu.PrefetchScalarGridSpec(
        num_scalar_prefetch=1,
        grid=(t // tm,),
        in_specs=[
            pl.BlockSpec((tm, d), lambda i, ps: (i, 0)),
            pl.BlockSpec((1, d), lambda i, ps: (0, 0)),
            pl.BlockSpec((SUBLANES, tm), lambda i, ps: (0, i)),
            pl.BlockSpec((1, SUBLANES, LANES), lambda i, ps: (i, 0, 0), memory_space=pltpu.SMEM),
            pl.BlockSpec(memory_space=pl.ANY),
        ],
        out_specs=pl.BlockSpec(memory_space=pl.ANY),
        scratch_shapes=[pltpu.VMEM((MOE_TILE_BUF, d), F32), pltpu.SemaphoreType.DMA((len(RUN_SIZES),))],
    )
    return pl.pallas_call(
        _dispatch_kernel,
        grid_spec=grid_spec,
        out_shape=jax.ShapeDtypeStruct((cap, d), F32),
        input_output_aliases={5: 0},
        compiler_params=_params("arbitrary"),
        name="moe_dispatch",
    )(pstart, x, norm_w, gates, tinfo, rows0)


def _expert_kernel(be_ref, nused_ref, rows_ref, wu_ref, wd_ref, y_ref, wu_bf_ref, wd_bf_ref):
    ff = wd_ref.shape[0]
    i = pl.program_id(0)
    used = i < nused_ref[0]
    new_expert = jnp.logical_or(i == 0, be_ref[i] != be_ref[jnp.maximum(i - 1, 0)])

    @pl.when(jnp.logical_and(used, new_expert))
    def _():
        wu_bf_ref[...] = wu_ref[...].astype(BF16)
        wd_bf_ref[...] = wd_ref[...].astype(BF16)

    @pl.when(used)
    def _():
        xb = rows_ref[...].astype(BF16)
        gu = _dot(xb, wu_bf_ref[...])
        act = _silu(gu[:, 0:ff]) * gu[:, ff:2 * ff]
        y_ref[...] = _dot(act.astype(BF16), wd_bf_ref[...])

    @pl.when(jnp.logical_not(used))
    def _():
        y_ref[...] = jnp.zeros_like(y_ref)


def _experts(block_expert, n_used, rows, w_up, w_down, layer):
    cap, d = rows.shape
    rb = MOE_ROW_BLOCK
    ff = w_down.shape[2]
    grid_spec = pltpu.PrefetchScalarGridSpec(
        num_scalar_prefetch=2,
        grid=(cap // rb,),
        in_specs=[
            pl.BlockSpec((rb, d), lambda i, be, nu: (i, 0)),
            pl.BlockSpec((None, None, d, 2 * ff), lambda i, be, nu: (layer, be[i], 0, 0)),
            pl.BlockSpec((None, None, ff, d), lambda i, be, nu: (layer, be[i], 0, 0)),
        ],
        out_specs=pl.BlockSpec((rb, d), lambda i, be, nu: (i, 0)),
        scratch_shapes=[pltpu.VMEM((d, 2 * ff), BF16), pltpu.VMEM((ff, d), BF16)],
    )
    return pl.pallas_call(
        _expert_kernel,
        grid_spec=grid_spec,
        out_shape=jax.ShapeDtypeStruct((cap, d), F32),
        compiler_params=_params("arbitrary"),
        name="moe_experts",
    )(block_expert, n_used, rows, w_up, w_down)


def _combine_kernel(pstart_ref, x_ref, gate_ref, tinfo_ref, fw_ref, y_ref, o_ref, buf_ref, sem, *, final_norm):
    tm = x_ref.shape[0]
    nbuf = buf_ref.shape[0]

    @pl.when(pl.program_id(0) == 0)
    def _():
        buf_ref[...] = jnp.zeros_like(buf_ref)

    def piece(buf_row, sorted_row, n, cls):
        return pltpu.make_async_copy(y_ref.at[pl.ds(sorted_row, n)], buf_ref.at[pl.ds(buf_row, n)], sem.at[cls])

    _for_each_run_piece(tinfo_ref, pstart_ref, lambda *a: piece(*a).start())

    g = gate_ref[...]
    cols = []
    for c in range(tm // LANES):
        sq = jnp.concatenate([g[:, c * LANES:(c + 1) * LANES], jnp.zeros((LANES - SUBLANES, LANES), F32)], axis=0)
        cols.append(sq.T)
    cols = jnp.concatenate(cols, axis=0)
    r = lax.broadcasted_iota(I32, (tm, nbuf), 1)
    weights = (jnp.where(r == cols[:, 2:3].astype(I32), cols[:, 0:1], 0.0)
               + jnp.where(r == cols[:, 3:4].astype(I32), cols[:, 1:2], 0.0))
    w_hi = weights.astype(BF16)
    w_lo = (weights - w_hi.astype(F32)).astype(BF16)

    _for_each_run_piece(tinfo_ref, pstart_ref, lambda *a: piece(*a).wait())

    y = buf_ref[...]
    y_hi = y.astype(BF16)
    y_lo = (y - y_hi.astype(F32)).astype(BF16)
    out = x_ref[...] + (_dot(w_hi, y_hi) + _dot(w_lo, y_hi) + _dot(w_hi, y_lo))
    if final_norm:
        out = _rmsnorm_rows(out, fw_ref[...])
    o_ref[...] = out


def _combine(pstart, x, gates, tinfo, y_rows, final_w, final_norm):
    t, d = x.shape
    tm = MOE_TILE
    grid_spec = pltpu.PrefetchScalarGridSpec(
        num_scalar_prefetch=1,
        grid=(t // tm,),
        in_specs=[
            pl.BlockSpec((tm, d), lambda i, ps: (i, 0)),
            pl.BlockSpec((SUBLANES, tm), lambda i, ps: (0, i)),
            pl.BlockSpec((1, SUBLANES, LANES), lambda i, ps: (i, 0, 0), memory_space=pltpu.SMEM),
            pl.BlockSpec((1, d), lambda i, ps: (0, 0)),
            pl.BlockSpec(memory_space=pl.ANY),
        ],
        out_specs=pl.BlockSpec((tm, d), lambda i, ps: (i, 0)),
        scratch_shapes=[pltpu.VMEM((MOE_TILE_BUF, d), F32), pltpu.SemaphoreType.DMA((len(RUN_SIZES),))],
    )
    return pl.pallas_call(
        functools.partial(_combine_kernel, final_norm=final_norm),
        grid_spec=grid_spec,
        out_shape=jax.ShapeDtypeStruct((t, d), F32),
        compiler_params=_params("arbitrary"),
        name="moe_combine",
    )(pstart, x, gates, tinfo, final_w, y_rows)


def _hier_moe(x, norm_w, w_group, b_group, w_router, b_router, w_up, w_down, layer, final_w, final_norm):
    t, d = x.shape
    g, _, epg = w_router.shape
    pad_rows = SUBLANES - g
    wr_t = jnp.concatenate([w_group.T, jnp.zeros((pad_rows, d), F32),
                            w_router.transpose(0, 2, 1).reshape(g * epg, d)], axis=0)
    br = jnp.concatenate([b_group, jnp.full((pad_rows,), -1e30, F32), b_router.reshape(-1)])[:, None]
    gates, tinfo, cnt = _router(x, norm_w, wr_t, br)

    rb = MOE_ROW_BLOCK
    n_exp = g * epg
    counts = cnt[:, 0].astype(I32)
    padded = ((counts + rb - 1) // rb) * rb
    pends = jnp.cumsum(padded)
    pstart = (pends - padded).astype(I32)
    max_rows = 2 * t + (t // MOE_TILE) * n_exp * (RUN_ALIGN - 1) + n_exp * (rb - 1)
    n_blocks = -(-max_rows // rb)
    block_start = jnp.arange(n_blocks, dtype=I32) * rb
    block_expert = jnp.minimum(jnp.sum(block_start[:, None] >= pends[None, :], axis=1), n_exp - 1).astype(I32)
    n_used = (pends[-1:] // rb).astype(I32)

    rows = _dispatch(pstart, x, norm_w, gates, tinfo, n_blocks * rb)
    y_rows = _experts(block_expert, n_used, rows, w_up, w_down, layer)
    return _combine(pstart, x, gates, tinfo, y_rows, final_w, final_norm)


def _ssd_in_kernel(x_ref, nw_ref, wz_ref, wx_ref, wdt_ref, z_ref, xbc_ref, dt_ref):
    h = _rmsnorm_rows(x_ref[...], nw_ref[...]).astype(BF16)
    z_ref[...] = _dot(h, wz_ref[...])
    xbc_ref[...] = _dot(h, wx_ref[...])
    dt_ref[...] = _dot(h, wdt_ref[...])


def _ssd_in_proj(x, norm_w, w_z, w_xbc, w_dt, tm=256):
    t, d = x.shape
    dz, dx, dh = w_z.shape[1], w_xbc.shape[1], w_dt.shape[1]
    resident = lambda n: pl.BlockSpec((d, n), lambda i: (0, 0), pipeline_mode=pl.Buffered(1))
    return pl.pallas_call(
        _ssd_in_kernel,
        grid=(t // tm,),
        in_specs=[
            pl.BlockSpec((tm, d), lambda i: (i, 0)),
            pl.BlockSpec((1, d), lambda i: (0, 0)),
            resident(dz), resident(dx), resident(dh),
        ],
        out_specs=[
            pl.BlockSpec((tm, dz), lambda i: (i, 0)),
            pl.BlockSpec((tm, dx), lambda i: (i, 0)),
            pl.BlockSpec((tm, dh), lambda i: (i, 0)),
        ],
        out_shape=[
            jax.ShapeDtypeStruct((t, dz), F32),
            jax.ShapeDtypeStruct((t, dx), F32),
            jax.ShapeDtypeStruct((t, dh), F32),
        ],
        compiler_params=_params("parallel"),
        name="ssd_in_proj",
    )(x, norm_w, w_z, w_xbc, w_dt)


def _ssd_kernel(xbc_ref, z_ref, dt_ref, cw_ref, cb_ref, dtb_ref, alog_ref, dskip_ref, nw_ref,
                y_ref, ext_ref, state_ref, *, d_inner, n_heads):
    L = SSD_CHUNK
    G = SSD_GROUPS
    N = SSD_STATE
    P = SSD_HEAD_DIM
    R = n_heads // G
    GW = R * P
    halo = SUBLANES

    @pl.when(pl.program_id(1) == 0)
    def _():
        ext_ref[0:halo, :] = jnp.zeros((halo, ext_ref.shape[1]), F32)
        state_ref[...] = jnp.zeros_like(state_ref)

    ext_ref[halo:halo + L, :] = xbc_ref[...]

    def conv_silu(c0, width):
        acc = jnp.broadcast_to(cb_ref[:, c0:c0 + width], (L, width))
        for k in range(SSD_CONV):
            back = SSD_CONV - 1 - k
            acc = acc + ext_ref[halo - back:halo - back + L, c0:c0 + width] * cw_ref[k:k + 1, c0:c0 + width]
        return _silu(acc)

    dt = _softplus(dt_ref[...] + dtb_ref[...])
    a_neg = -jnp.exp(alog_ref[...])
    da = dt * a_neg
    rr = lax.broadcasted_iota(I32, (L, L), 0)
    cc = lax.broadcasted_iota(I32, (L, L), 1)
    lower = rr >= cc
    cs = _dot_f32_rhs(lower.astype(BF16), da)
    cs_last = cs[L - 1:L, :]
    cs_sq = jnp.concatenate([cs, jnp.zeros((L, L - n_heads), F32)], axis=1).T
    decay_to_end = jnp.exp(cs_last - cs)
    decay_from_start = jnp.exp(cs)
    chunk_decay = jnp.exp(cs_last)

    head_of_lane = lax.broadcasted_iota(I32, (n_heads, d_inner), 1) // P
    head_row = lax.broadcasted_iota(I32, (n_heads, d_inner), 0)
    spread = (head_row == head_of_lane).astype(BF16)
    rows8 = lambda v: jnp.broadcast_to(v, (SUBLANES, n_heads))
    per_head = jnp.concatenate([dt, decay_to_end, decay_from_start, rows8(chunk_decay), rows8(dskip_ref[...])],
                               axis=0)
    wide = _dot_f32_lhs(per_head, spread)
    lane_head = lax.broadcasted_iota(I32, (1, GW), 1) // P

    for g in range(G):
        cols = slice(g * GW, (g + 1) * GW)
        dt_g = wide[0:L, cols]
        to_end_g = wide[L:2 * L, cols]
        from_start_g = wide[2 * L:3 * L, cols]
        chunk_decay_g = wide[3 * L:3 * L + 1, cols]
        dskip_g = wide[3 * L + SUBLANES:3 * L + SUBLANES + 1, cols]

        x_g = conv_silu(g * GW, GW)
        b_g = conv_silu(d_inner + g * N, N)
        c_g = conv_silu(d_inner + G * N + g * N, N)
        xdt = x_g * dt_g
        b_bf = b_g.astype(BF16)
        c_bf = c_g.astype(BF16)

        cb = _dot(c_bf, b_bf, NT_DIMS)
        xdt_bf = xdt.astype(BF16)
        y_g = jnp.zeros((L, GW), F32)
        for r in range(R):
            hd = g * R + r
            seg = cs[:, hd:hd + 1] - cs_sq[hd:hd + 1, :]
            m = jnp.where(lower, cb * jnp.exp(seg), 0.0)
            x_r = jnp.where(lane_head == r, xdt_bf, jnp.zeros_like(xdt_bf))
            y_g = y_g + _dot(m.astype(BF16), x_r)
        prev = state_ref[g]
        y_g = y_g + from_start_g * _dot(c_bf, prev.astype(BF16))
        new_part = _dot(b_bf, (xdt * to_end_g).astype(BF16), TN_DIMS)
        state_ref[g] = prev * chunk_decay_g + new_part
        y_g = y_g + x_g * dskip_g

        yz = y_g * _silu(z_ref[:, g * GW:(g + 1) * GW])
        yn = yz * lax.rsqrt(jnp.mean(yz * yz, axis=-1, keepdims=True) + EPS)
        y_ref[:, g * GW:(g + 1) * GW] = (yn * nw_ref[:, g * GW:(g + 1) * GW]).astype(BF16)

    ext_ref[0:halo, :] = ext_ref[L:L + halo, :]


def _ssd_scan(xbc, z, dt, conv_w, conv_b, dt_bias, a_log, d_skip, norm_w, batch, seq):
    t, dx = xbc.shape
    d_inner = z.shape[1]
    n_heads = dt.shape[1]
    L = SSD_CHUNK
    nc = seq // L
    row = lambda n: pl.BlockSpec((1, n), lambda b, c: (0, 0))
    kern = functools.partial(_ssd_kernel, d_inner=d_inner, n_heads=n_heads)
    return pl.pallas_call(
        kern,
        grid=(batch, nc),
        in_specs=[
            pl.BlockSpec((L, dx), lambda b, c: (b * nc + c, 0)),
            pl.BlockSpec((L, d_inner), lambda b, c: (b * nc + c, 0)),
            pl.BlockSpec((L, n_heads), lambda b, c: (b * nc + c, 0)),
            pl.BlockSpec((SSD_CONV, dx), lambda b, c: (0, 0)),
            row(dx), row(n_heads), row(n_heads), row(n_heads), row(d_inner),
        ],
        out_specs=pl.BlockSpec((L, d_inner), lambda b, c: (b * nc + c, 0)),
        out_shape=jax.ShapeDtypeStruct((t, d_inner), BF16),
        scratch_shapes=[
            pltpu.VMEM((L + 2 * SUBLANES, dx), F32),
            pltpu.VMEM((SSD_GROUPS, SSD_STATE, d_inner // SSD_GROUPS), F32),
        ],
        compiler_params=_params("arbitrary", "arbitrary"),
        name="ssd_scan",
    )(xbc, z, dt, conv_w, conv_b, dt_bias, a_log, d_skip, norm_w)


def _rope_tables(seq):
    half = ATTN_HEAD_DIM // 2
    inv = ROPE_THETA ** (-jnp.arange(half, dtype=F32) / half)
    ang = jnp.arange(seq).astype(F32)[:, None] * inv[None, :]
    cos = jnp.cos(ang)
    sin = jnp.sin(ang)
    reps = LANES // ATTN_HEAD_DIM
    cos_t = jnp.tile(jnp.concatenate([cos, cos], axis=1), (1, reps))
    sin_t = jnp.tile(jnp.concatenate([-sin, sin], axis=1), (1, reps))
    return cos_t, sin_t


def kernel(x, mix_norm, ffn_norm, final_norm, attn_w_qkv, attn_w_o, ssd_w_in, ssd_conv_w, ssd_conv_b,
           ssd_dt_bias, ssd_a_log, ssd_d, ssd_norm, ssd_w_out, moe_w_group, moe_b_group, moe_w_router,
           moe_b_router, moe_w_up, moe_w_down):
    batch, seq, d = x.shape
    assert seq % MOBA_BLOCK == 0 and seq % SSD_CHUNK == 0 and d % LANES == 0
    depth = mix_norm.shape[0]
    t = batch * seq
    xt = x.reshape(t, d)
    cos_t, sin_t = _rope_tables(seq)
    final_w = final_norm[None, :]

    for i in range(depth):
        j = i // 2
        nw = mix_norm[i][None, :]
        if i % 2 == 0:
            w_qkv = attn_w_qkv[j].astype(BF16)
            q, k, vt, kmean = _qkv_rope(xt, nw, w_qkv[:, 0:2 * d], w_qkv[:, 2 * d:].T, cos_t, sin_t, seq)
            kmean = kmean.reshape(batch, seq // MOBA_BLOCK, d)
            o = _moba_attention(q, k, vt, kmean, batch, seq)
            xt = _proj_residual(o, attn_w_o[j].astype(BF16), xt)
        else:
            w_in = ssd_w_in[j].astype(BF16)
            d_inner = ssd_norm.shape[1]
            dx = ssd_conv_b.shape[1]
            z, xbc, dt = _ssd_in_proj(xt, nw, w_in[:, 0:d_inner], w_in[:, d_inner:d_inner + dx],
                                      w_in[:, d_inner + dx:])
            y = _ssd_scan(xbc, z, dt, ssd_conv_w[j][:, 0, :], ssd_conv_b[j][None, :], ssd_dt_bias[j][None, :],
                          ssd_a_log[j][None, :], ssd_d[j][None, :], ssd_norm[j][None, :], batch, seq)
            xt = _proj_residual(y, ssd_w_out[j].astype(BF16), xt)
        xt = _hier_moe(xt, ffn_norm[i][None, :], moe_w_group[i], moe_b_group[i], moe_w_router[i],
                       moe_b_router[i], moe_w_up, moe_w_down, i, final_w, final_norm=(i == depth - 1))
    return xt.reshape(batch, seq, d)
```

```python
import functools
import math

import jax
import jax.numpy as jnp
from jax import lax
from jax.experimental import pallas as pl
from jax.experimental.pallas import tpu as pltpu

F32 = jnp.float32
BF16 = jnp.bfloat16
I32 = jnp.int32

EPS = 1e-6
LANES = 128
SUBLANES = 8
VMEM_LIMIT = 56 * 1024 * 1024

ATTN_HEADS = 16
ATTN_HEAD_DIM = 64
MOBA_BLOCK = 256
MOBA_TOPK = 3
ROPE_THETA = 10000.0
HEADS_PER_TILE = LANES // ATTN_HEAD_DIM

SSD_HEAD_DIM = 64
SSD_GROUPS = 8
SSD_STATE = 128
SSD_CONV = 4
SSD_CONV_COLS = 512
SSD_CHUNK = 128

MOE_GROUPS = 4
MOE_EXPERTS_PER_GROUP = 8
MOE_EXPERTS = MOE_GROUPS * MOE_EXPERTS_PER_GROUP
MOE_ROW_BLOCK = 512
ROUTER_ROWS = SUBLANES + MOE_EXPERTS
MOE_TILE = 256
RUN_ALIGN = SUBLANES
RUN_SIZES = (16, 8)
MOE_TILE_BUF = 2 * MOE_TILE + MOE_EXPERTS * RUN_ALIGN
PAD_FILL_SIZES = tuple(MOE_ROW_BLOCK >> s for s in range((MOE_ROW_BLOCK // RUN_ALIGN).bit_length()))

NT_DIMS = (((1,), (1,)), ((), ()))
NN_DIMS = (((1,), (0,)), ((), ()))
TN_DIMS = (((0,), (0,)), ((), ()))


def _params(*sem):
    return pltpu.CompilerParams(dimension_semantics=sem, vmem_limit_bytes=VMEM_LIMIT)


def _rmsnorm_rows(x, w):
    var = jnp.mean(x * x, axis=-1, keepdims=True)
    return (x * lax.rsqrt(var + EPS)) * w


def _split3(x):
    hi = x.astype(BF16)
    r = x - hi.astype(F32)
    mid = r.astype(BF16)
    lo = (r - mid.astype(F32)).astype(BF16)
    return hi, mid, lo


def _dot(a, b, dims=NN_DIMS):
    return lax.dot_general(a, b, dims, preferred_element_type=F32)


def _dot_f32_lhs(x, e, dims=NN_DIMS):
    hi, mid, lo = _split3(x)
    return _dot(hi, e, dims) + _dot(mid, e, dims) + _dot(lo, e, dims)


def _dot_f32_rhs(e, x, dims=NN_DIMS):
    hi, mid, lo = _split3(x)
    return _dot(e, hi, dims) + _dot(e, mid, dims) + _dot(e, lo, dims)


def _dot_f32(a, b, dims=NN_DIMS):
    ah, am, al = _split3(a)
    bh, bm, bl = _split3(b)
    return (_dot(ah, bh, dims) + _dot(ah, bm, dims) + _dot(am, bh, dims)
            + _dot(ah, bl, dims) + _dot(am, bm, dims) + _dot(al, bh, dims))


def _silu(x):
    return x * (0.5 * jnp.tanh(0.5 * x) + 0.5)


def _softplus(x):
    return jnp.maximum(x, 0.0) + jnp.log(1.0 + jnp.exp(-jnp.abs(x)))


def _qkv_kernel(x_ref, nw_ref, wqk_ref, wvt_ref, cos_ref, sin_ref, q_ref, k_ref, vt_ref, km_ref, *, d, scale):
    h = _rmsnorm_rows(x_ref[...], nw_ref[...]).astype(BF16)
    cos = cos_ref[...]
    sin = sin_ref[...]
    lane = lax.broadcasted_iota(I32, (1, LANES), 1)
    first_half = (lane % ATTN_HEAD_DIM) < (ATTN_HEAD_DIM // 2)

    def rope_tile(xc):
        lo_partner = pltpu.roll(xc, LANES - ATTN_HEAD_DIM // 2, 1)
        hi_partner = pltpu.roll(xc, ATTN_HEAD_DIM // 2, 1)
        return xc * cos + jnp.where(first_half, lo_partner, hi_partner) * sin

    q = _dot(h, wqk_ref[:, 0:d])
    k = _dot(h, wqk_ref[:, d:2 * d])
    for c in range(d // LANES):
        sl = slice(c * LANES, (c + 1) * LANES)
        q_ref[:, sl] = (rope_tile(q[:, sl]) * scale).astype(BF16)
        kc = rope_tile(k[:, sl])
        k_ref[:, sl] = kc.astype(BF16)
        km_ref[0, :, sl] = jnp.mean(kc, axis=0, keepdims=True)
    vt_ref[0] = _dot(wvt_ref[...], h, NT_DIMS).astype(BF16)


def _qkv_rope(x, norm_w, w_qk, w_vt, cos_t, sin_t, seq):
    t, d = x.shape
    tm = MOBA_BLOCK
    n_tiles = t // tm
    seq_tiles = seq // tm
    kern = functools.partial(_qkv_kernel, d=d, scale=math.log2(math.e) / math.sqrt(ATTN_HEAD_DIM))
    return pl.pallas_call(
        kern,
        grid=(n_tiles,),
        in_specs=[
            pl.BlockSpec((tm, d), lambda i: (i, 0)),
            pl.BlockSpec((1, d), lambda i: (0, 0)),
            pl.BlockSpec((d, 2 * d), lambda i: (0, 0), pipeline_mode=pl.Buffered(1)),
            pl.BlockSpec((d, d), lambda i: (0, 0), pipeline_mode=pl.Buffered(1)),
            pl.BlockSpec((tm, LANES), lambda i: (i % seq_tiles, 0)),
            pl.BlockSpec((tm, LANES), lambda i: (i % seq_tiles, 0)),
        ],
        out_specs=[
            pl.BlockSpec((tm, d), lambda i: (i, 0)),
            pl.BlockSpec((tm, d), lambda i: (i, 0)),
            pl.BlockSpec((1, d, tm), lambda i: (i, 0, 0)),
            pl.BlockSpec((1, 1, d), lambda i: (i, 0, 0)),
        ],
        out_shape=[
            jax.ShapeDtypeStruct((t, d), BF16),
            jax.ShapeDtypeStruct((t, d), BF16),
            jax.ShapeDtypeStruct((n_tiles, d, tm), BF16),
            jax.ShapeDtypeStruct((n_tiles, 1, d), F32),
        ],
        compiler_params=_params("parallel"),
        name="qkv_rope",
    )(x, norm_w, w_qk, w_vt, cos_t, sin_t)


def _attn_kernel(q_ref, k_ref, vt_ref, km_ref, o_ref, sa_ref, sb_ref, p_ref, *, nb):
    qi = pl.program_id(2)
    tq = q_ref.shape[0]
    q2 = q_ref[...]
    km = km_ref[...]
    lane = lax.broadcasted_iota(I32, (1, LANES), 1)
    key = lax.broadcasted_iota(I32, (MOBA_BLOCK, tq), 0)
    qry = lax.broadcasted_iota(I32, (MOBA_BLOCK, tq), 1)
    causal = key <= qry
    blk = lax.broadcasted_iota(I32, (nb, tq), 0)
    neg_inf = jnp.float32(-jnp.inf)
    heads = range(HEADS_PER_TILE)

    qh = [jnp.where((lane // ATTN_HEAD_DIM) == h, q2, jnp.zeros_like(q2)) for h in heads]

    bits = []
    for h in heads:
        gate = _dot_f32_lhs(km, qh[h], NT_DIMS)
        alive = (blk < qi).astype(I32)
        sel = jnp.zeros((nb, tq), I32)
        for _ in range(min(MOBA_TOPK, nb)):
            gm = jnp.where(alive > 0, gate, neg_inf)
            top = jnp.max(gm, axis=0, keepdims=True)
            cand = jnp.where((alive > 0) & (gm == top), blk, nb)
            idx = jnp.min(cand, axis=0, keepdims=True)
            pick = (blk == idx).astype(I32)
            sel = sel | pick
            alive = alive & (1 - pick)
        bits.append(jnp.sum(sel << blk, axis=0, keepdims=True))

    def block_of(k):
        return jnp.where(k == 0, qi, jnp.minimum(k - 1, qi))

    def issue_scores(k, s_out):
        start = pl.multiple_of(block_of(k) * MOBA_BLOCK, MOBA_BLOCK)
        kb = k_ref[pl.ds(start, MOBA_BLOCK), :]
        for h in heads:
            s_out[h] = _dot(kb, qh[h], NT_DIMS)

    def issue_values(k):
        blk_idx = block_of(k)
        return [_dot(vt_ref[blk_idx, h * ATTN_HEAD_DIM:(h + 1) * ATTN_HEAD_DIM, :], p_ref[h]) for h in heads]

    def softmax(k, s_in, state, diag):
        out = []
        for h in heads:
            m_i, l_i = state[h]
            if diag:
                s = jnp.where(causal, s_in[h], neg_inf)
                m_new = jnp.maximum(m_i, jnp.max(s, axis=0, keepdims=True))
                shift = m_new
            else:
                s = s_in[h]
                keep = ((bits[h] >> (k - 1)) & 1) == 1
                m_new = jnp.maximum(m_i, jnp.where(keep, jnp.max(s, axis=0, keepdims=True), neg_inf))
                shift = jnp.where(keep, m_new, -neg_inf)
            alpha = jnp.exp2(m_i - m_new)
            p = jnp.exp2(s - shift)
            l_new = alpha * l_i + jnp.sum(p, axis=0, keepdims=True)
            p_ref[h] = p.astype(BF16)
            out.append((m_new, l_new, alpha))
        return out

    def phase(k, carry, s_cur, s_nxt):
        issue_scores(k + 1, s_nxt)
        pv = issue_values(k - 1)
        stats = softmax(k, s_cur, [(c[0], c[1]) for c in carry], False)
        return tuple((stats[h][0], stats[h][1], carry[h][3] * carry[h][2] + pv[h], stats[h][2]) for h in heads)

    issue_scores(0, sa_ref)
    issue_scores(1, sb_ref)
    first = softmax(0, sa_ref, [(jnp.full((1, tq), neg_inf, F32), jnp.zeros((1, tq), F32)) for _ in heads], True)
    carry = tuple((m, l, jnp.zeros((ATTN_HEAD_DIM, tq), F32), a) for (m, l, a) in first)

    def two_phases(u, c):
        c = phase(2 * u + 1, c, sb_ref, sa_ref)
        return phase(2 * u + 2, c, sa_ref, sb_ref)

    n_pairs = (qi + 1) // 2
    carry = lax.fori_loop(0, n_pairs, two_phases, carry)
    pv = issue_values(2 * n_pairs)
    o_t = jnp.concatenate([(alpha * acc + pv[h]) / l_i for h, (_, l_i, acc, alpha) in enumerate(carry)],
                          axis=0)
    o_ref[...] = o_t.T.astype(BF16)


def _moba_attention(q, k, vt, kmean, batch, seq):
    t, d = q.shape
    nb = seq // MOBA_BLOCK
    tq = MOBA_BLOCK
    nq = seq // tq
    col_tiles = d // LANES
    kern = functools.partial(_attn_kernel, nb=nb)
    return pl.pallas_call(
        kern,
        grid=(batch, col_tiles, nq),
        in_specs=[
            pl.BlockSpec((tq, LANES), lambda b, c, i: (b * nq + i, c)),
            pl.BlockSpec((seq, LANES), lambda b, c, i: (b, c)),
            pl.BlockSpec((nb, LANES, MOBA_BLOCK), lambda b, c, i: (b, c, 0)),
            pl.BlockSpec((None, nb, LANES), lambda b, c, i: (b, 0, c)),
        ],
        out_specs=pl.BlockSpec((tq, LANES), lambda b, c, i: (b * nq + i, c)),
        out_shape=jax.ShapeDtypeStruct((t, d), BF16),
        scratch_shapes=[
            pltpu.VMEM((HEADS_PER_TILE, MOBA_BLOCK, tq), F32),
            pltpu.VMEM((HEADS_PER_TILE, MOBA_BLOCK, tq), F32),
            pltpu.VMEM((HEADS_PER_TILE, MOBA_BLOCK, tq), BF16),
        ],
        compiler_params=_params("parallel", "parallel", "arbitrary"),
        name="moba_attention",
    )(q, k, vt, kmean)


def _proj_res_kernel(a_ref, w_ref, res_ref, o_ref):
    o_ref[...] = res_ref[...] + _dot(a_ref[...], w_ref[...])


def _proj_residual(a, w, res, tm=512):
    t, kdim = a.shape
    d = w.shape[1]
    return pl.pallas_call(
        _proj_res_kernel,
        grid=(t // tm,),
        in_specs=[
            pl.BlockSpec((tm, kdim), lambda i: (i, 0)),
            pl.BlockSpec((kdim, d), lambda i: (0, 0), pipeline_mode=pl.Buffered(1)),
            pl.BlockSpec((tm, d), lambda i: (i, 0)),
        ],
        out_specs=pl.BlockSpec((tm, d), lambda i: (i, 0)),
        out_shape=jax.ShapeDtypeStruct((t, d), F32),
        compiler_params=_params("parallel"),
        name="proj_residual",
    )(a, w, res)


def _router_kernel(x_ref, nw_ref, wr_ref, br_ref, gate_ref, tinfo_ref, cnt_ref, carry_ref):
    tm = x_ref.shape[0]

    @pl.when(pl.program_id(0) == 0)
    def _():
        carry_ref[...] = jnp.zeros_like(carry_ref)

    hn = _rmsnorm_rows(x_ref[...], nw_ref[...])
    logits = _dot_f32(wr_ref[...], hn, NT_DIMS) + br_ref[...]
    sub = lax.broadcasted_iota(I32, (SUBLANES, tm), 0)
    neg_inf = jnp.float32(-jnp.inf)

    gl = logits[0:SUBLANES]
    gmax = jnp.max(gl, axis=0, keepdims=True)
    gidx = jnp.min(jnp.where(gl == gmax, sub, SUBLANES), axis=0, keepdims=True)
    gprob = 1.0 / jnp.sum(jnp.exp(gl - gmax), axis=0, keepdims=True)

    el = jnp.zeros((SUBLANES, tm), F32)
    for g in range(MOE_GROUPS):
        el = jnp.where(gidx == g, logits[SUBLANES * (g + 1):SUBLANES * (g + 2)], el)
    m1 = jnp.max(el, axis=0, keepdims=True)
    i1 = jnp.min(jnp.where(el == m1, sub, SUBLANES), axis=0, keepdims=True)
    el2 = jnp.where(sub == i1, neg_inf, el)
    m2 = jnp.max(el2, axis=0, keepdims=True)
    i2 = jnp.min(jnp.where(el2 == m2, sub, SUBLANES), axis=0, keepdims=True)
    ratio = jnp.exp(m2 - m1)
    den = 1.0 + ratio
    gate_ref[...] = jnp.zeros_like(gate_ref)
    gate_ref[0:1, :] = gprob / den
    gate_ref[1:2, :] = gprob * ratio / den

    e1 = gidx * MOE_EXPERTS_PER_GROUP + i1
    e2 = gidx * MOE_EXPERTS_PER_GROUP + i2

    eiota = lax.broadcasted_iota(I32, (MOE_EXPERTS, tm), 0)
    oh1 = eiota == e1
    oh2 = eiota == e2
    onehot = (oh1 | oh2).astype(BF16)
    src = lax.broadcasted_iota(I32, (tm, tm), 0)
    dst = lax.broadcasted_iota(I32, (tm, tm), 1)
    before = (src < dst).astype(BF16)
    prefix = _dot(onehot, before)
    count = _dot(onehot, jnp.ones((tm, tm), BF16))
    total = jnp.floor((count + (RUN_ALIGN - 1)) * (1.0 / RUN_ALIGN)) * RUN_ALIGN
    erow = lax.broadcasted_iota(I32, (MOE_EXPERTS, MOE_EXPERTS), 0)
    ecol = lax.broadcasted_iota(I32, (MOE_EXPERTS, MOE_EXPERTS), 1)
    run_start = _dot_f32_rhs((ecol < erow).astype(BF16), total)
    slot = run_start + prefix
    gate_ref[2:3, :] = jnp.sum(jnp.where(oh1, slot, 0.0), axis=0, keepdims=True)
    gate_ref[3:4, :] = jnp.sum(jnp.where(oh2, slot, 0.0), axis=0, keepdims=True)

    lane = lax.broadcasted_iota(I32, (MOE_EXPERTS, tm), 1)
    on_lanes = lambda v: jnp.sum(jnp.where(eiota == lane, v, 0.0), axis=0, keepdims=True)[:, 0:LANES].astype(I32)
    tinfo_ref[...] = jnp.zeros_like(tinfo_ref)
    tinfo_ref[0, 0:1, :] = on_lanes(total)
    tinfo_ref[0, 1:2, :] = on_lanes(carry_ref[...])
    tinfo_ref[0, 2:3, :] = on_lanes(run_start)
    carry_ref[...] = carry_ref[...] + total
    cnt_ref[...] = carry_ref[...]


def _router(x, norm_w, wr_t, br):
    t, d = x.shape
    tm = MOE_TILE
    return pl.pallas_call(
        _router_kernel,
        grid=(t // tm,),
        in_specs=[
            pl.BlockSpec((tm, d), lambda i: (i, 0)),
            pl.BlockSpec((1, d), lambda i: (0, 0)),
            pl.BlockSpec((ROUTER_ROWS, d), lambda i: (0, 0)),
            pl.BlockSpec((ROUTER_ROWS, 1), lambda i: (0, 0)),
        ],
        out_specs=[
            pl.BlockSpec((SUBLANES, tm), lambda i: (0, i)),
            pl.BlockSpec((1, SUBLANES, LANES), lambda i: (i, 0, 0)),
            pl.BlockSpec((MOE_EXPERTS, tm), lambda i: (0, 0)),
        ],
        out_shape=[
            jax.ShapeDtypeStruct((SUBLANES, t), F32),
            jax.ShapeDtypeStruct((t // tm, SUBLANES, LANES), I32),
            jax.ShapeDtypeStruct((MOE_EXPERTS, tm), F32),
        ],
        scratch_shapes=[pltpu.VMEM((MOE_EXPERTS, tm), F32)],
        compiler_params=_params("arbitrary"),
        name="moe_router",
    )(x, norm_w, wr_t, br)


def _for_each_run_piece(tinfo_ref, pstart_ref, fn):
    full = RUN_SIZES[0]
    aligned = lambda v: pl.multiple_of(v, RUN_ALIGN)

    def per_expert(e, c):
        cnt = tinfo_ref[0, 0, e]
        sorted_row = pstart_ref[e] + tinfo_ref[0, 1, e]
        buf_row = tinfo_ref[0, 2, e]

        def whole(k, c2):
            fn(aligned(buf_row + k * full), aligned(sorted_row + k * full), full, 0)
            return c2

        lax.fori_loop(0, cnt // full, whole, 0)
        for cls, size in enumerate(RUN_SIZES[1:], 1):
            done = cnt & ~(2 * size - 1)

            @pl.when((cnt & size) != 0)
            def _():
                fn(aligned(buf_row + done), aligned(sorted_row + done), size, cls)
        return c

    lax.fori_loop(0, MOE_EXPERTS, per_expert, 0)


def _zero_unused_rows(padstart_ref, padgap_ref, nused_ref, rows_ref, zero_ref, sem):
    zero_ref[...] = jnp.zeros_like(zero_ref)
    block = PAD_FILL_SIZES[0]
    n_blocks = rows_ref.shape[0] // block

    def piece(row, n, cls):
        return pltpu.make_async_copy(zero_ref.at[pl.ds(0, n)], rows_ref.at[pl.ds(row, n)], sem.at[cls])

    def for_each_piece(fn):
        def per_expert(e, c):
            gap = padgap_ref[e]
            for cls, size in enumerate(PAD_FILL_SIZES[1:], 1):
                done = gap & ~(2 * size - 1)

                @pl.when((gap & size) != 0)
                def _():
                    fn(pl.multiple_of(padstart_ref[e] + done, RUN_ALIGN), size, cls)
            return c

        lax.fori_loop(0, MOE_EXPERTS, per_expert, 0)

        def unused_block(b, c):
            fn(pl.multiple_of(b * block, block), block, 0)
            return c

        lax.fori_loop(nused_ref[0], n_blocks, unused_block, 0)

    for_each_piece(lambda *a: piece(*a).start())
    for_each_piece(lambda *a: piece(*a).wait())


def _dispatch_kernel(pstart_ref, padstart_ref, padgap_ref, nused_ref, x_ref, nw_ref, slot_ref, tinfo_ref,
                     rows_ref, buf_ref, zero_ref, sem, zero_sem):
    tm = x_ref.shape[0]
    nbuf = buf_ref.shape[0]

    @pl.when(pl.program_id(0) == 0)
    def _():
        _zero_unused_rows(padstart_ref, padgap_ref, nused_ref, rows_ref, zero_ref, zero_sem)

    hn = _rmsnorm_rows(x_ref[...], nw_ref[...]).astype(BF16)
    r = lax.broadcasted_iota(I32, (nbuf, tm), 0)
    slots = slot_ref[...].astype(I32)
    place = ((r == slots[2:3, :]) | (r == slots[3:4, :])).astype(BF16)
    buf_ref[...] = _dot(place, hn)

    def piece(buf_row, sorted_row, n, cls):
        return pltpu.make_async_copy(buf_ref.at[pl.ds(buf_row, n)], rows_ref.at[pl.ds(sorted_row, n)], sem.at[cls])

    _for_each_run_piece(tinfo_ref, pstart_ref, lambda *a: piece(*a).start())
    _for_each_run_piece(tinfo_ref, pstart_ref, lambda *a: piece(*a).wait())


def _dispatch(pstart, pad_start, pad_gap, n_used, x, norm_w, gates, tinfo, cap):
    t, d = x.shape
    tm = MOE_TILE
    spec = lambda shape, index: pl.BlockSpec(shape, lambda i, *prefetch: index(i))
    grid_spec = pltpu.PrefetchScalarGridSpec(
        num_scalar_prefetch=4,
        grid=(t // tm,),
        in_specs=[
            spec((tm, d), lambda i: (i, 0)),
            spec((1, d), lambda i: (0, 0)),
            spec((SUBLANES, tm), lambda i: (0, i)),
            pl.BlockSpec((1, SUBLANES, LANES), lambda i, *prefetch: (i, 0, 0), memory_space=pltpu.SMEM),
        ],
        out_specs=pl.BlockSpec(memory_space=pl.ANY),
        scratch_shapes=[
            pltpu.VMEM((MOE_TILE_BUF, d), F32),
            pltpu.VMEM((PAD_FILL_SIZES[0], d), F32),
            pltpu.SemaphoreType.DMA((len(RUN_SIZES),)),
            pltpu.SemaphoreType.DMA((len(PAD_FILL_SIZES),)),
        ],
    )
    return pl.pallas_call(
        _dispatch_kernel,
        grid_spec=grid_spec,
        out_shape=jax.ShapeDtypeStruct((cap, d), F32),
        compiler_params=_params("arbitrary"),
        name="moe_dispatch",
    )(pstart, pad_start, pad_gap, n_used, x, norm_w, gates, tinfo)


def _expert_kernel(be_ref, nused_ref, rows_ref, wu_ref, wd_ref, y_ref, wu_bf_ref, wd_bf_ref):
    ff = wd_ref.shape[0]
    i = pl.program_id(0)
    used = i < nused_ref[0]
    new_expert = jnp.logical_or(i == 0, be_ref[i] != be_ref[jnp.maximum(i - 1, 0)])

    @pl.when(jnp.logical_and(used, new_expert))
    def _():
        wu_bf_ref[...] = wu_ref[...].astype(BF16)
        wd_bf_ref[...] = wd_ref[...].astype(BF16)

    @pl.when(used)
    def _():
        xb = rows_ref[...].astype(BF16)
        gu = _dot(xb, wu_bf_ref[...])
        act = _silu(gu[:, 0:ff]) * gu[:, ff:2 * ff]
        y_ref[...] = _dot(act.astype(BF16), wd_bf_ref[...])

    @pl.when(jnp.logical_not(used))
    def _():
        y_ref[...] = jnp.zeros_like(y_ref)


def _experts(block_expert, n_used, rows, w_up, w_down, layer):
    cap, d = rows.shape
    rb = MOE_ROW_BLOCK
    ff = w_down.shape[2]
    grid_spec = pltpu.PrefetchScalarGridSpec(
        num_scalar_prefetch=2,
        grid=(cap // rb,),
        in_specs=[
            pl.BlockSpec((rb, d), lambda i, be, nu: (jnp.minimum(i, nu[0] - 1), 0)),
            pl.BlockSpec((None, None, d, 2 * ff), lambda i, be, nu: (layer, be[i], 0, 0)),
            pl.BlockSpec((None, None, ff, d), lambda i, be, nu: (layer, be[i], 0, 0)),
        ],
        out_specs=pl.BlockSpec((rb, d), lambda i, be, nu: (i, 0)),
        scratch_shapes=[pltpu.VMEM((d, 2 * ff), BF16), pltpu.VMEM((ff, d), BF16)],
    )
    return pl.pallas_call(
        _expert_kernel,
        grid_spec=grid_spec,
        out_shape=jax.ShapeDtypeStruct((cap, d), F32),
        compiler_params=_params("arbitrary"),
        name="moe_experts",
    )(block_expert, n_used, rows, w_up, w_down)


def _combine_kernel(pstart_ref, x_ref, gate_ref, tinfo_ref, fw_ref, y_ref, o_ref, buf_ref, sem, *, final_norm):
    tm = x_ref.shape[0]
    nbuf = buf_ref.shape[0]

    @pl.when(pl.program_id(0) == 0)
    def _():
        buf_ref[...] = jnp.zeros_like(buf_ref)

    def piece(buf_row, sorted_row, n, cls):
        return pltpu.make_async_copy(y_ref.at[pl.ds(sorted_row, n)], buf_ref.at[pl.ds(buf_row, n)], sem.at[cls])

    _for_each_run_piece(tinfo_ref, pstart_ref, lambda *a: piece(*a).start())

    g = gate_ref[...]
    cols = []
    for c in range(tm // LANES):
        sq = jnp.concatenate([g[:, c * LANES:(c + 1) * LANES], jnp.zeros((LANES - SUBLANES, LANES), F32)], axis=0)
        cols.append(sq.T)
    cols = jnp.concatenate(cols, axis=0)
    r = lax.broadcasted_iota(I32, (tm, nbuf), 1)
    weights = (jnp.where(r == cols[:, 2:3].astype(I32), cols[:, 0:1], 0.0)
               + jnp.where(r == cols[:, 3:4].astype(I32), cols[:, 1:2], 0.0))
    w_hi = weights.astype(BF16)
    w_lo = (weights - w_hi.astype(F32)).astype(BF16)

    _for_each_run_piece(tinfo_ref, pstart_ref, lambda *a: piece(*a).wait())

    y = buf_ref[...]
    y_hi = y.astype(BF16)
    y_lo = (y - y_hi.astype(F32)).astype(BF16)
    out = x_ref[...] + (_dot(w_hi, y_hi) + _dot(w_lo, y_hi) + _dot(w_hi, y_lo))
    if final_norm:
        out = _rmsnorm_rows(out, fw_ref[...])
    o_ref[...] = out


def _combine(pstart, x, gates, tinfo, y_rows, final_w, final_norm):
    t, d = x.shape
    tm = MOE_TILE
    grid_spec = pltpu.PrefetchScalarGridSpec(
        num_scalar_prefetch=1,
        grid=(t // tm,),
        in_specs=[
            pl.BlockSpec((tm, d), lambda i, ps: (i, 0)),
            pl.BlockSpec((SUBLANES, tm), lambda i, ps: (0, i)),
            pl.BlockSpec((1, SUBLANES, LANES), lambda i, ps: (i, 0, 0), memory_space=pltpu.SMEM),
            pl.BlockSpec((1, d), lambda i, ps: (0, 0)),
            pl.BlockSpec(memory_space=pl.ANY),
        ],
        out_specs=pl.BlockSpec((tm, d), lambda i, ps: (i, 0)),
        scratch_shapes=[pltpu.VMEM((MOE_TILE_BUF, d), F32), pltpu.SemaphoreType.DMA((len(RUN_SIZES),))],
    )
    return pl.pallas_call(
        functools.partial(_combine_kernel, final_norm=final_norm),
        grid_spec=grid_spec,
        out_shape=jax.ShapeDtypeStruct((t, d), F32),
        compiler_params=_params("arbitrary"),
        name="moe_combine",
    )(pstart, x, gates, tinfo, final_w, y_rows)


def _hier_moe(x, norm_w, w_group, b_group, w_router, b_router, w_up, w_down, layer, final_w, final_norm):
    t, d = x.shape
    g, _, epg = w_router.shape
    pad_rows = SUBLANES - g
    wr_t = jnp.concatenate([w_group.T, jnp.zeros((pad_rows, d), F32),
                            w_router.transpose(0, 2, 1).reshape(g * epg, d)], axis=0)
    br = jnp.concatenate([b_group, jnp.full((pad_rows,), -1e30, F32), b_router.reshape(-1)])[:, None]
    gates, tinfo, cnt = _router(x, norm_w, wr_t, br)

    rb = MOE_ROW_BLOCK
    n_exp = g * epg
    counts = cnt[:, 0].astype(I32)
    padded = ((counts + rb - 1) // rb) * rb
    pends = jnp.cumsum(padded)
    pstart = (pends - padded).astype(I32)
    max_rows = 2 * t + (t // MOE_TILE) * n_exp * (RUN_ALIGN - 1) + n_exp * (rb - 1)
    n_blocks = -(-max_rows // rb)
    block_start = jnp.arange(n_blocks, dtype=I32) * rb
    block_expert = jnp.minimum(jnp.sum(block_start[:, None] >= pends[None, :], axis=1), n_exp - 1).astype(I32)
    n_used = (pends[-1:] // rb).astype(I32)

    rows = _dispatch(pstart, (pstart + counts).astype(I32), (padded - counts).astype(I32), n_used,
                     x, norm_w, gates, tinfo, n_blocks * rb)
    y_rows = _experts(block_expert, n_used, rows, w_up, w_down, layer)
    return _combine(pstart, x, gates, tinfo, y_rows, final_w, final_norm)


def _ssd_in_kernel(x_ref, nw_ref, wz_ref, wx_ref, wdt_ref, cw_ref, cb_ref, z_ref, xbc_ref, dt_ref, ext_ref,
                   *, seq_tiles):
    tm = x_ref.shape[0]
    dx = wx_ref.shape[1]
    halo = SUBLANES

    @pl.when(pl.program_id(0) % seq_tiles == 0)
    def _():
        ext_ref[...] = jnp.zeros_like(ext_ref)

    h = _rmsnorm_rows(x_ref[...], nw_ref[...]).astype(BF16)
    row = lax.broadcasted_iota(I32, (halo, SSD_CONV_COLS), 0)
    for c in range(dx // SSD_CONV_COLS):
        cols = slice(c * SSD_CONV_COLS, (c + 1) * SSD_CONV_COLS)
        cur = _dot(h, wx_ref[:, cols])
        prev = ext_ref[:, cols]
        acc = cb_ref[:, cols] + cur * cw_ref[SSD_CONV - 1:SSD_CONV, cols]
        for back in range(1, SSD_CONV):
            rolled = pltpu.roll(cur, back, 0)
            head = jnp.where(row < back, pltpu.roll(prev, back, 0), rolled[0:halo])
            shifted = jnp.concatenate([head, rolled[halo:]], axis=0)
            acc = acc + shifted * cw_ref[SSD_CONV - 1 - back:SSD_CONV - back, cols]
        xbc_ref[:, cols] = _silu(acc)
        ext_ref[:, cols] = cur[tm - halo:tm]
    z_ref[...] = _dot(h, wz_ref[...])
    dt_ref[...] = _dot(h, wdt_ref[...])


def _ssd_in_proj(x, norm_w, w_z, w_xbc, w_dt, conv_w, conv_b, seq, tm=256):
    t, d = x.shape
    dz, dx, dh = w_z.shape[1], w_xbc.shape[1], w_dt.shape[1]
    assert seq % tm == 0 and dx % SSD_CONV_COLS == 0
    resident = lambda n: pl.BlockSpec((d, n), lambda i: (0, 0), pipeline_mode=pl.Buffered(1))
    return pl.pallas_call(
        functools.partial(_ssd_in_kernel, seq_tiles=seq // tm),
        grid=(t // tm,),
        in_specs=[
            pl.BlockSpec((tm, d), lambda i: (i, 0)),
            pl.BlockSpec((1, d), lambda i: (0, 0)),
            resident(dz), resident(dx), resident(dh),
            pl.BlockSpec((SSD_CONV, dx), lambda i: (0, 0)),
            pl.BlockSpec((1, dx), lambda i: (0, 0)),
        ],
        out_specs=[
            pl.BlockSpec((tm, dz), lambda i: (i, 0)),
            pl.BlockSpec((tm, dx), lambda i: (i, 0)),
            pl.BlockSpec((tm, dh), lambda i: (i, 0)),
        ],
        out_shape=[
            jax.ShapeDtypeStruct((t, dz), F32),
            jax.ShapeDtypeStruct((t, dx), F32),
            jax.ShapeDtypeStruct((t, dh), F32),
        ],
        scratch_shapes=[pltpu.VMEM((SUBLANES, dx), F32)],
        compiler_params=_params("arbitrary"),
        name="ssd_in_proj",
    )(x, norm_w, w_z, w_xbc, w_dt, conv_w, conv_b)


def _ssd_kernel(xbc_ref, z_ref, dt_ref, dtb_ref, alog_ref, dskip_ref, nw_ref, y_ref, state_ref,
                *, d_inner, n_heads):
    L = SSD_CHUNK
    G = SSD_GROUPS
    N = SSD_STATE
    P = SSD_HEAD_DIM
    R = n_heads // G
    GW = R * P

    @pl.when(pl.program_id(1) == 0)
    def _():
        state_ref[...] = jnp.zeros_like(state_ref)

    dt = _softplus(dt_ref[...] + dtb_ref[...])
    a_neg = -jnp.exp(alog_ref[...])
    da = dt * a_neg
    rr = lax.broadcasted_iota(I32, (L, L), 0)
    cc = lax.broadcasted_iota(I32, (L, L), 1)
    lower = rr >= cc
    cs = _dot_f32_rhs(lower.astype(BF16), da)
    cs_last = cs[L - 1:L, :]
    cs_sq = jnp.concatenate([cs, jnp.zeros((L, L - n_heads), F32)], axis=1).T
    decay_to_end = jnp.exp(cs_last - cs)
    decay_from_start = jnp.exp(cs)
    chunk_decay = jnp.exp(cs_last)

    head_of_lane = lax.broadcasted_iota(I32, (n_heads, d_inner), 1) // P
    head_row = lax.broadcasted_iota(I32, (n_heads, d_inner), 0)
    spread = (head_row == head_of_lane).astype(BF16)
    rows8 = lambda v: jnp.broadcast_to(v, (SUBLANES, n_heads))
    per_head = jnp.concatenate([dt, decay_to_end, decay_from_start, rows8(chunk_decay), rows8(dskip_ref[...])],
                               axis=0)
    wide = _dot_f32_lhs(per_head, spread)
    lane_head = lax.broadcasted_iota(I32, (1, GW), 1) // P

    for g in range(G):
        cols = slice(g * GW, (g + 1) * GW)
        dt_g = wide[0:L, cols]
        to_end_g = wide[L:2 * L, cols]
        from_start_g = wide[2 * L:3 * L, cols]
        chunk_decay_g = wide[3 * L:3 * L + 1, cols]
        dskip_g = wide[3 * L + SUBLANES:3 * L + SUBLANES + 1, cols]

        x_g = xbc_ref[:, cols]
        b_bf = xbc_ref[:, d_inner + g * N:d_inner + (g + 1) * N].astype(BF16)
        c_bf = xbc_ref[:, d_inner + (G + g) * N:d_inner + (G + g + 1) * N].astype(BF16)
        xdt = x_g * dt_g

        cb = _dot(c_bf, b_bf, NT_DIMS)
        xdt_bf = xdt.astype(BF16)
        y_g = jnp.zeros((L, GW), F32)
        for r in range(R):
            hd = g * R + r
            seg = cs[:, hd:hd + 1] - cs_sq[hd:hd + 1, :]
            m = jnp.where(lower, cb * jnp.exp(seg), 0.0)
            x_r = jnp.where(lane_head == r, xdt_bf, jnp.zeros_like(xdt_bf))
            y_g = y_g + _dot(m.astype(BF16), x_r)
        prev = state_ref[g]
        y_g = y_g + from_start_g * _dot(c_bf, prev.astype(BF16))
        new_part = _dot(b_bf, (xdt * to_end_g).astype(BF16), TN_DIMS)
        state_ref[g] = prev * chunk_decay_g + new_part
        y_g = y_g + x_g * dskip_g

        yz = y_g * _silu(z_ref[:, g * GW:(g + 1) * GW])
        yn = yz * lax.rsqrt(jnp.mean(yz * yz, axis=-1, keepdims=True) + EPS)
        y_ref[:, g * GW:(g + 1) * GW] = (yn * nw_ref[:, g * GW:(g + 1) * GW]).astype(BF16)


def _ssd_scan(xbc, z, dt, dt_bias, a_log, d_skip, norm_w, batch, seq):
    t, dx = xbc.shape
    d_inner = z.shape[1]
    n_heads = dt.shape[1]
    L = SSD_CHUNK
    nc = seq // L
    row = lambda n: pl.BlockSpec((1, n), lambda b, c: (0, 0))
    kern = functools.partial(_ssd_kernel, d_inner=d_inner, n_heads=n_heads)
    return pl.pallas_call(
        kern,
        grid=(batch, nc),
        in_specs=[
            pl.BlockSpec((L, dx), lambda b, c: (b * nc + c, 0)),
            pl.BlockSpec((L, d_inner), lambda b, c: (b * nc + c, 0)),
            pl.BlockSpec((L, n_heads), lambda b, c: (b * nc + c, 0)),
            row(n_heads), row(n_heads), row(n_heads), row(d_inner),
        ],
        out_specs=pl.BlockSpec((L, d_inner), lambda b, c: (b * nc + c, 0)),
        out_shape=jax.ShapeDtypeStruct((t, d_inner), BF16),
        scratch_shapes=[pltpu.VMEM((SSD_GROUPS, SSD_STATE, d_inner // SSD_GROUPS), F32)],
        compiler_params=_params("arbitrary", "arbitrary"),
        name="ssd_scan",
    )(xbc, z, dt, dt_bias, a_log, d_skip, norm_w)


def _rope_tables(seq):
    half = ATTN_HEAD_DIM // 2
    inv = ROPE_THETA ** (-jnp.arange(half, dtype=F32) / half)
    ang = jnp.arange(seq).astype(F32)[:, None] * inv[None, :]
    cos = jnp.cos(ang)
    sin = jnp.sin(ang)
    reps = LANES // ATTN_HEAD_DIM
    cos_t = jnp.tile(jnp.concatenate([cos, cos], axis=1), (1, reps))
    sin_t = jnp.tile(jnp.concatenate([-sin, sin], axis=1), (1, reps))
    return cos_t, sin_t


def kernel(x, mix_norm, ffn_norm, final_norm, attn_w_qkv, attn_w_o, ssd_w_in, ssd_conv_w, ssd_conv_b,
           ssd_dt_bias, ssd_a_log, ssd_d, ssd_norm, ssd_w_out, moe_w_group, moe_b_group, moe_w_router,
           moe_b_router, moe_w_up, moe_w_down):
    batch, seq, d = x.shape
    assert seq % MOBA_BLOCK == 0 and seq % SSD_CHUNK == 0 and d % LANES == 0
    depth = mix_norm.shape[0]
    t = batch * seq
    xt = x.reshape(t, d)
    cos_t, sin_t = _rope_tables(seq)
    final_w = final_norm[None, :]

    for i in range(depth):
        j = i // 2
        nw = mix_norm[i][None, :]
        if i % 2 == 0:
            w_qkv = attn_w_qkv[j].astype(BF16)
            q, k, vt, kmean = _qkv_rope(xt, nw, w_qkv[:, 0:2 * d], w_qkv[:, 2 * d:].T, cos_t, sin_t, seq)
            kmean = kmean.reshape(batch, seq // MOBA_BLOCK, d)
            o = _moba_attention(q, k, vt, kmean, batch, seq)
            xt = _proj_residual(o, attn_w_o[j].astype(BF16), xt)
        else:
            w_in = ssd_w_in[j].astype(BF16)
            d_inner = ssd_norm.shape[1]
            dx = ssd_conv_b.shape[1]
            z, xbc, dt = _ssd_in_proj(xt, nw, w_in[:, 0:d_inner], w_in[:, d_inner:d_inner + dx],
                                      w_in[:, d_inner + dx:], ssd_conv_w[j][:, 0, :], ssd_conv_b[j][None, :], seq)
            y = _ssd_scan(xbc, z, dt, ssd_dt_bias[j][None, :], ssd_a_log[j][None, :], ssd_d[j][None, :],
                          ssd_norm[j][None, :], batch, seq)
            xt = _proj_residual(y, ssd_w_out[j].astype(BF16), xt)
        xt = _hier_moe(xt, ffn_norm[i][None, :], moe_w_group[i], moe_b_group[i], moe_w_router[i],
                       moe_b_router[i], moe_w_up, moe_w_down, i, final_w, final_norm=(i == depth - 1))
    return xt.reshape(batch, seq, d)
```

```python
import functools
import math

import jax
import jax.numpy as jnp
from jax import lax
from jax.experimental import pallas as pl
from jax.experimental.pallas import tpu as pltpu

F32 = jnp.float32
BF16 = jnp.bfloat16
I32 = jnp.int32

EPS = 1e-6
LANES = 128
SUBLANES = 8
VMEM_LIMIT = 56 * 1024 * 1024

ATTN_HEADS = 16
ATTN_HEAD_DIM = 64
MOBA_BLOCK = 256
MOBA_TOPK = 3
ROPE_THETA = 10000.0
HEADS_PER_TILE = LANES // ATTN_HEAD_DIM

SSD_HEAD_DIM = 64
SSD_GROUPS = 8
SSD_STATE = 128
SSD_CONV = 4
SSD_CHUNK = 128

MOE_GROUPS = 4
MOE_EXPERTS_PER_GROUP = 8
MOE_EXPERTS = MOE_GROUPS * MOE_EXPERTS_PER_GROUP
MOE_ROW_BLOCK = 512
ROUTER_ROWS = SUBLANES + MOE_EXPERTS
MOE_TILE = 256
RUN_ALIGN = SUBLANES
RUN_SIZES = (16, 8)
MOE_TILE_BUF = 2 * MOE_TILE + MOE_EXPERTS * RUN_ALIGN
PAD_FILL_SIZES = tuple(MOE_ROW_BLOCK >> s for s in range((MOE_ROW_BLOCK // RUN_ALIGN).bit_length()))

NT_DIMS = (((1,), (1,)), ((), ()))
NN_DIMS = (((1,), (0,)), ((), ()))
TN_DIMS = (((0,), (0,)), ((), ()))


def _params(*sem):
    return pltpu.CompilerParams(dimension_semantics=sem, vmem_limit_bytes=VMEM_LIMIT)


def _rmsnorm_rows(x, w):
    var = jnp.mean(x * x, axis=-1, keepdims=True)
    return (x * lax.rsqrt(var + EPS)) * w


def _split3(x):
    hi = x.astype(BF16)
    r = x - hi.astype(F32)
    mid = r.astype(BF16)
    lo = (r - mid.astype(F32)).astype(BF16)
    return hi, mid, lo


def _dot(a, b, dims=NN_DIMS):
    return lax.dot_general(a, b, dims, preferred_element_type=F32)


def _dot_f32_lhs(x, e, dims=NN_DIMS):
    hi, mid, lo = _split3(x)
    return _dot(hi, e, dims) + _dot(mid, e, dims) + _dot(lo, e, dims)


def _dot_f32_rhs(e, x, dims=NN_DIMS):
    hi, mid, lo = _split3(x)
    return _dot(e, hi, dims) + _dot(e, mid, dims) + _dot(e, lo, dims)


def _dot_f32(a, b, dims=NN_DIMS):
    ah, am, al = _split3(a)
    bh, bm, bl = _split3(b)
    return (_dot(ah, bh, dims) + _dot(ah, bm, dims) + _dot(am, bh, dims)
            + _dot(ah, bl, dims) + _dot(am, bm, dims) + _dot(al, bh, dims))


def _silu(x):
    return x * (0.5 * jnp.tanh(0.5 * x) + 0.5)


def _softplus(x):
    return jnp.maximum(x, 0.0) + jnp.log(1.0 + jnp.exp(-jnp.abs(x)))


def _qkv_kernel(x_ref, nw_ref, wqk_ref, wvt_ref, cos_ref, sin_ref, q_ref, k_ref, vt_ref, km_ref, *, d, scale):
    h = _rmsnorm_rows(x_ref[...], nw_ref[...]).astype(BF16)
    cos = cos_ref[...]
    sin = sin_ref[...]
    lane = lax.broadcasted_iota(I32, (1, LANES), 1)
    first_half = (lane % ATTN_HEAD_DIM) < (ATTN_HEAD_DIM // 2)

    def rope_tile(xc):
        lo_partner = pltpu.roll(xc, LANES - ATTN_HEAD_DIM // 2, 1)
        hi_partner = pltpu.roll(xc, ATTN_HEAD_DIM // 2, 1)
        return xc * cos + jnp.where(first_half, lo_partner, hi_partner) * sin

    q = _dot(h, wqk_ref[:, 0:d])
    k = _dot(h, wqk_ref[:, d:2 * d])
    for c in range(d // LANES):
        sl = slice(c * LANES, (c + 1) * LANES)
        q_ref[:, sl] = (rope_tile(q[:, sl]) * scale).astype(BF16)
        kc = rope_tile(k[:, sl])
        k_ref[:, sl] = kc.astype(BF16)
        km_ref[0, :, sl] = jnp.mean(kc, axis=0, keepdims=True)
    vt_ref[0] = _dot(wvt_ref[...], h, NT_DIMS).astype(BF16)


def _qkv_rope(x, norm_w, w_qk, w_vt, cos_t, sin_t, seq):
    t, d = x.shape
    tm = MOBA_BLOCK
    n_tiles = t // tm
    seq_tiles = seq // tm
    kern = functools.partial(_qkv_kernel, d=d, scale=math.log2(math.e) / math.sqrt(ATTN_HEAD_DIM))
    return pl.pallas_call(
        kern,
        grid=(n_tiles,),
        in_specs=[
            pl.BlockSpec((tm, d), lambda i: (i, 0)),
            pl.BlockSpec((1, d), lambda i: (0, 0)),
            pl.BlockSpec((d, 2 * d), lambda i: (0, 0), pipeline_mode=pl.Buffered(1)),
            pl.BlockSpec((d, d), lambda i: (0, 0), pipeline_mode=pl.Buffered(1)),
            pl.BlockSpec((tm, LANES), lambda i: (i % seq_tiles, 0)),
            pl.BlockSpec((tm, LANES), lambda i: (i % seq_tiles, 0)),
        ],
        out_specs=[
            pl.BlockSpec((tm, d), lambda i: (i, 0)),
            pl.BlockSpec((tm, d), lambda i: (i, 0)),
            pl.BlockSpec((1, d, tm), lambda i: (i, 0, 0)),
            pl.BlockSpec((1, 1, d), lambda i: (i, 0, 0)),
        ],
        out_shape=[
            jax.ShapeDtypeStruct((t, d), BF16),
            jax.ShapeDtypeStruct((t, d), BF16),
            jax.ShapeDtypeStruct((n_tiles, d, tm), BF16),
            jax.ShapeDtypeStruct((n_tiles, 1, d), F32),
        ],
        compiler_params=_params("parallel"),
        name="qkv_rope",
    )(x, norm_w, w_qk, w_vt, cos_t, sin_t)


def _attn_kernel(q_ref, k_ref, vt_ref, km_ref, o_ref, sa_ref, sb_ref, p_ref, *, nb):
    qi = pl.program_id(2)
    tq = q_ref.shape[0]
    q2 = q_ref[...]
    km = km_ref[...]
    lane = lax.broadcasted_iota(I32, (1, LANES), 1)
    key = lax.broadcasted_iota(I32, (MOBA_BLOCK, tq), 0)
    qry = lax.broadcasted_iota(I32, (MOBA_BLOCK, tq), 1)
    causal = key <= qry
    blk = lax.broadcasted_iota(I32, (nb, tq), 0)
    neg_inf = jnp.float32(-jnp.inf)
    heads = range(HEADS_PER_TILE)

    qh = [jnp.where((lane // ATTN_HEAD_DIM) == h, q2, jnp.zeros_like(q2)) for h in heads]

    bits = []
    for h in heads:
        gate = _dot_f32_lhs(km, qh[h], NT_DIMS)
        alive = (blk < qi).astype(I32)
        sel = jnp.zeros((nb, tq), I32)
        for _ in range(min(MOBA_TOPK, nb)):
            gm = jnp.where(alive > 0, gate, neg_inf)
            top = jnp.max(gm, axis=0, keepdims=True)
            cand = jnp.where((alive > 0) & (gm == top), blk, nb)
            idx = jnp.min(cand, axis=0, keepdims=True)
            pick = (blk == idx).astype(I32)
            sel = sel | pick
            alive = alive & (1 - pick)
        bits.append(jnp.sum(sel << blk, axis=0, keepdims=True))

    def block_of(k):
        return jnp.where(k == 0, qi, jnp.minimum(k - 1, qi))

    def issue_scores(k, s_out):
        start = pl.multiple_of(block_of(k) * MOBA_BLOCK, MOBA_BLOCK)
        kb = k_ref[pl.ds(start, MOBA_BLOCK), :]
        for h in heads:
            s_out[h] = _dot(kb, qh[h], NT_DIMS)

    def issue_values(k):
        blk_idx = block_of(k)
        return [_dot(vt_ref[blk_idx, h * ATTN_HEAD_DIM:(h + 1) * ATTN_HEAD_DIM, :], p_ref[h]) for h in heads]

    def softmax(k, s_in, state, diag):
        out = []
        for h in heads:
            m_i, l_i = state[h]
            if diag:
                s = jnp.where(causal, s_in[h], neg_inf)
                m_new = jnp.maximum(m_i, jnp.max(s, axis=0, keepdims=True))
                shift = m_new
            else:
                s = s_in[h]
                keep = ((bits[h] >> (k - 1)) & 1) == 1
                m_new = jnp.maximum(m_i, jnp.where(keep, jnp.max(s, axis=0, keepdims=True), neg_inf))
                shift = jnp.where(keep, m_new, -neg_inf)
            alpha = jnp.exp2(m_i - m_new)
            p = jnp.exp2(s - shift)
            l_new = alpha * l_i + jnp.sum(p, axis=0, keepdims=True)
            p_ref[h] = p.astype(BF16)
            out.append((m_new, l_new, alpha))
        return out

    def phase(k, carry, s_cur, s_nxt):
        issue_scores(k + 1, s_nxt)
        pv = issue_values(k - 1)
        stats = softmax(k, s_cur, [(c[0], c[1]) for c in carry], False)
        return tuple((stats[h][0], stats[h][1], carry[h][3] * carry[h][2] + pv[h], stats[h][2]) for h in heads)

    issue_scores(0, sa_ref)
    issue_scores(1, sb_ref)
    first = softmax(0, sa_ref, [(jnp.full((1, tq), neg_inf, F32), jnp.zeros((1, tq), F32)) for _ in heads], True)
    carry = tuple((m, l, jnp.zeros((ATTN_HEAD_DIM, tq), F32), a) for (m, l, a) in first)

    def two_phases(u, c):
        c = phase(2 * u + 1, c, sb_ref, sa_ref)
        return phase(2 * u + 2, c, sa_ref, sb_ref)

    n_pairs = (qi + 1) // 2
    carry = lax.fori_loop(0, n_pairs, two_phases, carry)
    pv = issue_values(2 * n_pairs)
    o_t = jnp.concatenate([(alpha * acc + pv[h]) / l_i for h, (_, l_i, acc, alpha) in enumerate(carry)],
                          axis=0)
    o_ref[...] = o_t.T.astype(BF16)


def _moba_attention(q, k, vt, kmean, batch, seq):
    t, d = q.shape
    nb = seq // MOBA_BLOCK
    tq = MOBA_BLOCK
    nq = seq // tq
    col_tiles = d // LANES
    kern = functools.partial(_attn_kernel, nb=nb)
    return pl.pallas_call(
        kern,
        grid=(batch, col_tiles, nq),
        in_specs=[
            pl.BlockSpec((tq, LANES), lambda b, c, i: (b * nq + i, c)),
            pl.BlockSpec((seq, LANES), lambda b, c, i: (b, c)),
            pl.BlockSpec((nb, LANES, MOBA_BLOCK), lambda b, c, i: (b, c, 0)),
            pl.BlockSpec((None, nb, LANES), lambda b, c, i: (b, 0, c)),
        ],
        out_specs=pl.BlockSpec((tq, LANES), lambda b, c, i: (b * nq + i, c)),
        out_shape=jax.ShapeDtypeStruct((t, d), BF16),
        scratch_shapes=[
            pltpu.VMEM((HEADS_PER_TILE, MOBA_BLOCK, tq), F32),
            pltpu.VMEM((HEADS_PER_TILE, MOBA_BLOCK, tq), F32),
            pltpu.VMEM((HEADS_PER_TILE, MOBA_BLOCK, tq), BF16),
        ],
        compiler_params=_params("parallel", "parallel", "arbitrary"),
        name="moba_attention",
    )(q, k, vt, kmean)


def _proj_res_kernel(a_ref, w_ref, res_ref, o_ref):
    o_ref[...] = res_ref[...] + _dot(a_ref[...], w_ref[...])


def _proj_residual(a, w, res, tm=512):
    t, kdim = a.shape
    d = w.shape[1]
    return pl.pallas_call(
        _proj_res_kernel,
        grid=(t // tm,),
        in_specs=[
            pl.BlockSpec((tm, kdim), lambda i: (i, 0)),
            pl.BlockSpec((kdim, d), lambda i: (0, 0), pipeline_mode=pl.Buffered(1)),
            pl.BlockSpec((tm, d), lambda i: (i, 0)),
        ],
        out_specs=pl.BlockSpec((tm, d), lambda i: (i, 0)),
        out_shape=jax.ShapeDtypeStruct((t, d), F32),
        compiler_params=_params("parallel"),
        name="proj_residual",
    )(a, w, res)


def _router_kernel(x_ref, nw_ref, wr_ref, br_ref, gate_ref, tinfo_ref, cnt_ref, carry_ref):
    tm = x_ref.shape[0]

    @pl.when(pl.program_id(0) == 0)
    def _():
        carry_ref[...] = jnp.zeros_like(carry_ref)

    hn = _rmsnorm_rows(x_ref[...], nw_ref[...])
    logits = _dot_f32(wr_ref[...], hn, NT_DIMS) + br_ref[...]
    sub = lax.broadcasted_iota(I32, (SUBLANES, tm), 0)
    neg_inf = jnp.float32(-jnp.inf)

    gl = logits[0:SUBLANES]
    gmax = jnp.max(gl, axis=0, keepdims=True)
    gidx = jnp.min(jnp.where(gl == gmax, sub, SUBLANES), axis=0, keepdims=True)
    gprob = 1.0 / jnp.sum(jnp.exp(gl - gmax), axis=0, keepdims=True)

    el = jnp.zeros((SUBLANES, tm), F32)
    for g in range(MOE_GROUPS):
        el = jnp.where(gidx == g, logits[SUBLANES * (g + 1):SUBLANES * (g + 2)], el)
    m1 = jnp.max(el, axis=0, keepdims=True)
    i1 = jnp.min(jnp.where(el == m1, sub, SUBLANES), axis=0, keepdims=True)
    el2 = jnp.where(sub == i1, neg_inf, el)
    m2 = jnp.max(el2, axis=0, keepdims=True)
    i2 = jnp.min(jnp.where(el2 == m2, sub, SUBLANES), axis=0, keepdims=True)
    ratio = jnp.exp(m2 - m1)
    den = 1.0 + ratio
    gate_ref[...] = jnp.zeros_like(gate_ref)
    gate_ref[0:1, :] = gprob / den
    gate_ref[1:2, :] = gprob * ratio / den

    e1 = gidx * MOE_EXPERTS_PER_GROUP + i1
    e2 = gidx * MOE_EXPERTS_PER_GROUP + i2

    eiota = lax.broadcasted_iota(I32, (MOE_EXPERTS, tm), 0)
    oh1 = eiota == e1
    oh2 = eiota == e2
    onehot = (oh1 | oh2).astype(BF16)
    src = lax.broadcasted_iota(I32, (tm, tm), 0)
    dst = lax.broadcasted_iota(I32, (tm, tm), 1)
    before = (src < dst).astype(BF16)
    prefix = _dot(onehot, before)
    count = _dot(onehot, jnp.ones((tm, tm), BF16))
    total = jnp.floor((count + (RUN_ALIGN - 1)) * (1.0 / RUN_ALIGN)) * RUN_ALIGN
    erow = lax.broadcasted_iota(I32, (MOE_EXPERTS, MOE_EXPERTS), 0)
    ecol = lax.broadcasted_iota(I32, (MOE_EXPERTS, MOE_EXPERTS), 1)
    run_start = _dot_f32_rhs((ecol < erow).astype(BF16), total)
    slot = run_start + prefix
    gate_ref[2:3, :] = jnp.sum(jnp.where(oh1, slot, 0.0), axis=0, keepdims=True)
    gate_ref[3:4, :] = jnp.sum(jnp.where(oh2, slot, 0.0), axis=0, keepdims=True)

    lane = lax.broadcasted_iota(I32, (MOE_EXPERTS, tm), 1)
    on_lanes = lambda v: jnp.sum(jnp.where(eiota == lane, v, 0.0), axis=0, keepdims=True)[:, 0:LANES].astype(I32)
    tinfo_ref[...] = jnp.zeros_like(tinfo_ref)
    tinfo_ref[0, 0:1, :] = on_lanes(total)
    tinfo_ref[0, 1:2, :] = on_lanes(carry_ref[...])
    tinfo_ref[0, 2:3, :] = on_lanes(run_start)
    carry_ref[...] = carry_ref[...] + total
    cnt_ref[...] = carry_ref[...]


def _router(x, norm_w, wr_t, br):
    t, d = x.shape
    tm = MOE_TILE
    return pl.pallas_call(
        _router_kernel,
        grid=(t // tm,),
        in_specs=[
            pl.BlockSpec((tm, d), lambda i: (i, 0)),
            pl.BlockSpec((1, d), lambda i: (0, 0)),
            pl.BlockSpec((ROUTER_ROWS, d), lambda i: (0, 0)),
            pl.BlockSpec((ROUTER_ROWS, 1), lambda i: (0, 0)),
        ],
        out_specs=[
            pl.BlockSpec((SUBLANES, tm), lambda i: (0, i)),
            pl.BlockSpec((1, SUBLANES, LANES), lambda i: (i, 0, 0)),
            pl.BlockSpec((MOE_EXPERTS, tm), lambda i: (0, 0)),
        ],
        out_shape=[
            jax.ShapeDtypeStruct((SUBLANES, t), F32),
            jax.ShapeDtypeStruct((t // tm, SUBLANES, LANES), I32),
            jax.ShapeDtypeStruct((MOE_EXPERTS, tm), F32),
        ],
        scratch_shapes=[pltpu.VMEM((MOE_EXPERTS, tm), F32)],
        compiler_params=_params("arbitrary"),
        name="moe_router",
    )(x, norm_w, wr_t, br)


def _for_each_run_piece(tinfo_ref, pstart_ref, fn):
    full = RUN_SIZES[0]
    aligned = lambda v: pl.multiple_of(v, RUN_ALIGN)

    def per_expert(e, c):
        cnt = tinfo_ref[0, 0, e]
        sorted_row = pstart_ref[e] + tinfo_ref[0, 1, e]
        buf_row = tinfo_ref[0, 2, e]

        def whole(k, c2):
            fn(aligned(buf_row + k * full), aligned(sorted_row + k * full), full, 0)
            return c2

        lax.fori_loop(0, cnt // full, whole, 0)
        for cls, size in enumerate(RUN_SIZES[1:], 1):
            done = cnt & ~(2 * size - 1)

            @pl.when((cnt & size) != 0)
            def _():
                fn(aligned(buf_row + done), aligned(sorted_row + done), size, cls)
        return c

    lax.fori_loop(0, MOE_EXPERTS, per_expert, 0)


def _zero_unused_rows(padstart_ref, padgap_ref, nused_ref, rows_ref, zero_ref, sem):
    zero_ref[...] = jnp.zeros_like(zero_ref)
    block = PAD_FILL_SIZES[0]
    n_blocks = rows_ref.shape[0] // block

    def piece(row, n, cls):
        return pltpu.make_async_copy(zero_ref.at[pl.ds(0, n)], rows_ref.at[pl.ds(row, n)], sem.at[cls])

    def for_each_piece(fn):
        def per_expert(e, c):
            gap = padgap_ref[e]
            for cls, size in enumerate(PAD_FILL_SIZES[1:], 1):
                done = gap & ~(2 * size - 1)

                @pl.when((gap & size) != 0)
                def _():
                    fn(pl.multiple_of(padstart_ref[e] + done, RUN_ALIGN), size, cls)
            return c

        lax.fori_loop(0, MOE_EXPERTS, per_expert, 0)

        def unused_block(b, c):
            fn(pl.multiple_of(b * block, block), block, 0)
            return c

        lax.fori_loop(nused_ref[0], n_blocks, unused_block, 0)

    for_each_piece(lambda *a: piece(*a).start())
    for_each_piece(lambda *a: piece(*a).wait())


def _dispatch_kernel(pstart_ref, padstart_ref, padgap_ref, nused_ref, x_ref, nw_ref, slot_ref, tinfo_ref,
                     tprev_ref, rows_ref, buf_ref, zero_ref, sem, zero_sem):
    tm = x_ref.shape[0]
    nbuf = buf_ref.shape[1]
    i = pl.program_id(0)
    cur = i % 2

    @pl.when(i == 0)
    def _():
        _zero_unused_rows(padstart_ref, padgap_ref, nused_ref, rows_ref, zero_ref, zero_sem)

    hn = _rmsnorm_rows(x_ref[...], nw_ref[...]).astype(BF16)
    r = lax.broadcasted_iota(I32, (nbuf, tm), 0)
    slots = slot_ref[...].astype(I32)
    place = ((r == slots[2:3, :]) | (r == slots[3:4, :])).astype(BF16)
    buf_ref[cur] = _dot(place, hn)

    def piece_from(which):
        def piece(buf_row, sorted_row, n, cls):
            return pltpu.make_async_copy(buf_ref.at[which, pl.ds(buf_row, n)], rows_ref.at[pl.ds(sorted_row, n)],
                                         sem.at[which, cls])
        return piece

    @pl.when(i > 0)
    def _():
        _for_each_run_piece(tprev_ref, pstart_ref, lambda *a: piece_from(1 - cur)(*a).wait())

    _for_each_run_piece(tinfo_ref, pstart_ref, lambda *a: piece_from(cur)(*a).start())

    @pl.when(i == pl.num_programs(0) - 1)
    def _():
        _for_each_run_piece(tinfo_ref, pstart_ref, lambda *a: piece_from(cur)(*a).wait())


def _dispatch(pstart, pad_start, pad_gap, n_used, x, norm_w, gates, tinfo, cap):
    t, d = x.shape
    tm = MOE_TILE
    spec = lambda shape, index: pl.BlockSpec(shape, lambda i, *prefetch: index(i))
    grid_spec = pltpu.PrefetchScalarGridSpec(
        num_scalar_prefetch=4,
        grid=(t // tm,),
        in_specs=[
            spec((tm, d), lambda i: (i, 0)),
            spec((1, d), lambda i: (0, 0)),
            spec((SUBLANES, tm), lambda i: (0, i)),
            pl.BlockSpec((1, SUBLANES, LANES), lambda i, *prefetch: (i, 0, 0), memory_space=pltpu.SMEM),
            pl.BlockSpec((1, SUBLANES, LANES), lambda i, *prefetch: (jnp.maximum(i - 1, 0), 0, 0),
                         memory_space=pltpu.SMEM),
        ],
        out_specs=pl.BlockSpec(memory_space=pl.ANY),
        scratch_shapes=[
            pltpu.VMEM((2, MOE_TILE_BUF, d), F32),
            pltpu.VMEM((PAD_FILL_SIZES[0], d), F32),
            pltpu.SemaphoreType.DMA((2, len(RUN_SIZES))),
            pltpu.SemaphoreType.DMA((len(PAD_FILL_SIZES),)),
        ],
    )
    return pl.pallas_call(
        _dispatch_kernel,
        grid_spec=grid_spec,
        out_shape=jax.ShapeDtypeStruct((cap, d), F32),
        compiler_params=_params("arbitrary"),
        name="moe_dispatch",
    )(pstart, pad_start, pad_gap, n_used, x, norm_w, gates, tinfo, tinfo)


def _expert_kernel(be_ref, nused_ref, rows_ref, wu_ref, wd_ref, y_ref, wu_bf_ref, wd_bf_ref):
    ff = wd_ref.shape[0]
    i = pl.program_id(0)
    used = i < nused_ref[0]
    new_expert = jnp.logical_or(i == 0, be_ref[i] != be_ref[jnp.maximum(i - 1, 0)])

    @pl.when(jnp.logical_and(used, new_expert))
    def _():
        wu_bf_ref[...] = wu_ref[...].astype(BF16)
        wd_bf_ref[...] = wd_ref[...].astype(BF16)

    @pl.when(used)
    def _():
        xb = rows_ref[...].astype(BF16)
        gu = _dot(xb, wu_bf_ref[...])
        act = _silu(gu[:, 0:ff]) * gu[:, ff:2 * ff]
        y_ref[...] = _dot(act.astype(BF16), wd_bf_ref[...])

    @pl.when(jnp.logical_not(used))
    def _():
        y_ref[...] = jnp.zeros_like(y_ref)


def _experts(block_expert, n_used, rows, w_up, w_down, layer):
    cap, d = rows.shape
    rb = MOE_ROW_BLOCK
    ff = w_down.shape[2]
    grid_spec = pltpu.PrefetchScalarGridSpec(
        num_scalar_prefetch=2,
        grid=(cap // rb,),
        in_specs=[
            pl.BlockSpec((rb, d), lambda i, be, nu: (jnp.minimum(i, nu[0] - 1), 0)),
            pl.BlockSpec((None, None, d, 2 * ff), lambda i, be, nu: (layer, be[i], 0, 0)),
            pl.BlockSpec((None, None, ff, d), lambda i, be, nu: (layer, be[i], 0, 0)),
        ],
        out_specs=pl.BlockSpec((rb, d), lambda i, be, nu: (i, 0)),
        scratch_shapes=[pltpu.VMEM((d, 2 * ff), BF16), pltpu.VMEM((ff, d), BF16)],
    )
    return pl.pallas_call(
        _expert_kernel,
        grid_spec=grid_spec,
        out_shape=jax.ShapeDtypeStruct((cap, d), F32),
        compiler_params=_params("arbitrary"),
        name="moe_experts",
    )(block_expert, n_used, rows, w_up, w_down)


def _combine_kernel(pstart_ref, x_ref, gate_ref, tinfo_ref, tnext_ref, fw_ref, y_ref, o_ref, buf_ref, sem,
                    *, final_norm):
    tm = x_ref.shape[0]
    nbuf = buf_ref.shape[1]
    i = pl.program_id(0)
    cur = i % 2

    def piece_into(which):
        def piece(buf_row, sorted_row, n, cls):
            return pltpu.make_async_copy(y_ref.at[pl.ds(sorted_row, n)], buf_ref.at[which, pl.ds(buf_row, n)],
                                         sem.at[which, cls])
        return piece

    @pl.when(i == 0)
    def _():
        buf_ref[...] = jnp.zeros_like(buf_ref)
        _for_each_run_piece(tinfo_ref, pstart_ref, lambda *a: piece_into(cur)(*a).start())

    @pl.when(i + 1 < pl.num_programs(0))
    def _():
        _for_each_run_piece(tnext_ref, pstart_ref, lambda *a: piece_into(1 - cur)(*a).start())

    g = gate_ref[...]
    cols = []
    for c in range(tm // LANES):
        sq = jnp.concatenate([g[:, c * LANES:(c + 1) * LANES], jnp.zeros((LANES - SUBLANES, LANES), F32)], axis=0)
        cols.append(sq.T)
    cols = jnp.concatenate(cols, axis=0)
    r = lax.broadcasted_iota(I32, (tm, nbuf), 1)
    weights = (jnp.where(r == cols[:, 2:3].astype(I32), cols[:, 0:1], 0.0)
               + jnp.where(r == cols[:, 3:4].astype(I32), cols[:, 1:2], 0.0))
    w_hi = weights.astype(BF16)
    w_lo = (weights - w_hi.astype(F32)).astype(BF16)

    _for_each_run_piece(tinfo_ref, pstart_ref, lambda *a: piece_into(cur)(*a).wait())

    y = buf_ref[cur]
    y_hi = y.astype(BF16)
    y_lo = (y - y_hi.astype(F32)).astype(BF16)
    out = x_ref[...] + (_dot(w_hi, y_hi) + _dot(w_lo, y_hi) + _dot(w_hi, y_lo))
    if final_norm:
        out = _rmsnorm_rows(out, fw_ref[...])
    o_ref[...] = out


def _combine(pstart, x, gates, tinfo, y_rows, final_w, final_norm):
    t, d = x.shape
    tm = MOE_TILE
    last = t // tm - 1
    grid_spec = pltpu.PrefetchScalarGridSpec(
        num_scalar_prefetch=1,
        grid=(t // tm,),
        in_specs=[
            pl.BlockSpec((tm, d), lambda i, ps: (i, 0)),
            pl.BlockSpec((SUBLANES, tm), lambda i, ps: (0, i)),
            pl.BlockSpec((1, SUBLANES, LANES), lambda i, ps: (i, 0, 0), memory_space=pltpu.SMEM),
            pl.BlockSpec((1, SUBLANES, LANES), lambda i, ps: (jnp.minimum(i + 1, last), 0, 0),
                         memory_space=pltpu.SMEM),
            pl.BlockSpec((1, d), lambda i, ps: (0, 0)),
            pl.BlockSpec(memory_space=pl.ANY),
        ],
        out_specs=pl.BlockSpec((tm, d), lambda i, ps: (i, 0)),
        scratch_shapes=[pltpu.VMEM((2, MOE_TILE_BUF, d), F32), pltpu.SemaphoreType.DMA((2, len(RUN_SIZES)))],
    )
    return pl.pallas_call(
        functools.partial(_combine_kernel, final_norm=final_norm),
        grid_spec=grid_spec,
        out_shape=jax.ShapeDtypeStruct((t, d), F32),
        compiler_params=_params("arbitrary"),
        name="moe_combine",
    )(pstart, x, gates, tinfo, tinfo, final_w, y_rows)


def _hier_moe(x, norm_w, w_group, b_group, w_router, b_router, w_up, w_down, layer, final_w, final_norm):
    t, d = x.shape
    g, _, epg = w_router.shape
    pad_rows = SUBLANES - g
    wr_t = jnp.concatenate([w_group.T, jnp.zeros((pad_rows, d), F32),
                            w_router.transpose(0, 2, 1).reshape(g * epg, d)], axis=0)
    br = jnp.concatenate([b_group, jnp.full((pad_rows,), -1e30, F32), b_router.reshape(-1)])[:, None]
    gates, tinfo, cnt = _router(x, norm_w, wr_t, br)

    rb = MOE_ROW_BLOCK
    n_exp = g * epg
    counts = cnt[:, 0].astype(I32)
    padded = ((counts + rb - 1) // rb) * rb
    pends = jnp.cumsum(padded)
    pstart = (pends - padded).astype(I32)
    max_rows = 2 * t + (t // MOE_TILE) * n_exp * (RUN_ALIGN - 1) + n_exp * (rb - 1)
    n_blocks = -(-max_rows // rb)
    block_start = jnp.arange(n_blocks, dtype=I32) * rb
    block_expert = jnp.minimum(jnp.sum(block_start[:, None] >= pends[None, :], axis=1), n_exp - 1).astype(I32)
    n_used = (pends[-1:] // rb).astype(I32)

    rows = _dispatch(pstart, (pstart + counts).astype(I32), (padded - counts).astype(I32), n_used,
                     x, norm_w, gates, tinfo, n_blocks * rb)
    y_rows = _experts(block_expert, n_used, rows, w_up, w_down, layer)
    return _combine(pstart, x, gates, tinfo, y_rows, final_w, final_norm)


def _ssd_in_kernel(x_ref, nw_ref, wz_ref, wx_ref, wdt_ref, z_ref, xbc_ref, dt_ref):
    h = _rmsnorm_rows(x_ref[...], nw_ref[...]).astype(BF16)
    z_ref[...] = _dot(h, wz_ref[...])
    xbc_ref[...] = _dot(h, wx_ref[...])
    dt_ref[...] = _dot(h, wdt_ref[...])


def _ssd_in_proj(x, norm_w, w_z, w_xbc, w_dt, tm=256):
    t, d = x.shape
    dz, dx, dh = w_z.shape[1], w_xbc.shape[1], w_dt.shape[1]
    resident = lambda n: pl.BlockSpec((d, n), lambda i: (0, 0), pipeline_mode=pl.Buffered(1))
    return pl.pallas_call(
        _ssd_in_kernel,
        grid=(t // tm,),
        in_specs=[
            pl.BlockSpec((tm, d), lambda i: (i, 0)),
            pl.BlockSpec((1, d), lambda i: (0, 0)),
            resident(dz), resident(dx), resident(dh),
        ],
        out_specs=[
            pl.BlockSpec((tm, dz), lambda i: (i, 0)),
            pl.BlockSpec((tm, dx), lambda i: (i, 0)),
            pl.BlockSpec((tm, dh), lambda i: (i, 0)),
        ],
        out_shape=[
            jax.ShapeDtypeStruct((t, dz), F32),
            jax.ShapeDtypeStruct((t, dx), F32),
            jax.ShapeDtypeStruct((t, dh), F32),
        ],
        compiler_params=_params("parallel"),
        name="ssd_in_proj",
    )(x, norm_w, w_z, w_xbc, w_dt)


def _ssd_kernel(xbc_ref, z_ref, dt_ref, cw_ref, cb_ref, dtb_ref, alog_ref, dskip_ref, nw_ref,
                y_ref, ext_ref, state_ref, *, d_inner, n_heads):
    L = SSD_CHUNK
    G = SSD_GROUPS
    N = SSD_STATE
    P = SSD_HEAD_DIM
    R = n_heads // G
    GW = R * P
    halo = SUBLANES

    @pl.when(pl.program_id(1) == 0)
    def _():
        ext_ref[0:halo, :] = jnp.zeros((halo, ext_ref.shape[1]), F32)
        state_ref[...] = jnp.zeros_like(state_ref)

    ext_ref[halo:halo + L, :] = xbc_ref[...]

    def conv_silu(c0, width):
        acc = jnp.broadcast_to(cb_ref[:, c0:c0 + width], (L, width))
        for k in range(SSD_CONV):
            back = SSD_CONV - 1 - k
            acc = acc + ext_ref[halo - back:halo - back + L, c0:c0 + width] * cw_ref[k:k + 1, c0:c0 + width]
        return _silu(acc)

    dt = _softplus(dt_ref[...] + dtb_ref[...])
    a_neg = -jnp.exp(alog_ref[...])
    da = dt * a_neg
    rr = lax.broadcasted_iota(I32, (L, L), 0)
    cc = lax.broadcasted_iota(I32, (L, L), 1)
    lower = rr >= cc
    cs = _dot_f32_rhs(lower.astype(BF16), da)
    cs_last = cs[L - 1:L, :]
    cs_sq = jnp.concatenate([cs, jnp.zeros((L, L - n_heads), F32)], axis=1).T
    decay_to_end = jnp.exp(cs_last - cs)
    decay_from_start = jnp.exp(cs)
    chunk_decay = jnp.exp(cs_last)

    head_of_lane = lax.broadcasted_iota(I32, (n_heads, d_inner), 1) // P
    head_row = lax.broadcasted_iota(I32, (n_heads, d_inner), 0)
    spread = (head_row == head_of_lane).astype(BF16)
    rows8 = lambda v: jnp.broadcast_to(v, (SUBLANES, n_heads))
    per_head = jnp.concatenate([dt, decay_to_end, decay_from_start, rows8(chunk_decay), rows8(dskip_ref[...])],
                               axis=0)
    wide = _dot_f32_lhs(per_head, spread)
    lane_head = lax.broadcasted_iota(I32, (1, GW), 1) // P

    for g in range(G):
        cols = slice(g * GW, (g + 1) * GW)
        dt_g = wide[0:L, cols]
        to_end_g = wide[L:2 * L, cols]
        from_start_g = wide[2 * L:3 * L, cols]
        chunk_decay_g = wide[3 * L:3 * L + 1, cols]
        dskip_g = wide[3 * L + SUBLANES:3 * L + SUBLANES + 1, cols]

        x_g = conv_silu(g * GW, GW)
        b_g = conv_silu(d_inner + g * N, N)
        c_g = conv_silu(d_inner + G * N + g * N, N)
        xdt = x_g * dt_g
        b_bf = b_g.astype(BF16)
        c_bf = c_g.astype(BF16)

        cb = _dot(c_bf, b_bf, NT_DIMS)
        xdt_bf = xdt.astype(BF16)
        y_g = jnp.zeros((L, GW), F32)
        for r in range(R):
            hd = g * R + r
            seg = cs[:, hd:hd + 1] - cs_sq[hd:hd + 1, :]
            m = jnp.where(lower, cb * jnp.exp(seg), 0.0)
            x_r = jnp.where(lane_head == r, xdt_bf, jnp.zeros_like(xdt_bf))
            y_g = y_g + _dot(m.astype(BF16), x_r)
        prev = state_ref[g]
        y_g = y_g + from_start_g * _dot(c_bf, prev.astype(BF16))
        new_part = _dot(b_bf, (xdt * to_end_g).astype(BF16), TN_DIMS)
        state_ref[g] = prev * chunk_decay_g + new_part
        y_g = y_g + x_g * dskip_g

        yz = y_g * _silu(z_ref[:, g * GW:(g + 1) * GW])
        yn = yz * lax.rsqrt(jnp.mean(yz * yz, axis=-1, keepdims=True) + EPS)
        y_ref[:, g * GW:(g + 1) * GW] = (yn * nw_ref[:, g * GW:(g + 1) * GW]).astype(BF16)

    ext_ref[0:halo, :] = ext_ref[L:L + halo, :]


def _ssd_scan(xbc, z, dt, conv_w, conv_b, dt_bias, a_log, d_skip, norm_w, batch, seq):
    t, dx = xbc.shape
    d_inner = z.shape[1]
    n_heads = dt.shape[1]
    L = SSD_CHUNK
    nc = seq // L
    row = lambda n: pl.BlockSpec((1, n), lambda b, c: (0, 0))
    kern = functools.partial(_ssd_kernel, d_inner=d_inner, n_heads=n_heads)
    return pl.pallas_call(
        kern,
        grid=(batch, nc),
        in_specs=[
            pl.BlockSpec((L, dx), lambda b, c: (b * nc + c, 0)),
            pl.BlockSpec((L, d_inner), lambda b, c: (b * nc + c, 0)),
            pl.BlockSpec((L, n_heads), lambda b, c: (b * nc + c, 0)),
            pl.BlockSpec((SSD_CONV, dx), lambda b, c: (0, 0)),
            row(dx), row(n_heads), row(n_heads), row(n_heads), row(d_inner),
        ],
        out_specs=pl.BlockSpec((L, d_inner), lambda b, c: (b * nc + c, 0)),
        out_shape=jax.ShapeDtypeStruct((t, d_inner), BF16),
        scratch_shapes=[
            pltpu.VMEM((L + 2 * SUBLANES, dx), F32),
            pltpu.VMEM((SSD_GROUPS, SSD_STATE, d_inner // SSD_GROUPS), F32),
        ],
        compiler_params=_params("arbitrary", "arbitrary"),
        name="ssd_scan",
    )(xbc, z, dt, conv_w, conv_b, dt_bias, a_log, d_skip, norm_w)


def _rope_tables(seq):
    half = ATTN_HEAD_DIM // 2
    inv = ROPE_THETA ** (-jnp.arange(half, dtype=F32) / half)
    ang = jnp.arange(seq).astype(F32)[:, None] * inv[None, :]
    cos = jnp.cos(ang)
    sin = jnp.sin(ang)
    reps = LANES // ATTN_HEAD_DIM
    cos_t = jnp.tile(jnp.concatenate([cos, cos], axis=1), (1, reps))
    sin_t = jnp.tile(jnp.concatenate([-sin, sin], axis=1), (1, reps))
    return cos_t, sin_t


def kernel(x, mix_norm, ffn_norm, final_norm, attn_w_qkv, attn_w_o, ssd_w_in, ssd_conv_w, ssd_conv_b,
           ssd_dt_bias, ssd_a_log, ssd_d, ssd_norm, ssd_w_out, moe_w_group, moe_b_group, moe_w_router,
           moe_b_router, moe_w_up, moe_w_down):
    batch, seq, d = x.shape
    assert seq % MOBA_BLOCK == 0 and seq % SSD_CHUNK == 0 and d % LANES == 0
    depth = mix_norm.shape[0]
    t = batch * seq
    xt = x.reshape(t, d)
    cos_t, sin_t = _rope_tables(seq)
    final_w = final_norm[None, :]

    for i in range(depth):
        j = i // 2
        nw = mix_norm[i][None, :]
        if i % 2 == 0:
            w_qkv = attn_w_qkv[j].astype(BF16)
            q, k, vt, kmean = _qkv_rope(xt, nw, w_qkv[:, 0:2 * d], w_qkv[:, 2 * d:].T, cos_t, sin_t, seq)
            kmean = kmean.reshape(batch, seq // MOBA_BLOCK, d)
            o = _moba_attention(q, k, vt, kmean, batch, seq)
            xt = _proj_residual(o, attn_w_o[j].astype(BF16), xt)
        else:
            w_in = ssd_w_in[j].astype(BF16)
            d_inner = ssd_norm.shape[1]
            dx = ssd_conv_b.shape[1]
            z, xbc, dt = _ssd_in_proj(xt, nw, w_in[:, 0:d_inner], w_in[:, d_inner:d_inner + dx],
                                      w_in[:, d_inner + dx:])
            y = _ssd_scan(xbc, z, dt, ssd_conv_w[j][:, 0, :], ssd_conv_b[j][None, :], ssd_dt_bias[j][None, :],
                          ssd_a_log[j][None, :], ssd_d[j][None, :], ssd_norm[j][None, :], batch, seq)
            xt = _proj_residual(y, ssd_w_out[j].astype(BF16), xt)
        xt = _hier_moe(xt, ffn_norm[i][None, :], moe_w_group[i], moe_b_group[i], moe_w_router[i],
                       moe_b_router[i], moe_w_up, moe_w_down, i, final_w, final_norm=(i == depth - 1))
    return xt.reshape(batch, seq, d)
```

```python
import functools
import math

import jax
import jax.numpy as jnp
from jax import lax
from jax.experimental import pallas as pl
from jax.experimental.pallas import tpu as pltpu

F32 = jnp.float32
BF16 = jnp.bfloat16
I32 = jnp.int32

EPS = 1e-6
LANES = 128
SUBLANES = 8
VMEM_LIMIT = 56 * 1024 * 1024

ATTN_HEADS = 16
ATTN_HEAD_DIM = 64
MOBA_BLOCK = 256
MOBA_TOPK = 3
ROPE_THETA = 10000.0
HEADS_PER_TILE = LANES // ATTN_HEAD_DIM

SSD_HEAD_DIM = 64
SSD_GROUPS = 8
SSD_STATE = 128
SSD_CONV = 4
SSD_CHUNK = 128

MOE_GROUPS = 4
MOE_EXPERTS_PER_GROUP = 8
MOE_EXPERTS = MOE_GROUPS * MOE_EXPERTS_PER_GROUP
MOE_ROW_BLOCK = 512
ROUTER_ROWS = SUBLANES + MOE_EXPERTS
MOE_TILE = 256
RUN_ALIGN = SUBLANES
RUN_SIZES = (16, 8)
MOE_TILE_BUF = 2 * MOE_TILE + MOE_EXPERTS * RUN_ALIGN
PAD_FILL_SIZES = tuple(MOE_ROW_BLOCK >> s for s in range((MOE_ROW_BLOCK // RUN_ALIGN).bit_length()))

NT_DIMS = (((1,), (1,)), ((), ()))
NN_DIMS = (((1,), (0,)), ((), ()))
TN_DIMS = (((0,), (0,)), ((), ()))


def _params(*sem):
    return pltpu.CompilerParams(dimension_semantics=sem, vmem_limit_bytes=VMEM_LIMIT)


def _rmsnorm_rows(x, w):
    var = jnp.mean(x * x, axis=-1, keepdims=True)
    return (x * lax.rsqrt(var + EPS)) * w


def _split3(x):
    hi = x.astype(BF16)
    r = x - hi.astype(F32)
    mid = r.astype(BF16)
    lo = (r - mid.astype(F32)).astype(BF16)
    return hi, mid, lo


def _dot(a, b, dims=NN_DIMS):
    return lax.dot_general(a, b, dims, preferred_element_type=F32)


def _dot_f32_lhs(x, e, dims=NN_DIMS):
    hi, mid, lo = _split3(x)
    return _dot(hi, e, dims) + _dot(mid, e, dims) + _dot(lo, e, dims)


def _dot_f32_rhs(e, x, dims=NN_DIMS):
    hi, mid, lo = _split3(x)
    return _dot(e, hi, dims) + _dot(e, mid, dims) + _dot(e, lo, dims)


def _dot_f32_3pass(a, b, dims=NN_DIMS):
    ah, am, _ = _split3(a)
    bh, bm, _ = _split3(b)
    return _dot(ah, bh, dims) + _dot(ah, bm, dims) + _dot(am, bh, dims)


def _silu(x):
    return x * (0.5 * jnp.tanh(0.5 * x) + 0.5)


def _softplus(x):
    return jnp.maximum(x, 0.0) + jnp.log(1.0 + jnp.exp(-jnp.abs(x)))


def _qkv_kernel(x_ref, nw_ref, wqk_ref, wvt_ref, cos_ref, sin_ref, q_ref, k_ref, vt_ref, km_ref, *, d, scale):
    h = _rmsnorm_rows(x_ref[...], nw_ref[...]).astype(BF16)
    cos = cos_ref[...]
    sin = sin_ref[...]
    lane = lax.broadcasted_iota(I32, (1, LANES), 1)
    first_half = (lane % ATTN_HEAD_DIM) < (ATTN_HEAD_DIM // 2)

    def rope_tile(xc):
        lo_partner = pltpu.roll(xc, LANES - ATTN_HEAD_DIM // 2, 1)
        hi_partner = pltpu.roll(xc, ATTN_HEAD_DIM // 2, 1)
        return xc * cos + jnp.where(first_half, lo_partner, hi_partner) * sin

    q = _dot(h, wqk_ref[:, 0:d])
    k = _dot(h, wqk_ref[:, d:2 * d])
    for c in range(d // LANES):
        sl = slice(c * LANES, (c + 1) * LANES)
        q_ref[:, sl] = (rope_tile(q[:, sl]) * scale).astype(BF16)
        kc = rope_tile(k[:, sl])
        k_ref[:, sl] = kc.astype(BF16)
        km_ref[0, :, sl] = jnp.mean(kc, axis=0, keepdims=True)
    vt_ref[0] = _dot(wvt_ref[...], h, NT_DIMS).astype(BF16)


def _qkv_rope(x, norm_w, w_qk, w_vt, cos_t, sin_t, seq):
    t, d = x.shape
    tm = MOBA_BLOCK
    n_tiles = t // tm
    seq_tiles = seq // tm
    kern = functools.partial(_qkv_kernel, d=d, scale=math.log2(math.e) / math.sqrt(ATTN_HEAD_DIM))
    return pl.pallas_call(
        kern,
        grid=(n_tiles,),
        in_specs=[
            pl.BlockSpec((tm, d), lambda i: (i, 0)),
            pl.BlockSpec((1, d), lambda i: (0, 0)),
            pl.BlockSpec((d, 2 * d), lambda i: (0, 0), pipeline_mode=pl.Buffered(1)),
            pl.BlockSpec((d, d), lambda i: (0, 0), pipeline_mode=pl.Buffered(1)),
            pl.BlockSpec((tm, LANES), lambda i: (i % seq_tiles, 0)),
            pl.BlockSpec((tm, LANES), lambda i: (i % seq_tiles, 0)),
        ],
        out_specs=[
            pl.BlockSpec((tm, d), lambda i: (i, 0)),
            pl.BlockSpec((tm, d), lambda i: (i, 0)),
            pl.BlockSpec((1, d, tm), lambda i: (i, 0, 0)),
            pl.BlockSpec((1, 1, d), lambda i: (i, 0, 0)),
        ],
        out_shape=[
            jax.ShapeDtypeStruct((t, d), BF16),
            jax.ShapeDtypeStruct((t, d), BF16),
            jax.ShapeDtypeStruct((n_tiles, d, tm), BF16),
            jax.ShapeDtypeStruct((n_tiles, 1, d), F32),
        ],
        compiler_params=_params("parallel"),
        name="qkv_rope",
    )(x, norm_w, w_qk, w_vt, cos_t, sin_t)


def _attn_kernel(q_ref, k_ref, vt_ref, km_ref, o_ref, sa_ref, sb_ref, p_ref, *, nb):
    qi = pl.program_id(2)
    tq = q_ref.shape[0]
    q2 = q_ref[...]
    km = km_ref[...]
    lane = lax.broadcasted_iota(I32, (1, LANES), 1)
    key = lax.broadcasted_iota(I32, (MOBA_BLOCK, tq), 0)
    qry = lax.broadcasted_iota(I32, (MOBA_BLOCK, tq), 1)
    causal = key <= qry
    blk = lax.broadcasted_iota(I32, (nb, tq), 0)
    neg_inf = jnp.float32(-jnp.inf)
    heads = range(HEADS_PER_TILE)

    qh = [jnp.where((lane // ATTN_HEAD_DIM) == h, q2, jnp.zeros_like(q2)) for h in heads]

    bits = []
    for h in heads:
        gate = _dot_f32_lhs(km, qh[h], NT_DIMS)
        alive = (blk < qi).astype(I32)
        sel = jnp.zeros((nb, tq), I32)
        for _ in range(min(MOBA_TOPK, nb)):
            gm = jnp.where(alive > 0, gate, neg_inf)
            top = jnp.max(gm, axis=0, keepdims=True)
            cand = jnp.where((alive > 0) & (gm == top), blk, nb)
            idx = jnp.min(cand, axis=0, keepdims=True)
            pick = (blk == idx).astype(I32)
            sel = sel | pick
            alive = alive & (1 - pick)
        bits.append(jnp.sum(sel << blk, axis=0, keepdims=True))

    def block_of(k):
        return jnp.where(k == 0, qi, jnp.minimum(k - 1, qi))

    def issue_scores(k, s_out):
        start = pl.multiple_of(block_of(k) * MOBA_BLOCK, MOBA_BLOCK)
        kb = k_ref[pl.ds(start, MOBA_BLOCK), :]
        for h in heads:
            s_out[h] = _dot(kb, qh[h], NT_DIMS)

    def issue_values(k):
        blk_idx = block_of(k)
        return [_dot(vt_ref[blk_idx, h * ATTN_HEAD_DIM:(h + 1) * ATTN_HEAD_DIM, :], p_ref[h]) for h in heads]

    def softmax(k, s_in, state, diag):
        out = []
        for h in heads:
            m_i, l_i = state[h]
            if diag:
                s = jnp.where(causal, s_in[h], neg_inf)
                m_new = jnp.maximum(m_i, jnp.max(s, axis=0, keepdims=True))
                shift = m_new
            else:
                s = s_in[h]
                keep = ((bits[h] >> (k - 1)) & 1) == 1
                m_new = jnp.maximum(m_i, jnp.where(keep, jnp.max(s, axis=0, keepdims=True), neg_inf))
                shift = jnp.where(keep, m_new, -neg_inf)
            alpha = jnp.exp2(m_i - m_new)
            p = jnp.exp2(s - shift)
            l_new = alpha * l_i + jnp.sum(p, axis=0, keepdims=True)
            p_ref[h] = p.astype(BF16)
            out.append((m_new, l_new, alpha))
        return out

    def phase(k, carry, s_cur, s_nxt):
        issue_scores(k + 1, s_nxt)
        pv = issue_values(k - 1)
        stats = softmax(k, s_cur, [(c[0], c[1]) for c in carry], False)
        return tuple((stats[h][0], stats[h][1], carry[h][3] * carry[h][2] + pv[h], stats[h][2]) for h in heads)

    issue_scores(0, sa_ref)
    issue_scores(1, sb_ref)
    first = softmax(0, sa_ref, [(jnp.full((1, tq), neg_inf, F32), jnp.zeros((1, tq), F32)) for _ in heads], True)
    carry = tuple((m, l, jnp.zeros((ATTN_HEAD_DIM, tq), F32), a) for (m, l, a) in first)

    def two_phases(u, c):
        c = phase(2 * u + 1, c, sb_ref, sa_ref)
        return phase(2 * u + 2, c, sa_ref, sb_ref)

    n_pairs = (qi + 1) // 2
    carry = lax.fori_loop(0, n_pairs, two_phases, carry)
    pv = issue_values(2 * n_pairs)
    o_t = jnp.concatenate([(alpha * acc + pv[h]) / l_i for h, (_, l_i, acc, alpha) in enumerate(carry)],
                          axis=0)
    o_ref[...] = o_t.T.astype(BF16)


def _moba_attention(q, k, vt, kmean, batch, seq):
    t, d = q.shape
    nb = seq // MOBA_BLOCK
    tq = MOBA_BLOCK
    nq = seq // tq
    col_tiles = d // LANES
    kern = functools.partial(_attn_kernel, nb=nb)
    return pl.pallas_call(
        kern,
        grid=(batch, col_tiles, nq),
        in_specs=[
            pl.BlockSpec((tq, LANES), lambda b, c, i: (b * nq + i, c)),
            pl.BlockSpec((seq, LANES), lambda b, c, i: (b, c)),
            pl.BlockSpec((nb, LANES, MOBA_BLOCK), lambda b, c, i: (b, c, 0)),
            pl.BlockSpec((None, nb, LANES), lambda b, c, i: (b, 0, c)),
        ],
        out_specs=pl.BlockSpec((tq, LANES), lambda b, c, i: (b * nq + i, c)),
        out_shape=jax.ShapeDtypeStruct((t, d), BF16),
        scratch_shapes=[
            pltpu.VMEM((HEADS_PER_TILE, MOBA_BLOCK, tq), F32),
            pltpu.VMEM((HEADS_PER_TILE, MOBA_BLOCK, tq), F32),
            pltpu.VMEM((HEADS_PER_TILE, MOBA_BLOCK, tq), BF16),
        ],
        compiler_params=_params("parallel", "parallel", "arbitrary"),
        name="moba_attention",
    )(q, k, vt, kmean)


def _proj_res_kernel(a_ref, w_ref, res_ref, o_ref):
    o_ref[...] = res_ref[...] + _dot(a_ref[...], w_ref[...])


def _proj_residual(a, w, res, tm=512):
    t, kdim = a.shape
    d = w.shape[1]
    return pl.pallas_call(
        _proj_res_kernel,
        grid=(t // tm,),
        in_specs=[
            pl.BlockSpec((tm, kdim), lambda i: (i, 0)),
            pl.BlockSpec((kdim, d), lambda i: (0, 0), pipeline_mode=pl.Buffered(1)),
            pl.BlockSpec((tm, d), lambda i: (i, 0)),
        ],
        out_specs=pl.BlockSpec((tm, d), lambda i: (i, 0)),
        out_shape=jax.ShapeDtypeStruct((t, d), F32),
        compiler_params=_params("parallel"),
        name="proj_residual",
    )(a, w, res)


def _router_kernel(x_ref, nw_ref, wr_ref, br_ref, gate_ref, tinfo_ref, cnt_ref, carry_ref):
    tm = x_ref.shape[0]

    @pl.when(pl.program_id(0) == 0)
    def _():
        carry_ref[...] = jnp.zeros_like(carry_ref)

    hn = _rmsnorm_rows(x_ref[...], nw_ref[...])
    logits = _dot_f32_3pass(wr_ref[...], hn, NT_DIMS) + br_ref[...]
    sub = lax.broadcasted_iota(I32, (SUBLANES, tm), 0)
    neg_inf = jnp.float32(-jnp.inf)

    gl = logits[0:SUBLANES]
    gmax = jnp.max(gl, axis=0, keepdims=True)
    gidx = jnp.min(jnp.where(gl == gmax, sub, SUBLANES), axis=0, keepdims=True)
    gprob = 1.0 / jnp.sum(jnp.exp(gl - gmax), axis=0, keepdims=True)

    el = jnp.zeros((SUBLANES, tm), F32)
    for g in range(MOE_GROUPS):
        el = jnp.where(gidx == g, logits[SUBLANES * (g + 1):SUBLANES * (g + 2)], el)
    m1 = jnp.max(el, axis=0, keepdims=True)
    i1 = jnp.min(jnp.where(el == m1, sub, SUBLANES), axis=0, keepdims=True)
    el2 = jnp.where(sub == i1, neg_inf, el)
    m2 = jnp.max(el2, axis=0, keepdims=True)
    i2 = jnp.min(jnp.where(el2 == m2, sub, SUBLANES), axis=0, keepdims=True)
    ratio = jnp.exp(m2 - m1)
    den = 1.0 + ratio
    gate_ref[...] = jnp.zeros_like(gate_ref)
    gate_ref[0:1, :] = gprob / den
    gate_ref[1:2, :] = gprob * ratio / den

    e1 = gidx * MOE_EXPERTS_PER_GROUP + i1
    e2 = gidx * MOE_EXPERTS_PER_GROUP + i2

    eiota = lax.broadcasted_iota(I32, (MOE_EXPERTS, tm), 0)
    oh1 = eiota == e1
    oh2 = eiota == e2
    onehot = (oh1 | oh2).astype(BF16)
    src = lax.broadcasted_iota(I32, (tm, tm), 0)
    dst = lax.broadcasted_iota(I32, (tm, tm), 1)
    before = (src < dst).astype(BF16)
    prefix = _dot(onehot, before)
    count = _dot(onehot, jnp.ones((tm, tm), BF16))
    total = jnp.floor((count + (RUN_ALIGN - 1)) * (1.0 / RUN_ALIGN)) * RUN_ALIGN
    erow = lax.broadcasted_iota(I32, (MOE_EXPERTS, MOE_EXPERTS), 0)
    ecol = lax.broadcasted_iota(I32, (MOE_EXPERTS, MOE_EXPERTS), 1)
    run_start = _dot_f32_rhs((ecol < erow).astype(BF16), total)
    slot = run_start + prefix
    gate_ref[2:3, :] = jnp.sum(jnp.where(oh1, slot, 0.0), axis=0, keepdims=True)
    gate_ref[3:4, :] = jnp.sum(jnp.where(oh2, slot, 0.0), axis=0, keepdims=True)

    lane = lax.broadcasted_iota(I32, (MOE_EXPERTS, tm), 1)
    on_lanes = lambda v: jnp.sum(jnp.where(eiota == lane, v, 0.0), axis=0, keepdims=True)[:, 0:LANES].astype(I32)
    tinfo_ref[...] = jnp.zeros_like(tinfo_ref)
    tinfo_ref[0, 0:1, :] = on_lanes(total)
    tinfo_ref[0, 1:2, :] = on_lanes(carry_ref[...])
    tinfo_ref[0, 2:3, :] = on_lanes(run_start)
    carry_ref[...] = carry_ref[...] + total
    cnt_ref[...] = carry_ref[...]


def _router(x, norm_w, wr_t, br):
    t, d = x.shape
    tm = MOE_TILE
    return pl.pallas_call(
        _router_kernel,
        grid=(t // tm,),
        in_specs=[
            pl.BlockSpec((tm, d), lambda i: (i, 0)),
            pl.BlockSpec((1, d), lambda i: (0, 0)),
            pl.BlockSpec((ROUTER_ROWS, d), lambda i: (0, 0)),
            pl.BlockSpec((ROUTER_ROWS, 1), lambda i: (0, 0)),
        ],
        out_specs=[
            pl.BlockSpec((SUBLANES, tm), lambda i: (0, i)),
            pl.BlockSpec((1, SUBLANES, LANES), lambda i: (i, 0, 0)),
            pl.BlockSpec((MOE_EXPERTS, tm), lambda i: (0, 0)),
        ],
        out_shape=[
            jax.ShapeDtypeStruct((SUBLANES, t), F32),
            jax.ShapeDtypeStruct((t // tm, SUBLANES, LANES), I32),
            jax.ShapeDtypeStruct((MOE_EXPERTS, tm), F32),
        ],
        scratch_shapes=[pltpu.VMEM((MOE_EXPERTS, tm), F32)],
        compiler_params=_params("arbitrary"),
        name="moe_router",
    )(x, norm_w, wr_t, br)


U32 = jnp.uint32
HIGH_HALF = 0xFFFF0000


def _pack_halves(a):
    c = a.shape[1] // 2
    left = lax.bitcast_convert_type(a[:, :c], U32)
    right = lax.bitcast_convert_type(a[:, c:], U32)
    return (left & U32(HIGH_HALF)) | (right >> 16)


def _unpack_halves(p):
    left = lax.bitcast_convert_type(p & U32(HIGH_HALF), F32)
    right = lax.bitcast_convert_type(p << 16, F32)
    return left.astype(BF16), right.astype(BF16)


def _for_each_run_piece(tinfo_ref, pstart_ref, fn):
    full = RUN_SIZES[0]
    aligned = lambda v: pl.multiple_of(v, RUN_ALIGN)

    def per_expert(e, c):
        cnt = tinfo_ref[0, 0, e]
        sorted_row = pstart_ref[e] + tinfo_ref[0, 1, e]
        buf_row = tinfo_ref[0, 2, e]

        def whole(k, c2):
            fn(aligned(buf_row + k * full), aligned(sorted_row + k * full), full, 0)
            return c2

        lax.fori_loop(0, cnt // full, whole, 0)
        for cls, size in enumerate(RUN_SIZES[1:], 1):
            done = cnt & ~(2 * size - 1)

            @pl.when((cnt & size) != 0)
            def _():
                fn(aligned(buf_row + done), aligned(sorted_row + done), size, cls)
        return c

    lax.fori_loop(0, MOE_EXPERTS, per_expert, 0)


def _zero_unused_rows(padstart_ref, padgap_ref, nused_ref, rows_ref, zero_ref, sem):
    zero_ref[...] = jnp.zeros_like(zero_ref)
    block = PAD_FILL_SIZES[0]
    n_blocks = rows_ref.shape[0] // block

    def piece(row, n, cls):
        return pltpu.make_async_copy(zero_ref.at[pl.ds(0, n)], rows_ref.at[pl.ds(row, n)], sem.at[cls])

    def for_each_piece(fn):
        def per_expert(e, c):
            gap = padgap_ref[e]
            for cls, size in enumerate(PAD_FILL_SIZES[1:], 1):
                done = gap & ~(2 * size - 1)

                @pl.when((gap & size) != 0)
                def _():
                    fn(pl.multiple_of(padstart_ref[e] + done, RUN_ALIGN), size, cls)
            return c

        lax.fori_loop(0, MOE_EXPERTS, per_expert, 0)

        def unused_block(b, c):
            fn(pl.multiple_of(b * block, block), block, 0)
            return c

        lax.fori_loop(nused_ref[0], n_blocks, unused_block, 0)

    for_each_piece(lambda *a: piece(*a).start())
    for_each_piece(lambda *a: piece(*a).wait())


def _dispatch_kernel(pstart_ref, padstart_ref, padgap_ref, nused_ref, x_ref, nw_ref, slot_ref, tinfo_ref,
                     tprev_ref, rows_ref, buf_ref, zero_ref, sem, zero_sem):
    tm = x_ref.shape[0]
    nbuf = buf_ref.shape[1]
    i = pl.program_id(0)
    cur = i % 2

    @pl.when(i == 0)
    def _():
        _zero_unused_rows(padstart_ref, padgap_ref, nused_ref, rows_ref, zero_ref, zero_sem)

    hn = _rmsnorm_rows(x_ref[...], nw_ref[...]).astype(BF16)
    r = lax.broadcasted_iota(I32, (nbuf, tm), 0)
    slots = slot_ref[...].astype(I32)
    place = ((r == slots[2:3, :]) | (r == slots[3:4, :])).astype(BF16)
    buf_ref[cur] = _pack_halves(_dot(place, hn))

    def piece_from(which):
        def piece(buf_row, sorted_row, n, cls):
            return pltpu.make_async_copy(buf_ref.at[which, pl.ds(buf_row, n)], rows_ref.at[pl.ds(sorted_row, n)],
                                         sem.at[which, cls])
        return piece

    @pl.when(i > 0)
    def _():
        _for_each_run_piece(tprev_ref, pstart_ref, lambda *a: piece_from(1 - cur)(*a).wait())

    _for_each_run_piece(tinfo_ref, pstart_ref, lambda *a: piece_from(cur)(*a).start())

    @pl.when(i == pl.num_programs(0) - 1)
    def _():
        _for_each_run_piece(tinfo_ref, pstart_ref, lambda *a: piece_from(cur)(*a).wait())


def _dispatch(pstart, pad_start, pad_gap, n_used, x, norm_w, gates, tinfo, cap):
    t, d = x.shape
    tm = MOE_TILE
    spec = lambda shape, index: pl.BlockSpec(shape, lambda i, *prefetch: index(i))
    grid_spec = pltpu.PrefetchScalarGridSpec(
        num_scalar_prefetch=4,
        grid=(t // tm,),
        in_specs=[
            spec((tm, d), lambda i: (i, 0)),
            spec((1, d), lambda i: (0, 0)),
            spec((SUBLANES, tm), lambda i: (0, i)),
            pl.BlockSpec((1, SUBLANES, LANES), lambda i, *prefetch: (i, 0, 0), memory_space=pltpu.SMEM),
            pl.BlockSpec((1, SUBLANES, LANES), lambda i, *prefetch: (jnp.maximum(i - 1, 0), 0, 0),
                         memory_space=pltpu.SMEM),
        ],
        out_specs=pl.BlockSpec(memory_space=pl.ANY),
        scratch_shapes=[
            pltpu.VMEM((2, MOE_TILE_BUF, d // 2), U32),
            pltpu.VMEM((PAD_FILL_SIZES[0], d // 2), U32),
            pltpu.SemaphoreType.DMA((2, len(RUN_SIZES))),
            pltpu.SemaphoreType.DMA((len(PAD_FILL_SIZES),)),
        ],
    )
    return pl.pallas_call(
        _dispatch_kernel,
        grid_spec=grid_spec,
        out_shape=jax.ShapeDtypeStruct((cap, d // 2), U32),
        compiler_params=_params("arbitrary"),
        name="moe_dispatch",
    )(pstart, pad_start, pad_gap, n_used, x, norm_w, gates, tinfo, tinfo)


def _expert_kernel(be_ref, nused_ref, rows_ref, wu_ref, wd_ref, y_ref, wu_bf_ref, wd_bf_ref):
    ff = wd_ref.shape[0]
    i = pl.program_id(0)
    used = i < nused_ref[0]
    new_expert = jnp.logical_or(i == 0, be_ref[i] != be_ref[jnp.maximum(i - 1, 0)])

    @pl.when(jnp.logical_and(used, new_expert))
    def _():
        wu_bf_ref[...] = wu_ref[...].astype(BF16)
        wd_bf_ref[...] = wd_ref[...].astype(BF16)

    @pl.when(used)
    def _():
        x_left, x_right = _unpack_halves(rows_ref[...])
        half = x_left.shape[1]
        gu = _dot(x_left, wu_bf_ref[0:half, :]) + _dot(x_right, wu_bf_ref[half:2 * half, :])
        act = _silu(gu[:, 0:ff]) * gu[:, ff:2 * ff]
        y = _dot(act.astype(BF16), wd_bf_ref[...])
        y_ref[...] = _pack_halves(y.astype(BF16).astype(F32))

    @pl.when(jnp.logical_not(used))
    def _():
        y_ref[...] = jnp.zeros_like(y_ref)


def _experts(block_expert, n_used, rows, w_up, w_down, layer):
    cap, dp = rows.shape
    d = 2 * dp
    rb = MOE_ROW_BLOCK
    ff = w_down.shape[2]
    grid_spec = pltpu.PrefetchScalarGridSpec(
        num_scalar_prefetch=2,
        grid=(cap // rb,),
        in_specs=[
            pl.BlockSpec((rb, dp), lambda i, be, nu: (jnp.minimum(i, nu[0] - 1), 0)),
            pl.BlockSpec((None, None, d, 2 * ff), lambda i, be, nu: (layer, be[i], 0, 0)),
            pl.BlockSpec((None, None, ff, d), lambda i, be, nu: (layer, be[i], 0, 0)),
        ],
        out_specs=pl.BlockSpec((rb, dp), lambda i, be, nu: (i, 0)),
        scratch_shapes=[pltpu.VMEM((d, 2 * ff), BF16), pltpu.VMEM((ff, d), BF16)],
    )
    return pl.pallas_call(
        _expert_kernel,
        grid_spec=grid_spec,
        out_shape=jax.ShapeDtypeStruct((cap, dp), U32),
        compiler_params=_params("arbitrary"),
        name="moe_experts",
    )(block_expert, n_used, rows, w_up, w_down)


def _combine_kernel(pstart_ref, x_ref, gate_ref, tinfo_ref, tnext_ref, fw_ref, y_ref, o_ref, buf_ref, sem,
                    *, final_norm):
    tm = x_ref.shape[0]
    nbuf = buf_ref.shape[1]
    i = pl.program_id(0)
    cur = i % 2

    def piece_into(which):
        def piece(buf_row, sorted_row, n, cls):
            return pltpu.make_async_copy(y_ref.at[pl.ds(sorted_row, n)], buf_ref.at[which, pl.ds(buf_row, n)],
                                         sem.at[which, cls])
        return piece

    @pl.when(i == 0)
    def _():
        buf_ref[...] = jnp.zeros_like(buf_ref)
        _for_each_run_piece(tinfo_ref, pstart_ref, lambda *a: piece_into(cur)(*a).start())

    @pl.when(i + 1 < pl.num_programs(0))
    def _():
        _for_each_run_piece(tnext_ref, pstart_ref, lambda *a: piece_into(1 - cur)(*a).start())

    g = gate_ref[...]
    cols = []
    for c in range(tm // LANES):
        sq = jnp.concatenate([g[:, c * LANES:(c + 1) * LANES], jnp.zeros((LANES - SUBLANES, LANES), F32)], axis=0)
        cols.append(sq.T)
    cols = jnp.concatenate(cols, axis=0)
    r = lax.broadcasted_iota(I32, (tm, nbuf), 1)
    weights = (jnp.where(r == cols[:, 2:3].astype(I32), cols[:, 0:1], 0.0)
               + jnp.where(r == cols[:, 3:4].astype(I32), cols[:, 1:2], 0.0))
    w_hi = weights.astype(BF16)
    w_lo = (weights - w_hi.astype(F32)).astype(BF16)

    _for_each_run_piece(tinfo_ref, pstart_ref, lambda *a: piece_into(cur)(*a).wait())

    y_left, y_right = _unpack_halves(buf_ref[cur])
    moe = jnp.concatenate([_dot(w_hi, y_left) + _dot(w_lo, y_left), _dot(w_hi, y_right) + _dot(w_lo, y_right)],
                          axis=1)
    out = x_ref[...] + moe
    if final_norm:
        out = _rmsnorm_rows(out, fw_ref[...])
    o_ref[...] = out


def _combine(pstart, x, gates, tinfo, y_rows, final_w, final_norm):
    t, d = x.shape
    tm = MOE_TILE
    last = t // tm - 1
    grid_spec = pltpu.PrefetchScalarGridSpec(
        num_scalar_prefetch=1,
        grid=(t // tm,),
        in_specs=[
            pl.BlockSpec((tm, d), lambda i, ps: (i, 0)),
            pl.BlockSpec((SUBLANES, tm), lambda i, ps: (0, i)),
            pl.BlockSpec((1, SUBLANES, LANES), lambda i, ps: (i, 0, 0), memory_space=pltpu.SMEM),
            pl.BlockSpec((1, SUBLANES, LANES), lambda i, ps: (jnp.minimum(i + 1, last), 0, 0),
                         memory_space=pltpu.SMEM),
            pl.BlockSpec((1, d), lambda i, ps: (0, 0)),
            pl.BlockSpec(memory_space=pl.ANY),
        ],
        out_specs=pl.BlockSpec((tm, d), lambda i, ps: (i, 0)),
        scratch_shapes=[pltpu.VMEM((2, MOE_TILE_BUF, d // 2), U32), pltpu.SemaphoreType.DMA((2, len(RUN_SIZES)))],
    )
    return pl.pallas_call(
        functools.partial(_combine_kernel, final_norm=final_norm),
        grid_spec=grid_spec,
        out_shape=jax.ShapeDtypeStruct((t, d), F32),
        compiler_params=_params("arbitrary"),
        name="moe_combine",
    )(pstart, x, gates, tinfo, tinfo, final_w, y_rows)


def _hier_moe(x, norm_w, w_group, b_group, w_router, b_router, w_up, w_down, layer, final_w, final_norm):
    t, d = x.shape
    g, _, epg = w_router.shape
    pad_rows = SUBLANES - g
    wr_t = jnp.concatenate([w_group.T, jnp.zeros((pad_rows, d), F32),
                            w_router.transpose(0, 2, 1).reshape(g * epg, d)], axis=0)
    br = jnp.concatenate([b_group, jnp.full((pad_rows,), -1e30, F32), b_router.reshape(-1)])[:, None]
    gates, tinfo, cnt = _router(x, norm_w, wr_t, br)

    rb = MOE_ROW_BLOCK
    n_exp = g * epg
    counts = cnt[:, 0].astype(I32)
    padded = ((counts + rb - 1) // rb) * rb
    pends = jnp.cumsum(padded)
    pstart = (pends - padded).astype(I32)
    max_rows = 2 * t + (t // MOE_TILE) * n_exp * (RUN_ALIGN - 1) + n_exp * (rb - 1)
    n_blocks = -(-max_rows // rb)
    block_start = jnp.arange(n_blocks, dtype=I32) * rb
    block_expert = jnp.minimum(jnp.sum(block_start[:, None] >= pends[None, :], axis=1), n_exp - 1).astype(I32)
    n_used = (pends[-1:] // rb).astype(I32)

    rows = _dispatch(pstart, (pstart + counts).astype(I32), (padded - counts).astype(I32), n_used,
                     x, norm_w, gates, tinfo, n_blocks * rb)
    y_rows = _experts(block_expert, n_used, rows, w_up, w_down, layer)
    return _combine(pstart, x, gates, tinfo, y_rows, final_w, final_norm)


def _ssd_in_kernel(x_ref, nw_ref, wz_ref, wx_ref, wdt_ref, z_ref, xbc_ref, dt_ref):
    h = _rmsnorm_rows(x_ref[...], nw_ref[...]).astype(BF16)
    z_ref[...] = _dot(h, wz_ref[...])
    xbc_ref[...] = _dot(h, wx_ref[...])
    dt_ref[...] = _dot(h, wdt_ref[...])


def _ssd_in_proj(x, norm_w, w_z, w_xbc, w_dt, tm=256):
    t, d = x.shape
    dz, dx, dh = w_z.shape[1], w_xbc.shape[1], w_dt.shape[1]
    resident = lambda n: pl.BlockSpec((d, n), lambda i: (0, 0), pipeline_mode=pl.Buffered(1))
    return pl.pallas_call(
        _ssd_in_kernel,
        grid=(t // tm,),
        in_specs=[
            pl.BlockSpec((tm, d), lambda i: (i, 0)),
            pl.BlockSpec((1, d), lambda i: (0, 0)),
            resident(dz), resident(dx), resident(dh),
        ],
        out_specs=[
            pl.BlockSpec((tm, dz), lambda i: (i, 0)),
            pl.BlockSpec((tm, dx), lambda i: (i, 0)),
            pl.BlockSpec((tm, dh), lambda i: (i, 0)),
        ],
        out_shape=[
            jax.ShapeDtypeStruct((t, dz), F32),
            jax.ShapeDtypeStruct((t, dx), F32),
            jax.ShapeDtypeStruct((t, dh), F32),
        ],
        compiler_params=_params("parallel"),
        name="ssd_in_proj",
    )(x, norm_w, w_z, w_xbc, w_dt)


def _ssd_kernel(xbc_ref, z_ref, dt_ref, cw_ref, cb_ref, dtb_ref, alog_ref, dskip_ref, nw_ref,
                y_ref, ext_ref, state_ref, *, d_inner, n_heads):
    L = SSD_CHUNK
    G = SSD_GROUPS
    N = SSD_STATE
    P = SSD_HEAD_DIM
    R = n_heads // G
    GW = R * P
    halo = SUBLANES

    @pl.when(pl.program_id(1) == 0)
    def _():
        ext_ref[0:halo, :] = jnp.zeros((halo, ext_ref.shape[1]), F32)
        state_ref[...] = jnp.zeros_like(state_ref)

    ext_ref[halo:halo + L, :] = xbc_ref[...]

    def conv_silu(c0, width):
        acc = jnp.broadcast_to(cb_ref[:, c0:c0 + width], (L, width))
        for k in range(SSD_CONV):
            back = SSD_CONV - 1 - k
            acc = acc + ext_ref[halo - back:halo - back + L, c0:c0 + width] * cw_ref[k:k + 1, c0:c0 + width]
        return _silu(acc)

    dt = _softplus(dt_ref[...] + dtb_ref[...])
    a_neg = -jnp.exp(alog_ref[...])
    da = dt * a_neg
    rr = lax.broadcasted_iota(I32, (L, L), 0)
    cc = lax.broadcasted_iota(I32, (L, L), 1)
    lower = rr >= cc
    cs = _dot_f32_rhs(lower.astype(BF16), da)
    cs_last = cs[L - 1:L, :]
    cs_sq = jnp.concatenate([cs, jnp.zeros((L, L - n_heads), F32)], axis=1).T
    decay_to_end = jnp.exp(cs_last - cs)
    decay_from_start = jnp.exp(cs)
    chunk_decay = jnp.exp(cs_last)

    head_of_lane = lax.broadcasted_iota(I32, (n_heads, d_inner), 1) // P
    head_row = lax.broadcasted_iota(I32, (n_heads, d_inner), 0)
    spread = (head_row == head_of_lane).astype(BF16)
    rows8 = lambda v: jnp.broadcast_to(v, (SUBLANES, n_heads))
    per_head = jnp.concatenate([dt, decay_to_end, decay_from_start, rows8(chunk_decay), rows8(dskip_ref[...])],
                               axis=0)
    wide = _dot_f32_lhs(per_head, spread)
    lane_head = lax.broadcasted_iota(I32, (1, GW), 1) // P

    for g in range(G):
        cols = slice(g * GW, (g + 1) * GW)
        dt_g = wide[0:L, cols]
        to_end_g = wide[L:2 * L, cols]
        from_start_g = wide[2 * L:3 * L, cols]
        chunk_decay_g = wide[3 * L:3 * L + 1, cols]
        dskip_g = wide[3 * L + SUBLANES:3 * L + SUBLANES + 1, cols]

        x_g = conv_silu(g * GW, GW)
        b_g = conv_silu(d_inner + g * N, N)
        c_g = conv_silu(d_inner + G * N + g * N, N)
        xdt = x_g * dt_g
        b_bf = b_g.astype(BF16)
        c_bf = c_g.astype(BF16)

        cb = _dot(c_bf, b_bf, NT_DIMS)
        xdt_bf = xdt.astype(BF16)
        y_g = jnp.zeros((L, GW), F32)
        for r in range(R):
            hd = g * R + r
            seg = cs[:, hd:hd + 1] - cs_sq[hd:hd + 1, :]
            m = jnp.where(lower, cb * jnp.exp(seg), 0.0)
            x_r = jnp.where(lane_head == r, xdt_bf, jnp.zeros_like(xdt_bf))
            y_g = y_g + _dot(m.astype(BF16), x_r)
        prev = state_ref[g]
        y_g = y_g + from_start_g * _dot(c_bf, prev.astype(BF16))
        new_part = _dot(b_bf, (xdt * to_end_g).astype(BF16), TN_DIMS)
        state_ref[g] = prev * chunk_decay_g + new_part
        y_g = y_g + x_g * dskip_g

        yz = y_g * _silu(z_ref[:, g * GW:(g + 1) * GW])
        yn = yz * lax.rsqrt(jnp.mean(yz * yz, axis=-1, keepdims=True) + EPS)
        y_ref[:, g * GW:(g + 1) * GW] = (yn * nw_ref[:, g * GW:(g + 1) * GW]).astype(BF16)

    ext_ref[0:halo, :] = ext_ref[L:L + halo, :]


def _ssd_scan(xbc, z, dt, conv_w, conv_b, dt_bias, a_log, d_skip, norm_w, batch, seq):
    t, dx = xbc.shape
    d_inner = z.shape[1]
    n_heads = dt.shape[1]
    L = SSD_CHUNK
    nc = seq // L
    row = lambda n: pl.BlockSpec((1, n), lambda b, c: (0, 0))
    kern = functools.partial(_ssd_kernel, d_inner=d_inner, n_heads=n_heads)
    return pl.pallas_call(
        kern,
        grid=(batch, nc),
        in_specs=[
            pl.BlockSpec((L, dx), lambda b, c: (b * nc + c, 0)),
            pl.BlockSpec((L, d_inner), lambda b, c: (b * nc + c, 0)),
            pl.BlockSpec((L, n_heads), lambda b, c: (b * nc + c, 0)),
            pl.BlockSpec((SSD_CONV, dx), lambda b, c: (0, 0)),
            row(dx), row(n_heads), row(n_heads), row(n_heads), row(d_inner),
        ],
        out_specs=pl.BlockSpec((L, d_inner), lambda b, c: (b * nc + c, 0)),
        out_shape=jax.ShapeDtypeStruct((t, d_inner), BF16),
        scratch_shapes=[
            pltpu.VMEM((L + 2 * SUBLANES, dx), F32),
            pltpu.VMEM((SSD_GROUPS, SSD_STATE, d_inner // SSD_GROUPS), F32),
        ],
        compiler_params=_params("arbitrary", "arbitrary"),
        name="ssd_scan",
    )(xbc, z, dt, conv_w, conv_b, dt_bias, a_log, d_skip, norm_w)


def _rope_tables(seq):
    half = ATTN_HEAD_DIM // 2
    inv = ROPE_THETA ** (-jnp.arange(half, dtype=F32) / half)
    ang = jnp.arange(seq).astype(F32)[:, None] * inv[None, :]
    cos = jnp.cos(ang)
    sin = jnp.sin(ang)
    reps = LANES // ATTN_HEAD_DIM
    cos_t = jnp.tile(jnp.concatenate([cos, cos], axis=1), (1, reps))
    sin_t = jnp.tile(jnp.concatenate([-sin, sin], axis=1), (1, reps))
    return cos_t, sin_t


def kernel(x, mix_norm, ffn_norm, final_norm, attn_w_qkv, attn_w_o, ssd_w_in, ssd_conv_w, ssd_conv_b,
           ssd_dt_bias, ssd_a_log, ssd_d, ssd_norm, ssd_w_out, moe_w_group, moe_b_group, moe_w_router,
           moe_b_router, moe_w_up, moe_w_down):
    batch, seq, d = x.shape
    assert seq % MOBA_BLOCK == 0 and seq % SSD_CHUNK == 0 and d % LANES == 0
    depth = mix_norm.shape[0]
    t = batch * seq
    xt = x.reshape(t, d)
    cos_t, sin_t = _rope_tables(seq)
    final_w = final_norm[None, :]

    for i in range(depth):
        j = i // 2
        nw = mix_norm[i][None, :]
        if i % 2 == 0:
            w_qkv = attn_w_qkv[j].astype(BF16)
            q, k, vt, kmean = _qkv_rope(xt, nw, w_qkv[:, 0:2 * d], w_qkv[:, 2 * d:].T, cos_t, sin_t, seq)
            kmean = kmean.reshape(batch, seq // MOBA_BLOCK, d)
            o = _moba_attention(q, k, vt, kmean, batch, seq)
            xt = _proj_residual(o, attn_w_o[j].astype(BF16), xt)
        else:
            w_in = ssd_w_in[j].astype(BF16)
            d_inner = ssd_norm.shape[1]
            dx = ssd_conv_b.shape[1]
            z, xbc, dt = _ssd_in_proj(xt, nw, w_in[:, 0:d_inner], w_in[:, d_inner:d_inner + dx],
                                      w_in[:, d_inner + dx:])
            y = _ssd_scan(xbc, z, dt, ssd_conv_w[j][:, 0, :], ssd_conv_b[j][None, :], ssd_dt_bias[j][None, :],
                          ssd_a_log[j][None, :], ssd_d[j][None, :], ssd_norm[j][None, :], batch, seq)
            xt = _proj_residual(y, ssd_w_out[j].astype(BF16), xt)
        xt = _hier_moe(xt, ffn_norm[i][None, :], moe_w_group[i], moe_b_group[i], moe_w_router[i],
                       moe_b_router[i], moe_w_up, moe_w_down, i, final_w, final_norm=(i == depth - 1))
    return xt.reshape(batch, seq, d)
```

```python
import functools
import math

import jax
import jax.numpy as jnp
from jax import lax
from jax.experimental import pallas as pl
from jax.experimental.pallas import tpu as pltpu

F32 = jnp.float32
BF16 = jnp.bfloat16
I32 = jnp.int32

EPS = 1e-6
LANES = 128
SUBLANES = 8
VMEM_LIMIT = 56 * 1024 * 1024

ATTN_HEADS = 16
ATTN_HEAD_DIM = 64
MOBA_BLOCK = 256
MOBA_TOPK = 3
ROPE_THETA = 10000.0
HEADS_PER_TILE = LANES // ATTN_HEAD_DIM

SSD_HEAD_DIM = 64
SSD_GROUPS = 8
SSD_STATE = 128
SSD_CONV = 4
SSD_CHUNK = 128

MOE_GROUPS = 4
MOE_EXPERTS_PER_GROUP = 8
MOE_EXPERTS = MOE_GROUPS * MOE_EXPERTS_PER_GROUP
MOE_ROW_BLOCK = 512
ROUTER_ROWS = SUBLANES + MOE_EXPERTS
MOE_TILE = 256
RUN_ALIGN = SUBLANES
RUN_SIZES = (16, 8)
MOE_TILE_BUF = 2 * MOE_TILE + MOE_EXPERTS * RUN_ALIGN
PAD_FILL_SIZES = tuple(MOE_ROW_BLOCK >> s for s in range((MOE_ROW_BLOCK // RUN_ALIGN).bit_length()))

NT_DIMS = (((1,), (1,)), ((), ()))
NN_DIMS = (((1,), (0,)), ((), ()))
TN_DIMS = (((0,), (0,)), ((), ()))


def _params(*sem):
    return pltpu.CompilerParams(dimension_semantics=sem, vmem_limit_bytes=VMEM_LIMIT)


def _rmsnorm_rows(x, w):
    var = jnp.mean(x * x, axis=-1, keepdims=True)
    return (x * lax.rsqrt(var + EPS)) * w


def _split3(x):
    hi = x.astype(BF16)
    r = x - hi.astype(F32)
    mid = r.astype(BF16)
    lo = (r - mid.astype(F32)).astype(BF16)
    return hi, mid, lo


def _dot(a, b, dims=NN_DIMS):
    return lax.dot_general(a, b, dims, preferred_element_type=F32)


def _dot_f32_lhs(x, e, dims=NN_DIMS):
    hi, mid, lo = _split3(x)
    return _dot(hi, e, dims) + _dot(mid, e, dims) + _dot(lo, e, dims)


def _dot_f32_rhs(e, x, dims=NN_DIMS):
    hi, mid, lo = _split3(x)
    return _dot(e, hi, dims) + _dot(e, mid, dims) + _dot(e, lo, dims)


def _dot_f32_3pass(a, b, dims=NN_DIMS):
    ah, am, _ = _split3(a)
    bh, bm, _ = _split3(b)
    return _dot(ah, bh, dims) + _dot(ah, bm, dims) + _dot(am, bh, dims)


def _silu(x):
    return x * (0.5 * jnp.tanh(0.5 * x) + 0.5)


def _softplus(x):
    return jnp.maximum(x, 0.0) + jnp.log(1.0 + jnp.exp(-jnp.abs(x)))


def _qkv_kernel(x_ref, nw_ref, wqk_ref, wvt_ref, cos_ref, sin_ref, q_ref, k_ref, vt_ref, km_ref, *, d, scale):
    h = _rmsnorm_rows(x_ref[...], nw_ref[...]).astype(BF16)
    cos = cos_ref[...]
    sin = sin_ref[...]
    lane = lax.broadcasted_iota(I32, (1, LANES), 1)
    first_half = (lane % ATTN_HEAD_DIM) < (ATTN_HEAD_DIM // 2)

    def rope_tile(xc):
        lo_partner = pltpu.roll(xc, LANES - ATTN_HEAD_DIM // 2, 1)
        hi_partner = pltpu.roll(xc, ATTN_HEAD_DIM // 2, 1)
        return xc * cos + jnp.where(first_half, lo_partner, hi_partner) * sin

    q = _dot(h, wqk_ref[:, 0:d])
    k = _dot(h, wqk_ref[:, d:2 * d])
    for c in range(d // LANES):
        sl = slice(c * LANES, (c + 1) * LANES)
        q_ref[:, sl] = (rope_tile(q[:, sl]) * scale).astype(BF16)
        kc = rope_tile(k[:, sl])
        k_ref[:, sl] = kc.astype(BF16)
        km_ref[0, :, sl] = jnp.mean(kc, axis=0, keepdims=True)
    vt_ref[0] = _dot(wvt_ref[...], h, NT_DIMS).astype(BF16)


def _qkv_rope(x, norm_w, w_qk, w_vt, cos_t, sin_t, seq):
    t, d = x.shape
    tm = MOBA_BLOCK
    n_tiles = t // tm
    seq_tiles = seq // tm
    kern = functools.partial(_qkv_kernel, d=d, scale=math.log2(math.e) / math.sqrt(ATTN_HEAD_DIM))
    return pl.pallas_call(
        kern,
        grid=(n_tiles,),
        in_specs=[
            pl.BlockSpec((tm, d), lambda i: (i, 0)),
            pl.BlockSpec((1, d), lambda i: (0, 0)),
            pl.BlockSpec((d, 2 * d), lambda i: (0, 0), pipeline_mode=pl.Buffered(1)),
            pl.BlockSpec((d, d), lambda i: (0, 0), pipeline_mode=pl.Buffered(1)),
            pl.BlockSpec((tm, LANES), lambda i: (i % seq_tiles, 0)),
            pl.BlockSpec((tm, LANES), lambda i: (i % seq_tiles, 0)),
        ],
        out_specs=[
            pl.BlockSpec((tm, d), lambda i: (i, 0)),
            pl.BlockSpec((tm, d), lambda i: (i, 0)),
            pl.BlockSpec((1, d, tm), lambda i: (i, 0, 0)),
            pl.BlockSpec((1, 1, d), lambda i: (i, 0, 0)),
        ],
        out_shape=[
            jax.ShapeDtypeStruct((t, d), BF16),
            jax.ShapeDtypeStruct((t, d), BF16),
            jax.ShapeDtypeStruct((n_tiles, d, tm), BF16),
            jax.ShapeDtypeStruct((n_tiles, 1, d), F32),
        ],
        compiler_params=_params("parallel"),
        name="qkv_rope",
    )(x, norm_w, w_qk, w_vt, cos_t, sin_t)


def _attn_kernel(q_ref, k_ref, vt_ref, km_ref, o_ref, sa_ref, sb_ref, p_ref, *, nb):
    qi = pl.program_id(2)
    tq = q_ref.shape[0]
    q2 = q_ref[...]
    km = km_ref[...]
    lane = lax.broadcasted_iota(I32, (1, LANES), 1)
    key = lax.broadcasted_iota(I32, (MOBA_BLOCK, tq), 0)
    qry = lax.broadcasted_iota(I32, (MOBA_BLOCK, tq), 1)
    causal = key <= qry
    blk = lax.broadcasted_iota(I32, (nb, tq), 0)
    neg_inf = jnp.float32(-jnp.inf)
    heads = range(HEADS_PER_TILE)

    qh = [jnp.where((lane // ATTN_HEAD_DIM) == h, q2, jnp.zeros_like(q2)) for h in heads]

    bits = []
    for h in heads:
        gate = _dot_f32_lhs(km, qh[h], NT_DIMS)
        alive = (blk < qi).astype(I32)
        sel = jnp.zeros((nb, tq), I32)
        for _ in range(min(MOBA_TOPK, nb)):
            gm = jnp.where(alive > 0, gate, neg_inf)
            top = jnp.max(gm, axis=0, keepdims=True)
            cand = jnp.where((alive > 0) & (gm == top), blk, nb)
            idx = jnp.min(cand, axis=0, keepdims=True)
            pick = (blk == idx).astype(I32)
            sel = sel | pick
            alive = alive & (1 - pick)
        bits.append(jnp.sum(sel << blk, axis=0, keepdims=True))

    def block_of(k):
        return jnp.where(k == 0, qi, jnp.minimum(k - 1, qi))

    def issue_scores(k, s_out):
        start = pl.multiple_of(block_of(k) * MOBA_BLOCK, MOBA_BLOCK)
        kb = k_ref[pl.ds(start, MOBA_BLOCK), :]
        for h in heads:
            s_out[h] = _dot(kb, qh[h], NT_DIMS)

    def issue_values(k):
        blk_idx = block_of(k)
        return [_dot(vt_ref[blk_idx, h * ATTN_HEAD_DIM:(h + 1) * ATTN_HEAD_DIM, :], p_ref[h]) for h in heads]

    def softmax(k, s_in, state, diag):
        out = []
        for h in heads:
            m_i, l_i = state[h]
            if diag:
                s = jnp.where(causal, s_in[h], neg_inf)
                m_new = jnp.maximum(m_i, jnp.max(s, axis=0, keepdims=True))
                shift = m_new
            else:
                s = s_in[h]
                keep = ((bits[h] >> (k - 1)) & 1) == 1
                m_new = jnp.maximum(m_i, jnp.where(keep, jnp.max(s, axis=0, keepdims=True), neg_inf))
                shift = jnp.where(keep, m_new, -neg_inf)
            alpha = jnp.exp2(m_i - m_new)
            p = jnp.exp2(s - shift)
            l_new = alpha * l_i + jnp.sum(p, axis=0, keepdims=True)
            p_ref[h] = p.astype(BF16)
            out.append((m_new, l_new, alpha))
        return out

    def phase(k, carry, s_cur, s_nxt):
        issue_scores(k + 1, s_nxt)
        pv = issue_values(k - 1)
        stats = softmax(k, s_cur, [(c[0], c[1]) for c in carry], False)
        return tuple((stats[h][0], stats[h][1], carry[h][3] * carry[h][2] + pv[h], stats[h][2]) for h in heads)

    issue_scores(0, sa_ref)
    issue_scores(1, sb_ref)
    first = softmax(0, sa_ref, [(jnp.full((1, tq), neg_inf, F32), jnp.zeros((1, tq), F32)) for _ in heads], True)
    carry = tuple((m, l, jnp.zeros((ATTN_HEAD_DIM, tq), F32), a) for (m, l, a) in first)

    def two_phases(u, c):
        c = phase(2 * u + 1, c, sb_ref, sa_ref)
        return phase(2 * u + 2, c, sa_ref, sb_ref)

    n_pairs = (qi + 1) // 2
    carry = lax.fori_loop(0, n_pairs, two_phases, carry)
    pv = issue_values(2 * n_pairs)
    o_t = jnp.concatenate([(alpha * acc + pv[h]) / l_i for h, (_, l_i, acc, alpha) in enumerate(carry)],
                          axis=0)
    o_ref[...] = o_t.T.astype(BF16)


def _moba_attention(q, k, vt, kmean, batch, seq):
    t, d = q.shape
    nb = seq // MOBA_BLOCK
    tq = MOBA_BLOCK
    nq = seq // tq
    col_tiles = d // LANES
    kern = functools.partial(_attn_kernel, nb=nb)
    return pl.pallas_call(
        kern,
        grid=(batch, col_tiles, nq),
        in_specs=[
            pl.BlockSpec((tq, LANES), lambda b, c, i: (b * nq + i, c)),
            pl.BlockSpec((seq, LANES), lambda b, c, i: (b, c)),
            pl.BlockSpec((nb, LANES, MOBA_BLOCK), lambda b, c, i: (b, c, 0)),
            pl.BlockSpec((None, nb, LANES), lambda b, c, i: (b, 0, c)),
        ],
        out_specs=pl.BlockSpec((tq, LANES), lambda b, c, i: (b * nq + i, c)),
        out_shape=jax.ShapeDtypeStruct((t, d), BF16),
        scratch_shapes=[
            pltpu.VMEM((HEADS_PER_TILE, MOBA_BLOCK, tq), F32),
            pltpu.VMEM((HEADS_PER_TILE, MOBA_BLOCK, tq), F32),
            pltpu.VMEM((HEADS_PER_TILE, MOBA_BLOCK, tq), BF16),
        ],
        compiler_params=_params("parallel", "parallel", "arbitrary"),
        name="moba_attention",
    )(q, k, vt, kmean)


def _proj_res_kernel(a_ref, w_ref, res_ref, o_ref):
    o_ref[...] = res_ref[...] + _dot(a_ref[...], w_ref[...])


def _proj_residual(a, w, res, tm=512):
    t, kdim = a.shape
    d = w.shape[1]
    return pl.pallas_call(
        _proj_res_kernel,
        grid=(t // tm,),
        in_specs=[
            pl.BlockSpec((tm, kdim), lambda i: (i, 0)),
            pl.BlockSpec((kdim, d), lambda i: (0, 0), pipeline_mode=pl.Buffered(1)),
            pl.BlockSpec((tm, d), lambda i: (i, 0)),
        ],
        out_specs=pl.BlockSpec((tm, d), lambda i: (i, 0)),
        out_shape=jax.ShapeDtypeStruct((t, d), F32),
        compiler_params=_params("parallel"),
        name="proj_residual",
    )(a, w, res)


def _router_kernel(x_ref, nw_ref, wr_ref, br_ref, gate_ref, tinfo_ref, cnt_ref, carry_ref):
    tm = x_ref.shape[0]

    @pl.when(pl.program_id(0) == 0)
    def _():
        carry_ref[...] = jnp.zeros_like(carry_ref)

    hn = _rmsnorm_rows(x_ref[...], nw_ref[...])
    logits = _dot_f32_3pass(wr_ref[...], hn, NT_DIMS) + br_ref[...]
    sub = lax.broadcasted_iota(I32, (SUBLANES, tm), 0)
    neg_inf = jnp.float32(-jnp.inf)

    gl = logits[0:SUBLANES]
    gmax = jnp.max(gl, axis=0, keepdims=True)
    gidx = jnp.min(jnp.where(gl == gmax, sub, SUBLANES), axis=0, keepdims=True)
    gprob = 1.0 / jnp.sum(jnp.exp(gl - gmax), axis=0, keepdims=True)

    el = jnp.zeros((SUBLANES, tm), F32)
    for g in range(MOE_GROUPS):
        el = jnp.where(gidx == g, logits[SUBLANES * (g + 1):SUBLANES * (g + 2)], el)
    m1 = jnp.max(el, axis=0, keepdims=True)
    i1 = jnp.min(jnp.where(el == m1, sub, SUBLANES), axis=0, keepdims=True)
    el2 = jnp.where(sub == i1, neg_inf, el)
    m2 = jnp.max(el2, axis=0, keepdims=True)
    i2 = jnp.min(jnp.where(el2 == m2, sub, SUBLANES), axis=0, keepdims=True)
    ratio = jnp.exp(m2 - m1)
    den = 1.0 + ratio
    gate_ref[...] = jnp.zeros_like(gate_ref)
    gate_ref[0:1, :] = gprob / den
    gate_ref[1:2, :] = gprob * ratio / den

    e1 = gidx * MOE_EXPERTS_PER_GROUP + i1
    e2 = gidx * MOE_EXPERTS_PER_GROUP + i2

    eiota = lax.broadcasted_iota(I32, (MOE_EXPERTS, tm), 0)
    oh1 = eiota == e1
    oh2 = eiota == e2
    onehot = (oh1 | oh2).astype(BF16)
    src = lax.broadcasted_iota(I32, (tm, tm), 0)
    dst = lax.broadcasted_iota(I32, (tm, tm), 1)
    before = (src < dst).astype(BF16)
    prefix = _dot(onehot, before)
    count = _dot(onehot, jnp.ones((tm, tm), BF16))
    total = jnp.floor((count + (RUN_ALIGN - 1)) * (1.0 / RUN_ALIGN)) * RUN_ALIGN
    erow = lax.broadcasted_iota(I32, (MOE_EXPERTS, MOE_EXPERTS), 0)
    ecol = lax.broadcasted_iota(I32, (MOE_EXPERTS, MOE_EXPERTS), 1)
    run_start = _dot_f32_rhs((ecol < erow).astype(BF16), total)
    slot = run_start + prefix
    gate_ref[2:3, :] = jnp.sum(jnp.where(oh1, slot, 0.0), axis=0, keepdims=True)
    gate_ref[3:4, :] = jnp.sum(jnp.where(oh2, slot, 0.0), axis=0, keepdims=True)

    lane = lax.broadcasted_iota(I32, (MOE_EXPERTS, tm), 1)
    on_lanes = lambda v: jnp.sum(jnp.where(eiota == lane, v, 0.0), axis=0, keepdims=True)[:, 0:LANES].astype(I32)
    tinfo_ref[...] = jnp.zeros_like(tinfo_ref)
    tinfo_ref[0, 0:1, :] = on_lanes(total)
    tinfo_ref[0, 1:2, :] = on_lanes(carry_ref[...])
    tinfo_ref[0, 2:3, :] = on_lanes(run_start)
    carry_ref[...] = carry_ref[...] + total
    cnt_ref[...] = carry_ref[...]


def _router(x, norm_w, wr_t, br):
    t, d = x.shape
    tm = MOE_TILE
    return pl.pallas_call(
        _router_kernel,
        grid=(t // tm,),
        in_specs=[
            pl.BlockSpec((tm, d), lambda i: (i, 0)),
            pl.BlockSpec((1, d), lambda i: (0, 0)),
            pl.BlockSpec((ROUTER_ROWS, d), lambda i: (0, 0)),
            pl.BlockSpec((ROUTER_ROWS, 1), lambda i: (0, 0)),
        ],
        out_specs=[
            pl.BlockSpec((SUBLANES, tm), lambda i: (0, i)),
            pl.BlockSpec((1, SUBLANES, LANES), lambda i: (i, 0, 0)),
            pl.BlockSpec((MOE_EXPERTS, tm), lambda i: (0, 0)),
        ],
        out_shape=[
            jax.ShapeDtypeStruct((SUBLANES, t), F32),
            jax.ShapeDtypeStruct((t // tm, SUBLANES, LANES), I32),
            jax.ShapeDtypeStruct((MOE_EXPERTS, tm), F32),
        ],
        scratch_shapes=[pltpu.VMEM((MOE_EXPERTS, tm), F32)],
        compiler_params=_params("arbitrary"),
        name="moe_router",
    )(x, norm_w, wr_t, br)


U32 = jnp.uint32
HIGH_HALF = 0xFFFF0000


def _pack_halves(a):
    c = a.shape[1] // 2
    left = lax.bitcast_convert_type(a[:, :c], U32)
    right = lax.bitcast_convert_type(a[:, c:], U32)
    return (left & U32(HIGH_HALF)) | (right >> 16)


def _unpack_halves(p):
    left = lax.bitcast_convert_type(p & U32(HIGH_HALF), F32)
    right = lax.bitcast_convert_type(p << 16, F32)
    return left.astype(BF16), right.astype(BF16)


def _for_each_run_piece(tinfo_ref, pstart_ref, fn):
    full = RUN_SIZES[0]
    aligned = lambda v: pl.multiple_of(v, RUN_ALIGN)

    def per_expert(e, c):
        cnt = tinfo_ref[0, 0, e]
        sorted_row = pstart_ref[e] + tinfo_ref[0, 1, e]
        buf_row = tinfo_ref[0, 2, e]

        def whole(k, c2):
            fn(aligned(buf_row + k * full), aligned(sorted_row + k * full), full, 0)
            return c2

        lax.fori_loop(0, cnt // full, whole, 0)
        for cls, size in enumerate(RUN_SIZES[1:], 1):
            done = cnt & ~(2 * size - 1)

            @pl.when((cnt & size) != 0)
            def _():
                fn(aligned(buf_row + done), aligned(sorted_row + done), size, cls)
        return c

    lax.fori_loop(0, MOE_EXPERTS, per_expert, 0)


def _zero_unused_rows(padstart_ref, padgap_ref, nused_ref, rows_ref, zero_ref, sem):
    zero_ref[...] = jnp.zeros_like(zero_ref)
    block = PAD_FILL_SIZES[0]
    n_blocks = rows_ref.shape[0] // block

    def piece(row, n, cls):
        return pltpu.make_async_copy(zero_ref.at[pl.ds(0, n)], rows_ref.at[pl.ds(row, n)], sem.at[cls])

    def for_each_piece(fn):
        def per_expert(e, c):
            gap = padgap_ref[e]
            for cls, size in enumerate(PAD_FILL_SIZES[1:], 1):
                done = gap & ~(2 * size - 1)

                @pl.when((gap & size) != 0)
                def _():
                    fn(pl.multiple_of(padstart_ref[e] + done, RUN_ALIGN), size, cls)
            return c

        lax.fori_loop(0, MOE_EXPERTS, per_expert, 0)

        def unused_block(b, c):
            fn(pl.multiple_of(b * block, block), block, 0)
            return c

        lax.fori_loop(nused_ref[0], n_blocks, unused_block, 0)

    for_each_piece(lambda *a: piece(*a).start())
    for_each_piece(lambda *a: piece(*a).wait())


def _dispatch_kernel(pstart_ref, padstart_ref, padgap_ref, nused_ref, x_ref, nw_ref, slot_ref, tinfo_ref,
                     tprev_ref, rows_ref, buf_ref, zero_ref, sem, zero_sem):
    tm = x_ref.shape[0]
    nbuf = buf_ref.shape[1]
    i = pl.program_id(0)
    cur = i % 2

    @pl.when(i == 0)
    def _():
        _zero_unused_rows(padstart_ref, padgap_ref, nused_ref, rows_ref, zero_ref, zero_sem)

    hn = _rmsnorm_rows(x_ref[...], nw_ref[...]).astype(BF16)
    r = lax.broadcasted_iota(I32, (nbuf, tm), 0)
    slots = slot_ref[...].astype(I32)
    place = ((r == slots[2:3, :]) | (r == slots[3:4, :])).astype(BF16)
    buf_ref[cur] = _pack_halves(_dot(place, hn))

    def piece_from(which):
        def piece(buf_row, sorted_row, n, cls):
            return pltpu.make_async_copy(buf_ref.at[which, pl.ds(buf_row, n)], rows_ref.at[pl.ds(sorted_row, n)],
                                         sem.at[which, cls])
        return piece

    @pl.when(i > 0)
    def _():
        _for_each_run_piece(tprev_ref, pstart_ref, lambda *a: piece_from(1 - cur)(*a).wait())

    _for_each_run_piece(tinfo_ref, pstart_ref, lambda *a: piece_from(cur)(*a).start())

    @pl.when(i == pl.num_programs(0) - 1)
    def _():
        _for_each_run_piece(tinfo_ref, pstart_ref, lambda *a: piece_from(cur)(*a).wait())


def _dispatch(pstart, pad_start, pad_gap, n_used, x, norm_w, gates, tinfo, cap):
    t, d = x.shape
    tm = MOE_TILE
    spec = lambda shape, index: pl.BlockSpec(shape, lambda i, *prefetch: index(i))
    grid_spec = pltpu.PrefetchScalarGridSpec(
        num_scalar_prefetch=4,
        grid=(t // tm,),
        in_specs=[
            spec((tm, d), lambda i: (i, 0)),
            spec((1, d), lambda i: (0, 0)),
            spec((SUBLANES, tm), lambda i: (0, i)),
            pl.BlockSpec((1, SUBLANES, LANES), lambda i, *prefetch: (i, 0, 0), memory_space=pltpu.SMEM),
            pl.BlockSpec((1, SUBLANES, LANES), lambda i, *prefetch: (jnp.maximum(i - 1, 0), 0, 0),
                         memory_space=pltpu.SMEM),
        ],
        out_specs=pl.BlockSpec(memory_space=pl.ANY),
        scratch_shapes=[
            pltpu.VMEM((2, MOE_TILE_BUF, d // 2), U32),
            pltpu.VMEM((PAD_FILL_SIZES[0], d // 2), U32),
            pltpu.SemaphoreType.DMA((2, len(RUN_SIZES))),
            pltpu.SemaphoreType.DMA((len(PAD_FILL_SIZES),)),
        ],
    )
    return pl.pallas_call(
        _dispatch_kernel,
        grid_spec=grid_spec,
        out_shape=jax.ShapeDtypeStruct((cap, d // 2), U32),
        compiler_params=_params("arbitrary"),
        name="moe_dispatch",
    )(pstart, pad_start, pad_gap, n_used, x, norm_w, gates, tinfo, tinfo)


def _expert_kernel(be_ref, nused_ref, rows_ref, wu_ref, wd_ref, y_ref, wu_bf_ref, wd_bf_ref):
    ff = wd_ref.shape[0]
    i = pl.program_id(0)
    used = i < nused_ref[0]
    new_expert = jnp.logical_or(i == 0, be_ref[i] != be_ref[jnp.maximum(i - 1, 0)])

    @pl.when(jnp.logical_and(used, new_expert))
    def _():
        wu_bf_ref[...] = wu_ref[...].astype(BF16)
        wd_bf_ref[...] = wd_ref[...].astype(BF16)

    @pl.when(used)
    def _():
        x_left, x_right = _unpack_halves(rows_ref[...])
        half = x_left.shape[1]
        gu = _dot(x_left, wu_bf_ref[0:half, :]) + _dot(x_right, wu_bf_ref[half:2 * half, :])
        act = _silu(gu[:, 0:ff]) * gu[:, ff:2 * ff]
        y = _dot(act.astype(BF16), wd_bf_ref[...])
        y_ref[...] = _pack_halves(y.astype(BF16).astype(F32))

    @pl.when(jnp.logical_not(used))
    def _():
        y_ref[...] = jnp.zeros_like(y_ref)


def _experts(block_expert, n_used, rows, w_up, w_down, layer):
    cap, dp = rows.shape
    d = 2 * dp
    rb = MOE_ROW_BLOCK
    ff = w_down.shape[2]
    grid_spec = pltpu.PrefetchScalarGridSpec(
        num_scalar_prefetch=2,
        grid=(cap // rb,),
        in_specs=[
            pl.BlockSpec((rb, dp), lambda i, be, nu: (jnp.minimum(i, nu[0] - 1), 0)),
            pl.BlockSpec((None, None, d, 2 * ff), lambda i, be, nu: (layer, be[i], 0, 0)),
            pl.BlockSpec((None, None, ff, d), lambda i, be, nu: (layer, be[i], 0, 0)),
        ],
        out_specs=pl.BlockSpec((rb, dp), lambda i, be, nu: (i, 0)),
        scratch_shapes=[pltpu.VMEM((d, 2 * ff), BF16), pltpu.VMEM((ff, d), BF16)],
    )
    return pl.pallas_call(
        _expert_kernel,
        grid_spec=grid_spec,
        out_shape=jax.ShapeDtypeStruct((cap, dp), U32),
        compiler_params=_params("arbitrary"),
        name="moe_experts",
    )(block_expert, n_used, rows, w_up, w_down)


def _combine_kernel(pstart_ref, x_ref, gate_ref, tinfo_ref, tnext_ref, fw_ref, y_ref, o_ref, buf_ref, sem,
                    *, final_norm):
    tm = x_ref.shape[0]
    nbuf = buf_ref.shape[1]
    i = pl.program_id(0)
    cur = i % 2

    def piece_into(which):
        def piece(buf_row, sorted_row, n, cls):
            return pltpu.make_async_copy(y_ref.at[pl.ds(sorted_row, n)], buf_ref.at[which, pl.ds(buf_row, n)],
                                         sem.at[which, cls])
        return piece

    @pl.when(i == 0)
    def _():
        buf_ref[...] = jnp.zeros_like(buf_ref)
        _for_each_run_piece(tinfo_ref, pstart_ref, lambda *a: piece_into(cur)(*a).start())

    @pl.when(i + 1 < pl.num_programs(0))
    def _():
        _for_each_run_piece(tnext_ref, pstart_ref, lambda *a: piece_into(1 - cur)(*a).start())

    g = gate_ref[...]
    cols = []
    for c in range(tm // LANES):
        sq = jnp.concatenate([g[:, c * LANES:(c + 1) * LANES], jnp.zeros((LANES - SUBLANES, LANES), F32)], axis=0)
        cols.append(sq.T)
    cols = jnp.concatenate(cols, axis=0)
    r = lax.broadcasted_iota(I32, (tm, nbuf), 1)
    weights = (jnp.where(r == cols[:, 2:3].astype(I32), cols[:, 0:1], 0.0)
               + jnp.where(r == cols[:, 3:4].astype(I32), cols[:, 1:2], 0.0))
    w_hi = weights.astype(BF16)
    w_lo = (weights - w_hi.astype(F32)).astype(BF16)

    _for_each_run_piece(tinfo_ref, pstart_ref, lambda *a: piece_into(cur)(*a).wait())

    y_left, y_right = _unpack_halves(buf_ref[cur])
    moe = jnp.concatenate([_dot(w_hi, y_left) + _dot(w_lo, y_left), _dot(w_hi, y_right) + _dot(w_lo, y_right)],
                          axis=1)
    out = x_ref[...] + moe
    if final_norm:
        out = _rmsnorm_rows(out, fw_ref[...])
    o_ref[...] = out


def _combine(pstart, x, gates, tinfo, y_rows, final_w, final_norm):
    t, d = x.shape
    tm = MOE_TILE
    last = t // tm - 1
    grid_spec = pltpu.PrefetchScalarGridSpec(
        num_scalar_prefetch=1,
        grid=(t // tm,),
        in_specs=[
            pl.BlockSpec((tm, d), lambda i, ps: (i, 0)),
            pl.BlockSpec((SUBLANES, tm), lambda i, ps: (0, i)),
            pl.BlockSpec((1, SUBLANES, LANES), lambda i, ps: (i, 0, 0), memory_space=pltpu.SMEM),
            pl.BlockSpec((1, SUBLANES, LANES), lambda i, ps: (jnp.minimum(i + 1, last), 0, 0),
                         memory_space=pltpu.SMEM),
            pl.BlockSpec((1, d), lambda i, ps: (0, 0)),
            pl.BlockSpec(memory_space=pl.ANY),
        ],
        out_specs=pl.BlockSpec((tm, d), lambda i, ps: (i, 0)),
        scratch_shapes=[pltpu.VMEM((2, MOE_TILE_BUF, d // 2), U32), pltpu.SemaphoreType.DMA((2, len(RUN_SIZES)))],
    )
    return pl.pallas_call(
        functools.partial(_combine_kernel, final_norm=final_norm),
        grid_spec=grid_spec,
        out_shape=jax.ShapeDtypeStruct((t, d), F32),
        compiler_params=_params("arbitrary"),
        name="moe_combine",
    )(pstart, x, gates, tinfo, tinfo, final_w, y_rows)


def _hier_moe(x, norm_w, w_group, b_group, w_router, b_router, w_up, w_down, layer, final_w, final_norm):
    t, d = x.shape
    g, _, epg = w_router.shape
    pad_rows = SUBLANES - g
    wr_t = jnp.concatenate([w_group.T, jnp.zeros((pad_rows, d), F32),
                            w_router.transpose(0, 2, 1).reshape(g * epg, d)], axis=0)
    br = jnp.concatenate([b_group, jnp.full((pad_rows,), -1e30, F32), b_router.reshape(-1)])[:, None]
    gates, tinfo, cnt = _router(x, norm_w, wr_t, br)

    rb = MOE_ROW_BLOCK
    n_exp = g * epg
    counts = cnt[:, 0].astype(I32)
    padded = ((counts + rb - 1) // rb) * rb
    pends = jnp.cumsum(padded)
    pstart = (pends - padded).astype(I32)
    max_rows = 2 * t + (t // MOE_TILE) * n_exp * (RUN_ALIGN - 1) + n_exp * (rb - 1)
    n_blocks = -(-max_rows // rb)
    block_start = jnp.arange(n_blocks, dtype=I32) * rb
    block_expert = jnp.minimum(jnp.sum(block_start[:, None] >= pends[None, :], axis=1), n_exp - 1).astype(I32)
    n_used = (pends[-1:] // rb).astype(I32)

    rows = _dispatch(pstart, (pstart + counts).astype(I32), (padded - counts).astype(I32), n_used,
                     x, norm_w, gates, tinfo, n_blocks * rb)
    y_rows = _experts(block_expert, n_used, rows, w_up, w_down, layer)
    return _combine(pstart, x, gates, tinfo, y_rows, final_w, final_norm)


def _ssd_in_kernel(x_ref, nw_ref, wz_ref, wx_ref, wdt_ref, z_ref, xbc_ref, dt_ref):
    h = _rmsnorm_rows(x_ref[...], nw_ref[...]).astype(BF16)
    z_ref[...] = _dot(h, wz_ref[...])
    xbc_ref[...] = _dot(h, wx_ref[...])
    dt_ref[...] = _dot(h, wdt_ref[...])


def _ssd_in_proj(x, norm_w, w_z, w_xbc, w_dt, tm=256):
    t, d = x.shape
    dz, dx, dh = w_z.shape[1], w_xbc.shape[1], w_dt.shape[1]
    resident = lambda n: pl.BlockSpec((d, n), lambda i: (0, 0), pipeline_mode=pl.Buffered(1))
    return pl.pallas_call(
        _ssd_in_kernel,
        grid=(t // tm,),
        in_specs=[
            pl.BlockSpec((tm, d), lambda i: (i, 0)),
            pl.BlockSpec((1, d), lambda i: (0, 0)),
            resident(dz), resident(dx), resident(dh),
        ],
        out_specs=[
            pl.BlockSpec((tm, dz), lambda i: (i, 0)),
            pl.BlockSpec((tm, dx), lambda i: (i, 0)),
            pl.BlockSpec((tm, dh), lambda i: (i, 0)),
        ],
        out_shape=[
            jax.ShapeDtypeStruct((t, dz), F32),
            jax.ShapeDtypeStruct((t, dx), F32),
            jax.ShapeDtypeStruct((t, dh), F32),
        ],
        compiler_params=_params("parallel"),
        name="ssd_in_proj",
    )(x, norm_w, w_z, w_xbc, w_dt)


def _ssd_kernel(xbc_ref, z_ref, dt_ref, cw_ref, cb_ref, dtb_ref, alog_ref, dskip_ref, nw_ref,
                y_ref, ext_ref, state_ref, *, d_inner, n_heads):
    L = SSD_CHUNK
    G = SSD_GROUPS
    N = SSD_STATE
    P = SSD_HEAD_DIM
    R = n_heads // G
    GW = R * P
    halo = SUBLANES

    @pl.when(pl.program_id(1) == 0)
    def _():
        ext_ref[...] = jnp.zeros_like(ext_ref)
        state_ref[...] = jnp.zeros_like(state_ref)

    def conv_silu(c0, width):
        cols = slice(c0, c0 + width)
        cur = xbc_ref[:, cols]
        prev = ext_ref[:, cols]
        row = lax.broadcasted_iota(I32, (halo, width), 0)
        acc = cb_ref[:, cols] + cur * cw_ref[SSD_CONV - 1:SSD_CONV, cols]
        for back in range(1, SSD_CONV):
            rolled = pltpu.roll(cur, back, 0)
            head = jnp.where(row < back, pltpu.roll(prev, back, 0), rolled[0:halo])
            shifted = jnp.concatenate([head, rolled[halo:]], axis=0)
            acc = acc + shifted * cw_ref[SSD_CONV - 1 - back:SSD_CONV - back, cols]
        return _silu(acc)

    dt = _softplus(dt_ref[...] + dtb_ref[...])
    a_neg = -jnp.exp(alog_ref[...])
    da = dt * a_neg
    rr = lax.broadcasted_iota(I32, (L, L), 0)
    cc = lax.broadcasted_iota(I32, (L, L), 1)
    lower = rr >= cc
    cs = _dot_f32_rhs(lower.astype(BF16), da)
    cs_last = cs[L - 1:L, :]
    cs_sq = jnp.concatenate([cs, jnp.zeros((L, L - n_heads), F32)], axis=1).T
    decay_to_end = jnp.exp(cs_last - cs)
    decay_from_start = jnp.exp(cs)
    chunk_decay = jnp.exp(cs_last)

    head_of_lane = lax.broadcasted_iota(I32, (n_heads, d_inner), 1) // P
    head_row = lax.broadcasted_iota(I32, (n_heads, d_inner), 0)
    spread = (head_row == head_of_lane).astype(BF16)
    rows8 = lambda v: jnp.broadcast_to(v, (SUBLANES, n_heads))
    per_head = jnp.concatenate([dt, decay_to_end, decay_from_start, rows8(chunk_decay), rows8(dskip_ref[...])],
                               axis=0)
    wide = _dot_f32_lhs(per_head, spread)
    lane_head = lax.broadcasted_iota(I32, (1, GW), 1) // P

    for g in range(G):
        cols = slice(g * GW, (g + 1) * GW)
        dt_g = wide[0:L, cols]
        to_end_g = wide[L:2 * L, cols]
        from_start_g = wide[2 * L:3 * L, cols]
        chunk_decay_g = wide[3 * L:3 * L + 1, cols]
        dskip_g = wide[3 * L + SUBLANES:3 * L + SUBLANES + 1, cols]

        x_g = conv_silu(g * GW, GW)
        b_g = conv_silu(d_inner + g * N, N)
        c_g = conv_silu(d_inner + G * N + g * N, N)
        xdt = x_g * dt_g
        b_bf = b_g.astype(BF16)
        c_bf = c_g.astype(BF16)

        cb = _dot(c_bf, b_bf, NT_DIMS)
        xdt_bf = xdt.astype(BF16)
        y_g = jnp.zeros((L, GW), F32)
        for r in range(R):
            hd = g * R + r
            seg = cs[:, hd:hd + 1] - cs_sq[hd:hd + 1, :]
            m = jnp.where(lower, cb * jnp.exp(seg), 0.0)
            x_r = jnp.where(lane_head == r, xdt_bf, jnp.zeros_like(xdt_bf))
            y_g = y_g + _dot(m.astype(BF16), x_r)
        prev = state_ref[g]
        y_g = y_g + from_start_g * _dot(c_bf, prev.astype(BF16))
        new_part = _dot(b_bf, (xdt * to_end_g).astype(BF16), TN_DIMS)
        state_ref[g] = prev * chunk_decay_g + new_part
        y_g = y_g + x_g * dskip_g

        yz = y_g * _silu(z_ref[:, g * GW:(g + 1) * GW])
        yn = yz * lax.rsqrt(jnp.mean(yz * yz, axis=-1, keepdims=True) + EPS)
        y_ref[:, g * GW:(g + 1) * GW] = (yn * nw_ref[:, g * GW:(g + 1) * GW]).astype(BF16)

    ext_ref[...] = xbc_ref[L - halo:L, :]


def _ssd_scan(xbc, z, dt, conv_w, conv_b, dt_bias, a_log, d_skip, norm_w, batch, seq):
    t, dx = xbc.shape
    d_inner = z.shape[1]
    n_heads = dt.shape[1]
    L = SSD_CHUNK
    nc = seq // L
    row = lambda n: pl.BlockSpec((1, n), lambda b, c: (0, 0))
    kern = functools.partial(_ssd_kernel, d_inner=d_inner, n_heads=n_heads)
    return pl.pallas_call(
        kern,
        grid=(batch, nc),
        in_specs=[
            pl.BlockSpec((L, dx), lambda b, c: (b * nc + c, 0)),
            pl.BlockSpec((L, d_inner), lambda b, c: (b * nc + c, 0)),
            pl.BlockSpec((L, n_heads), lambda b, c: (b * nc + c, 0)),
            pl.BlockSpec((SSD_CONV, dx), lambda b, c: (0, 0)),
            row(dx), row(n_heads), row(n_heads), row(n_heads), row(d_inner),
        ],
        out_specs=pl.BlockSpec((L, d_inner), lambda b, c: (b * nc + c, 0)),
        out_shape=jax.ShapeDtypeStruct((t, d_inner), BF16),
        scratch_shapes=[
            pltpu.VMEM((SUBLANES, dx), F32),
            pltpu.VMEM((SSD_GROUPS, SSD_STATE, d_inner // SSD_GROUPS), F32),
        ],
        compiler_params=_params("arbitrary", "arbitrary"),
        name="ssd_scan",
    )(xbc, z, dt, conv_w, conv_b, dt_bias, a_log, d_skip, norm_w)


def _rope_tables(seq):
    half = ATTN_HEAD_DIM // 2
    inv = ROPE_THETA ** (-jnp.arange(half, dtype=F32) / half)
    ang = jnp.arange(seq).astype(F32)[:, None] * inv[None, :]
    cos = jnp.cos(ang)
    sin = jnp.sin(ang)
    reps = LANES // ATTN_HEAD_DIM
    cos_t = jnp.tile(jnp.concatenate([cos, cos], axis=1), (1, reps))
    sin_t = jnp.tile(jnp.concatenate([-sin, sin], axis=1), (1, reps))
    return cos_t, sin_t


def kernel(x, mix_norm, ffn_norm, final_norm, attn_w_qkv, attn_w_o, ssd_w_in, ssd_conv_w, ssd_conv_b,
           ssd_dt_bias, ssd_a_log, ssd_d, ssd_norm, ssd_w_out, moe_w_group, moe_b_group, moe_w_router,
           moe_b_router, moe_w_up, moe_w_down):
    batch, seq, d = x.shape
    assert seq % MOBA_BLOCK == 0 and seq % SSD_CHUNK == 0 and d % LANES == 0
    depth = mix_norm.shape[0]
    t = batch * seq
    xt = x.reshape(t, d)
    cos_t, sin_t = _rope_tables(seq)
    final_w = final_norm[None, :]

    for i in range(depth):
        j = i // 2
        nw = mix_norm[i][None, :]
        if i % 2 == 0:
            w_qkv = attn_w_qkv[j].astype(BF16)
            q, k, vt, kmean = _qkv_rope(xt, nw, w_qkv[:, 0:2 * d], w_qkv[:, 2 * d:].T, cos_t, sin_t, seq)
            kmean = kmean.reshape(batch, seq // MOBA_BLOCK, d)
            o = _moba_attention(q, k, vt, kmean, batch, seq)
            xt = _proj_residual(o, attn_w_o[j].astype(BF16), xt)
        else:
            w_in = ssd_w_in[j].astype(BF16)
            d_inner = ssd_norm.shape[1]
            dx = ssd_conv_b.shape[1]
            z, xbc, dt = _ssd_in_proj(xt, nw, w_in[:, 0:d_inner], w_in[:, d_inner:d_inner + dx],
                                      w_in[:, d_inner + dx:])
            y = _ssd_scan(xbc, z, dt, ssd_conv_w[j][:, 0, :], ssd_conv_b[j][None, :], ssd_dt_bias[j][None, :],
                          ssd_a_log[j][None, :], ssd_d[j][None, :], ssd_norm[j][None, :], batch, seq)
            xt = _proj_residual(y, ssd_w_out[j].astype(BF16), xt)
        xt = _hier_moe(xt, ffn_norm[i][None, :], moe_w_group[i], moe_b_group[i], moe_w_router[i],
                       moe_b_router[i], moe_w_up, moe_w_down, i, final_w, final_norm=(i == depth - 1))
    return xt.reshape(batch, seq, d)
```

```python
import functools
import math

import jax
import jax.numpy as jnp
from jax import lax
from jax.experimental import pallas as pl
from jax.experimental.pallas import tpu as pltpu

F32 = jnp.float32
BF16 = jnp.bfloat16
I32 = jnp.int32

EPS = 1e-6
LANES = 128
SUBLANES = 8
VMEM_LIMIT = 56 * 1024 * 1024

ATTN_HEADS = 16
ATTN_HEAD_DIM = 64
MOBA_BLOCK = 256
MOBA_TOPK = 3
ROPE_THETA = 10000.0
HEADS_PER_TILE = LANES // ATTN_HEAD_DIM

SSD_HEAD_DIM = 64
SSD_GROUPS = 8
SSD_STATE = 128
SSD_CONV = 4
SSD_CHUNK = 128

MOE_GROUPS = 4
MOE_EXPERTS_PER_GROUP = 8
MOE_EXPERTS = MOE_GROUPS * MOE_EXPERTS_PER_GROUP
MOE_ROW_BLOCK = 512
ROUTER_ROWS = SUBLANES + MOE_EXPERTS
MOE_TILE = 256
RUN_ALIGN = SUBLANES
RUN_SIZES = (16, 8)
MOE_TILE_BUF = 2 * MOE_TILE + MOE_EXPERTS * RUN_ALIGN
PAD_FILL_SIZES = tuple(MOE_ROW_BLOCK >> s for s in range((MOE_ROW_BLOCK // RUN_ALIGN).bit_length()))

NT_DIMS = (((1,), (1,)), ((), ()))
NN_DIMS = (((1,), (0,)), ((), ()))
TN_DIMS = (((0,), (0,)), ((), ()))


def _params(*sem):
    return pltpu.CompilerParams(dimension_semantics=sem, vmem_limit_bytes=VMEM_LIMIT)


def _rmsnorm_rows(x, w):
    var = jnp.mean(x * x, axis=-1, keepdims=True)
    return (x * lax.rsqrt(var + EPS)) * w


def _split3(x):
    hi = x.astype(BF16)
    r = x - hi.astype(F32)
    mid = r.astype(BF16)
    lo = (r - mid.astype(F32)).astype(BF16)
    return hi, mid, lo


def _dot(a, b, dims=NN_DIMS):
    return lax.dot_general(a, b, dims, preferred_element_type=F32)


def _dot_f32_lhs(x, e, dims=NN_DIMS):
    hi, mid, lo = _split3(x)
    return _dot(hi, e, dims) + _dot(mid, e, dims) + _dot(lo, e, dims)


def _dot_f32_rhs(e, x, dims=NN_DIMS):
    hi, mid, lo = _split3(x)
    return _dot(e, hi, dims) + _dot(e, mid, dims) + _dot(e, lo, dims)


def _dot_f32_3pass(a, b, dims=NN_DIMS):
    ah, am, _ = _split3(a)
    bh, bm, _ = _split3(b)
    return _dot(ah, bh, dims) + _dot(ah, bm, dims) + _dot(am, bh, dims)


def _silu(x):
    h = 0.5 * x
    return h + h * jnp.tanh(h)


def _softplus(x):
    return jnp.maximum(x, 0.0) + jnp.log(1.0 + jnp.exp(-jnp.abs(x)))


def _qkv_kernel(x_ref, nw_ref, wqk_ref, wvt_ref, cos_ref, sin_ref, q_ref, k_ref, vt_ref, km_ref, *, d, scale):
    h = _rmsnorm_rows(x_ref[...], nw_ref[...]).astype(BF16)
    cos = cos_ref[...]
    sin = sin_ref[...]
    lane = lax.broadcasted_iota(I32, (1, LANES), 1)
    first_half = (lane % ATTN_HEAD_DIM) < (ATTN_HEAD_DIM // 2)

    def rope_tile(xc):
        lo_partner = pltpu.roll(xc, LANES - ATTN_HEAD_DIM // 2, 1)
        hi_partner = pltpu.roll(xc, ATTN_HEAD_DIM // 2, 1)
        return xc * cos + jnp.where(first_half, lo_partner, hi_partner) * sin

    q = _dot(h, wqk_ref[:, 0:d])
    k = _dot(h, wqk_ref[:, d:2 * d])
    for c in range(d // LANES):
        sl = slice(c * LANES, (c + 1) * LANES)
        q_ref[:, sl] = (rope_tile(q[:, sl]) * scale).astype(BF16)
        kc = rope_tile(k[:, sl])
        k_ref[:, sl] = kc.astype(BF16)
        km_ref[0, :, sl] = jnp.mean(kc, axis=0, keepdims=True)
    vt_ref[0] = _dot(wvt_ref[...], h, NT_DIMS).astype(BF16)


def _qkv_rope(x, norm_w, w_qk, w_vt, cos_t, sin_t, seq):
    t, d = x.shape
    tm = MOBA_BLOCK
    n_tiles = t // tm
    seq_tiles = seq // tm
    kern = functools.partial(_qkv_kernel, d=d, scale=math.log2(math.e) / math.sqrt(ATTN_HEAD_DIM))
    return pl.pallas_call(
        kern,
        grid=(n_tiles,),
        in_specs=[
            pl.BlockSpec((tm, d), lambda i: (i, 0)),
            pl.BlockSpec((1, d), lambda i: (0, 0)),
            pl.BlockSpec((d, 2 * d), lambda i: (0, 0), pipeline_mode=pl.Buffered(1)),
            pl.BlockSpec((d, d), lambda i: (0, 0), pipeline_mode=pl.Buffered(1)),
            pl.BlockSpec((tm, LANES), lambda i: (i % seq_tiles, 0)),
            pl.BlockSpec((tm, LANES), lambda i: (i % seq_tiles, 0)),
        ],
        out_specs=[
            pl.BlockSpec((tm, d), lambda i: (i, 0)),
            pl.BlockSpec((tm, d), lambda i: (i, 0)),
            pl.BlockSpec((1, d, tm), lambda i: (i, 0, 0)),
            pl.BlockSpec((1, 1, d), lambda i: (i, 0, 0)),
        ],
        out_shape=[
            jax.ShapeDtypeStruct((t, d), BF16),
            jax.ShapeDtypeStruct((t, d), BF16),
            jax.ShapeDtypeStruct((n_tiles, d, tm), BF16),
            jax.ShapeDtypeStruct((n_tiles, 1, d), F32),
        ],
        compiler_params=_params("parallel"),
        name="qkv_rope",
    )(x, norm_w, w_qk, w_vt, cos_t, sin_t)


def _attn_kernel(q_ref, k_ref, vt_ref, km_ref, o_ref, sa_ref, sb_ref, p_ref, *, nb):
    qi = pl.program_id(2)
    tq = q_ref.shape[0]
    q2 = q_ref[...]
    km = km_ref[...]
    lane = lax.broadcasted_iota(I32, (1, LANES), 1)
    key = lax.broadcasted_iota(I32, (MOBA_BLOCK, tq), 0)
    qry = lax.broadcasted_iota(I32, (MOBA_BLOCK, tq), 1)
    causal = key <= qry
    blk = lax.broadcasted_iota(I32, (nb, tq), 0)
    neg_inf = jnp.float32(-jnp.inf)
    heads = range(HEADS_PER_TILE)

    qh = [jnp.where((lane // ATTN_HEAD_DIM) == h, q2, jnp.zeros_like(q2)) for h in heads]

    bits = []
    for h in heads:
        gate = _dot_f32_lhs(km, qh[h], NT_DIMS)
        alive = (blk < qi).astype(I32)
        sel = jnp.zeros((nb, tq), I32)
        for _ in range(min(MOBA_TOPK, nb)):
            gm = jnp.where(alive > 0, gate, neg_inf)
            top = jnp.max(gm, axis=0, keepdims=True)
            cand = jnp.where((alive > 0) & (gm == top), blk, nb)
            idx = jnp.min(cand, axis=0, keepdims=True)
            pick = (blk == idx).astype(I32)
            sel = sel | pick
            alive = alive & (1 - pick)
        bits.append(jnp.sum(sel << blk, axis=0, keepdims=True))

    def block_of(k):
        return jnp.where(k == 0, qi, jnp.minimum(k - 1, qi))

    def issue_scores(k, s_out):
        start = pl.multiple_of(block_of(k) * MOBA_BLOCK, MOBA_BLOCK)
        kb = k_ref[pl.ds(start, MOBA_BLOCK), :]
        for h in heads:
            s_out[h] = _dot(kb, qh[h], NT_DIMS)

    def issue_values(k):
        blk_idx = block_of(k)
        return [_dot(vt_ref[blk_idx, h * ATTN_HEAD_DIM:(h + 1) * ATTN_HEAD_DIM, :], p_ref[h]) for h in heads]

    def softmax(k, s_in, state, diag):
        out = []
        for h in heads:
            m_i, l_i = state[h]
            if diag:
                s = jnp.where(causal, s_in[h], neg_inf)
                m_new = jnp.maximum(m_i, jnp.max(s, axis=0, keepdims=True))
                shift = m_new
            else:
                s = s_in[h]
                keep = ((bits[h] >> (k - 1)) & 1) == 1
                m_new = jnp.maximum(m_i, jnp.where(keep, jnp.max(s, axis=0, keepdims=True), neg_inf))
                shift = jnp.where(keep, m_new, -neg_inf)
            alpha = jnp.exp2(m_i - m_new)
            p = jnp.exp2(s - shift)
            l_new = alpha * l_i + jnp.sum(p, axis=0, keepdims=True)
            p_ref[h] = p.astype(BF16)
            out.append((m_new, l_new, alpha))
        return out

    def phase(k, carry, s_cur, s_nxt):
        issue_scores(k + 1, s_nxt)
        pv = issue_values(k - 1)
        stats = softmax(k, s_cur, [(c[0], c[1]) for c in carry], False)
        return tuple((stats[h][0], stats[h][1], carry[h][3] * carry[h][2] + pv[h], stats[h][2]) for h in heads)

    issue_scores(0, sa_ref)
    issue_scores(1, sb_ref)
    first = softmax(0, sa_ref, [(jnp.full((1, tq), neg_inf, F32), jnp.zeros((1, tq), F32)) for _ in heads], True)
    carry = tuple((m, l, jnp.zeros((ATTN_HEAD_DIM, tq), F32), a) for (m, l, a) in first)

    def two_phases(u, c):
        c = phase(2 * u + 1, c, sb_ref, sa_ref)
        return phase(2 * u + 2, c, sa_ref, sb_ref)

    n_pairs = (qi + 1) // 2
    carry = lax.fori_loop(0, n_pairs, two_phases, carry)
    pv = issue_values(2 * n_pairs)
    o_t = jnp.concatenate([(alpha * acc + pv[h]) / l_i for h, (_, l_i, acc, alpha) in enumerate(carry)],
                          axis=0)
    o_ref[...] = o_t.T.astype(BF16)


def _moba_attention(q, k, vt, kmean, batch, seq):
    t, d = q.shape
    nb = seq // MOBA_BLOCK
    tq = MOBA_BLOCK
    nq = seq // tq
    col_tiles = d // LANES
    kern = functools.partial(_attn_kernel, nb=nb)
    return pl.pallas_call(
        kern,
        grid=(batch, col_tiles, nq),
        in_specs=[
            pl.BlockSpec((tq, LANES), lambda b, c, i: (b * nq + i, c)),
            pl.BlockSpec((seq, LANES), lambda b, c, i: (b, c)),
            pl.BlockSpec((nb, LANES, MOBA_BLOCK), lambda b, c, i: (b, c, 0)),
            pl.BlockSpec((None, nb, LANES), lambda b, c, i: (b, 0, c)),
        ],
        out_specs=pl.BlockSpec((tq, LANES), lambda b, c, i: (b * nq + i, c)),
        out_shape=jax.ShapeDtypeStruct((t, d), BF16),
        scratch_shapes=[
            pltpu.VMEM((HEADS_PER_TILE, MOBA_BLOCK, tq), F32),
            pltpu.VMEM((HEADS_PER_TILE, MOBA_BLOCK, tq), F32),
            pltpu.VMEM((HEADS_PER_TILE, MOBA_BLOCK, tq), BF16),
        ],
        compiler_params=_params("parallel", "parallel", "arbitrary"),
        name="moba_attention",
    )(q, k, vt, kmean)


def _proj_res_kernel(a_ref, w_ref, res_ref, o_ref):
    o_ref[...] = res_ref[...] + _dot(a_ref[...], w_ref[...])


def _proj_residual(a, w, res, tm=512):
    t, kdim = a.shape
    d = w.shape[1]
    return pl.pallas_call(
        _proj_res_kernel,
        grid=(t // tm,),
        in_specs=[
            pl.BlockSpec((tm, kdim), lambda i: (i, 0)),
            pl.BlockSpec((kdim, d), lambda i: (0, 0), pipeline_mode=pl.Buffered(1)),
            pl.BlockSpec((tm, d), lambda i: (i, 0)),
        ],
        out_specs=pl.BlockSpec((tm, d), lambda i: (i, 0)),
        out_shape=jax.ShapeDtypeStruct((t, d), F32),
        compiler_params=_params("parallel"),
        name="proj_residual",
    )(a, w, res)


def _router_kernel(x_ref, nw_ref, wr_ref, br_ref, gate_ref, tinfo_ref, cnt_ref, carry_ref):
    tm = x_ref.shape[0]

    @pl.when(pl.program_id(0) == 0)
    def _():
        carry_ref[...] = jnp.zeros_like(carry_ref)

    hn = _rmsnorm_rows(x_ref[...], nw_ref[...])
    logits = _dot_f32_3pass(wr_ref[...], hn, NT_DIMS) + br_ref[...]
    sub = lax.broadcasted_iota(I32, (SUBLANES, tm), 0)
    neg_inf = jnp.float32(-jnp.inf)

    gl = logits[0:SUBLANES]
    gmax = jnp.max(gl, axis=0, keepdims=True)
    gidx = jnp.min(jnp.where(gl == gmax, sub, SUBLANES), axis=0, keepdims=True)
    gprob = 1.0 / jnp.sum(jnp.exp(gl - gmax), axis=0, keepdims=True)

    el = jnp.zeros((SUBLANES, tm), F32)
    for g in range(MOE_GROUPS):
        el = jnp.where(gidx == g, logits[SUBLANES * (g + 1):SUBLANES * (g + 2)], el)
    m1 = jnp.max(el, axis=0, keepdims=True)
    i1 = jnp.min(jnp.where(el == m1, sub, SUBLANES), axis=0, keepdims=True)
    el2 = jnp.where(sub == i1, neg_inf, el)
    m2 = jnp.max(el2, axis=0, keepdims=True)
    i2 = jnp.min(jnp.where(el2 == m2, sub, SUBLANES), axis=0, keepdims=True)
    ratio = jnp.exp(m2 - m1)
    den = 1.0 + ratio
    gate_ref[...] = jnp.zeros_like(gate_ref)
    gate_ref[0:1, :] = gprob / den
    gate_ref[1:2, :] = gprob * ratio / den

    e1 = gidx * MOE_EXPERTS_PER_GROUP + i1
    e2 = gidx * MOE_EXPERTS_PER_GROUP + i2

    eiota = lax.broadcasted_iota(I32, (MOE_EXPERTS, tm), 0)
    oh1 = eiota == e1
    oh2 = eiota == e2
    onehot = (oh1 | oh2).astype(BF16)
    src = lax.broadcasted_iota(I32, (tm, tm), 0)
    dst = lax.broadcasted_iota(I32, (tm, tm), 1)
    before = (src < dst).astype(BF16)
    prefix = _dot(onehot, before)
    count = _dot(onehot, jnp.ones((tm, tm), BF16))
    total = jnp.floor((count + (RUN_ALIGN - 1)) * (1.0 / RUN_ALIGN)) * RUN_ALIGN
    erow = lax.broadcasted_iota(I32, (MOE_EXPERTS, MOE_EXPERTS), 0)
    ecol = lax.broadcasted_iota(I32, (MOE_EXPERTS, MOE_EXPERTS), 1)
    run_start = _dot_f32_rhs((ecol < erow).astype(BF16), total)
    slot = run_start + prefix
    gate_ref[2:3, :] = jnp.sum(jnp.where(oh1, slot, 0.0), axis=0, keepdims=True)
    gate_ref[3:4, :] = jnp.sum(jnp.where(oh2, slot, 0.0), axis=0, keepdims=True)

    lane = lax.broadcasted_iota(I32, (MOE_EXPERTS, tm), 1)
    on_lanes = lambda v: jnp.sum(jnp.where(eiota == lane, v, 0.0), axis=0, keepdims=True)[:, 0:LANES].astype(I32)
    tinfo_ref[...] = jnp.zeros_like(tinfo_ref)
    tinfo_ref[0, 0:1, :] = on_lanes(total)
    tinfo_ref[0, 1:2, :] = on_lanes(carry_ref[...])
    tinfo_ref[0, 2:3, :] = on_lanes(run_start)
    carry_ref[...] = carry_ref[...] + total
    cnt_ref[...] = carry_ref[...]


def _router(x, norm_w, wr_t, br):
    t, d = x.shape
    tm = MOE_TILE
    return pl.pallas_call(
        _router_kernel,
        grid=(t // tm,),
        in_specs=[
            pl.BlockSpec((tm, d), lambda i: (i, 0)),
            pl.BlockSpec((1, d), lambda i: (0, 0)),
            pl.BlockSpec((ROUTER_ROWS, d), lambda i: (0, 0)),
            pl.BlockSpec((ROUTER_ROWS, 1), lambda i: (0, 0)),
        ],
        out_specs=[
            pl.BlockSpec((SUBLANES, tm), lambda i: (0, i)),
            pl.BlockSpec((1, SUBLANES, LANES), lambda i: (i, 0, 0)),
            pl.BlockSpec((MOE_EXPERTS, tm), lambda i: (0, 0)),
        ],
        out_shape=[
            jax.ShapeDtypeStruct((SUBLANES, t), F32),
            jax.ShapeDtypeStruct((t // tm, SUBLANES, LANES), I32),
            jax.ShapeDtypeStruct((MOE_EXPERTS, tm), F32),
        ],
        scratch_shapes=[pltpu.VMEM((MOE_EXPERTS, tm), F32)],
        compiler_params=_params("arbitrary"),
        name="moe_router",
    )(x, norm_w, wr_t, br)


U32 = jnp.uint32
HIGH_HALF = 0xFFFF0000


def _pack_halves(a):
    c = a.shape[1] // 2
    left = lax.bitcast_convert_type(a[:, :c], U32)
    right = lax.bitcast_convert_type(a[:, c:], U32)
    return (left & U32(HIGH_HALF)) | (right >> 16)


def _unpack_halves(p):
    left = lax.bitcast_convert_type(p & U32(HIGH_HALF), F32)
    right = lax.bitcast_convert_type(p << 16, F32)
    return left.astype(BF16), right.astype(BF16)


def _for_each_run_piece(tinfo_ref, pstart_ref, fn):
    full = RUN_SIZES[0]
    aligned = lambda v: pl.multiple_of(v, RUN_ALIGN)

    def per_expert(e, c):
        cnt = tinfo_ref[0, 0, e]
        sorted_row = pstart_ref[e] + tinfo_ref[0, 1, e]
        buf_row = tinfo_ref[0, 2, e]

        def whole(k, c2):
            fn(aligned(buf_row + k * full), aligned(sorted_row + k * full), full, 0)
            return c2

        lax.fori_loop(0, cnt // full, whole, 0)
        for cls, size in enumerate(RUN_SIZES[1:], 1):
            done = cnt & ~(2 * size - 1)

            @pl.when((cnt & size) != 0)
            def _():
                fn(aligned(buf_row + done), aligned(sorted_row + done), size, cls)
        return c

    lax.fori_loop(0, MOE_EXPERTS, per_expert, 0)


def _zero_unused_rows(padstart_ref, padgap_ref, nused_ref, rows_ref, zero_ref, sem):
    zero_ref[...] = jnp.zeros_like(zero_ref)
    block = PAD_FILL_SIZES[0]
    n_blocks = rows_ref.shape[0] // block

    def piece(row, n, cls):
        return pltpu.make_async_copy(zero_ref.at[pl.ds(0, n)], rows_ref.at[pl.ds(row, n)], sem.at[cls])

    def for_each_piece(fn):
        def per_expert(e, c):
            gap = padgap_ref[e]
            for cls, size in enumerate(PAD_FILL_SIZES[1:], 1):
                done = gap & ~(2 * size - 1)

                @pl.when((gap & size) != 0)
                def _():
                    fn(pl.multiple_of(padstart_ref[e] + done, RUN_ALIGN), size, cls)
            return c

        lax.fori_loop(0, MOE_EXPERTS, per_expert, 0)

        def unused_block(b, c):
            fn(pl.multiple_of(b * block, block), block, 0)
            return c

        lax.fori_loop(nused_ref[0], n_blocks, unused_block, 0)

    for_each_piece(lambda *a: piece(*a).start())
    for_each_piece(lambda *a: piece(*a).wait())


def _dispatch_kernel(pstart_ref, padstart_ref, padgap_ref, nused_ref, x_ref, nw_ref, slot_ref, tinfo_ref,
                     rows_ref, buf_ref, zero_ref, sem, zero_sem):
    tm = x_ref.shape[0]
    nbuf = buf_ref.shape[0]

    @pl.when(pl.program_id(0) == 0)
    def _():
        _zero_unused_rows(padstart_ref, padgap_ref, nused_ref, rows_ref, zero_ref, zero_sem)

    hn = _rmsnorm_rows(x_ref[...], nw_ref[...]).astype(BF16)
    r = lax.broadcasted_iota(I32, (nbuf, tm), 0)
    slots = slot_ref[...].astype(I32)
    place = ((r == slots[2:3, :]) | (r == slots[3:4, :])).astype(BF16)
    buf_ref[...] = _pack_halves(_dot(place, hn))

    def piece(buf_row, sorted_row, n, cls):
        return pltpu.make_async_copy(buf_ref.at[pl.ds(buf_row, n)], rows_ref.at[pl.ds(sorted_row, n)], sem.at[cls])

    _for_each_run_piece(tinfo_ref, pstart_ref, lambda *a: piece(*a).start())
    _for_each_run_piece(tinfo_ref, pstart_ref, lambda *a: piece(*a).wait())


def _dispatch(pstart, pad_start, pad_gap, n_used, x, norm_w, gates, tinfo, cap):
    t, d = x.shape
    tm = MOE_TILE
    spec = lambda shape, index: pl.BlockSpec(shape, lambda i, *prefetch: index(i))
    grid_spec = pltpu.PrefetchScalarGridSpec(
        num_scalar_prefetch=4,
        grid=(t // tm,),
        in_specs=[
            spec((tm, d), lambda i: (i, 0)),
            spec((1, d), lambda i: (0, 0)),
            spec((SUBLANES, tm), lambda i: (0, i)),
            pl.BlockSpec((1, SUBLANES, LANES), lambda i, *prefetch: (i, 0, 0), memory_space=pltpu.SMEM),
        ],
        out_specs=pl.BlockSpec(memory_space=pl.ANY),
        scratch_shapes=[
            pltpu.VMEM((MOE_TILE_BUF, d // 2), U32),
            pltpu.VMEM((PAD_FILL_SIZES[0], d // 2), U32),
            pltpu.SemaphoreType.DMA((len(RUN_SIZES),)),
            pltpu.SemaphoreType.DMA((len(PAD_FILL_SIZES),)),
        ],
    )
    return pl.pallas_call(
        _dispatch_kernel,
        grid_spec=grid_spec,
        out_shape=jax.ShapeDtypeStruct((cap, d // 2), U32),
        compiler_params=_params("arbitrary"),
        name="moe_dispatch",
    )(pstart, pad_start, pad_gap, n_used, x, norm_w, gates, tinfo)


def _expert_kernel(be_ref, nused_ref, rows_ref, wu_ref, wd_ref, y_ref, wu_bf_ref, wd_bf_ref):
    ff = wd_ref.shape[0]
    i = pl.program_id(0)
    used = i < nused_ref[0]
    new_expert = jnp.logical_or(i == 0, be_ref[i] != be_ref[jnp.maximum(i - 1, 0)])

    @pl.when(jnp.logical_and(used, new_expert))
    def _():
        wu_bf_ref[...] = wu_ref[...].astype(BF16)
        wd_bf_ref[...] = wd_ref[...].astype(BF16)

    @pl.when(used)
    def _():
        x_left, x_right = _unpack_halves(rows_ref[...])
        half = x_left.shape[1]
        gu = _dot(x_left, wu_bf_ref[0:half, :]) + _dot(x_right, wu_bf_ref[half:2 * half, :])
        act = _silu(gu[:, 0:ff]) * gu[:, ff:2 * ff]
        y = _dot(act.astype(BF16), wd_bf_ref[...])
        y_ref[...] = _pack_halves(y.astype(BF16).astype(F32))

    @pl.when(jnp.logical_not(used))
    def _():
        y_ref[...] = jnp.zeros_like(y_ref)


def _experts(block_expert, n_used, rows, w_up, w_down, layer):
    cap, dp = rows.shape
    d = 2 * dp
    rb = MOE_ROW_BLOCK
    ff = w_down.shape[2]
    grid_spec = pltpu.PrefetchScalarGridSpec(
        num_scalar_prefetch=2,
        grid=(cap // rb,),
        in_specs=[
            pl.BlockSpec((rb, dp), lambda i, be, nu: (jnp.minimum(i, nu[0] - 1), 0)),
            pl.BlockSpec((None, None, d, 2 * ff), lambda i, be, nu: (layer, be[i], 0, 0)),
            pl.BlockSpec((None, None, ff, d), lambda i, be, nu: (layer, be[i], 0, 0)),
        ],
        out_specs=pl.BlockSpec((rb, dp), lambda i, be, nu: (i, 0)),
        scratch_shapes=[pltpu.VMEM((d, 2 * ff), BF16), pltpu.VMEM((ff, d), BF16)],
    )
    return pl.pallas_call(
        _expert_kernel,
        grid_spec=grid_spec,
        out_shape=jax.ShapeDtypeStruct((cap, dp), U32),
        compiler_params=_params("arbitrary"),
        name="moe_experts",
    )(block_expert, n_used, rows, w_up, w_down)


def _combine_kernel(pstart_ref, x_ref, gate_ref, tinfo_ref, tnext_ref, fw_ref, y_ref, o_ref, buf_ref, sem,
                    *, final_norm):
    tm = x_ref.shape[0]
    nbuf = buf_ref.shape[1]
    i = pl.program_id(0)
    cur = i % 2

    def piece_into(which):
        def piece(buf_row, sorted_row, n, cls):
            return pltpu.make_async_copy(y_ref.at[pl.ds(sorted_row, n)], buf_ref.at[which, pl.ds(buf_row, n)],
                                         sem.at[which, cls])
        return piece

    @pl.when(i == 0)
    def _():
        buf_ref[...] = jnp.zeros_like(buf_ref)
        _for_each_run_piece(tinfo_ref, pstart_ref, lambda *a: piece_into(cur)(*a).start())

    @pl.when(i + 1 < pl.num_programs(0))
    def _():
        _for_each_run_piece(tnext_ref, pstart_ref, lambda *a: piece_into(1 - cur)(*a).start())

    g = gate_ref[...]
    cols = []
    for c in range(tm // LANES):
        sq = jnp.concatenate([g[:, c * LANES:(c + 1) * LANES], jnp.zeros((LANES - SUBLANES, LANES), F32)], axis=0)
        cols.append(sq.T)
    cols = jnp.concatenate(cols, axis=0)
    r = lax.broadcasted_iota(I32, (tm, nbuf), 1)
    weights = (jnp.where(r == cols[:, 2:3].astype(I32), cols[:, 0:1], 0.0)
               + jnp.where(r == cols[:, 3:4].astype(I32), cols[:, 1:2], 0.0))
    w_hi = weights.astype(BF16)
    w_lo = (weights - w_hi.astype(F32)).astype(BF16)

    _for_each_run_piece(tinfo_ref, pstart_ref, lambda *a: piece_into(cur)(*a).wait())

    y_left, y_right = _unpack_halves(buf_ref[cur])
    moe = jnp.concatenate([_dot(w_hi, y_left) + _dot(w_lo, y_left), _dot(w_hi, y_right) + _dot(w_lo, y_right)],
                          axis=1)
    out = x_ref[...] + moe
    if final_norm:
        out = _rmsnorm_rows(out, fw_ref[...])
    o_ref[...] = out


def _combine(pstart, x, gates, tinfo, y_rows, final_w, final_norm):
    t, d = x.shape
    tm = MOE_TILE
    last = t // tm - 1
    grid_spec = pltpu.PrefetchScalarGridSpec(
        num_scalar_prefetch=1,
        grid=(t // tm,),
        in_specs=[
            pl.BlockSpec((tm, d), lambda i, ps: (i, 0)),
            pl.BlockSpec((SUBLANES, tm), lambda i, ps: (0, i)),
            pl.BlockSpec((1, SUBLANES, LANES), lambda i, ps: (i, 0, 0), memory_space=pltpu.SMEM),
            pl.BlockSpec((1, SUBLANES, LANES), lambda i, ps: (jnp.minimum(i + 1, last), 0, 0),
                         memory_space=pltpu.SMEM),
            pl.BlockSpec((1, d), lambda i, ps: (0, 0)),
            pl.BlockSpec(memory_space=pl.ANY),
        ],
        out_specs=pl.BlockSpec((tm, d), lambda i, ps: (i, 0)),
        scratch_shapes=[pltpu.VMEM((2, MOE_TILE_BUF, d // 2), U32), pltpu.SemaphoreType.DMA((2, len(RUN_SIZES)))],
    )
    return pl.pallas_call(
        functools.partial(_combine_kernel, final_norm=final_norm),
        grid_spec=grid_spec,
        out_shape=jax.ShapeDtypeStruct((t, d), F32),
        compiler_params=_params("arbitrary"),
        name="moe_combine",
    )(pstart, x, gates, tinfo, tinfo, final_w, y_rows)


def _hier_moe(x, norm_w, w_group, b_group, w_router, b_router, w_up, w_down, layer, final_w, final_norm):
    t, d = x.shape
    g, _, epg = w_router.shape
    pad_rows = SUBLANES - g
    wr_t = jnp.concatenate([w_group.T, jnp.zeros((pad_rows, d), F32),
                            w_router.transpose(0, 2, 1).reshape(g * epg, d)], axis=0)
    br = jnp.concatenate([b_group, jnp.full((pad_rows,), -1e30, F32), b_router.reshape(-1)])[:, None]
    gates, tinfo, cnt = _router(x, norm_w, wr_t, br)

    rb = MOE_ROW_BLOCK
    n_exp = g * epg
    counts = cnt[:, 0].astype(I32)
    padded = ((counts + rb - 1) // rb) * rb
    pends = jnp.cumsum(padded)
    pstart = (pends - padded).astype(I32)
    max_rows = 2 * t + (t // MOE_TILE) * n_exp * (RUN_ALIGN - 1) + n_exp * (rb - 1)
    n_blocks = -(-max_rows // rb)
    block_start = jnp.arange(n_blocks, dtype=I32) * rb
    block_expert = jnp.minimum(jnp.sum(block_start[:, None] >= pends[None, :], axis=1), n_exp - 1).astype(I32)
    n_used = (pends[-1:] // rb).astype(I32)

    rows = _dispatch(pstart, (pstart + counts).astype(I32), (padded - counts).astype(I32), n_used,
                     x, norm_w, gates, tinfo, n_blocks * rb)
    y_rows = _experts(block_expert, n_used, rows, w_up, w_down, layer)
    return _combine(pstart, x, gates, tinfo, y_rows, final_w, final_norm)


def _ssd_in_kernel(x_ref, nw_ref, wz_ref, wx_ref, wdt_ref, z_ref, xbc_ref, dt_ref):
    h = _rmsnorm_rows(x_ref[...], nw_ref[...]).astype(BF16)
    z_ref[...] = _dot(h, wz_ref[...])
    xbc_ref[...] = _dot(h, wx_ref[...])
    dt_ref[...] = _dot(h, wdt_ref[...])


def _ssd_in_proj(x, norm_w, w_z, w_xbc, w_dt, tm=256):
    t, d = x.shape
    dz, dx, dh = w_z.shape[1], w_xbc.shape[1], w_dt.shape[1]
    resident = lambda n: pl.BlockSpec((d, n), lambda i: (0, 0), pipeline_mode=pl.Buffered(1))
    return pl.pallas_call(
        _ssd_in_kernel,
        grid=(t // tm,),
        in_specs=[
            pl.BlockSpec((tm, d), lambda i: (i, 0)),
            pl.BlockSpec((1, d), lambda i: (0, 0)),
            resident(dz), resident(dx), resident(dh),
        ],
        out_specs=[
            pl.BlockSpec((tm, dz), lambda i: (i, 0)),
            pl.BlockSpec((tm, dx), lambda i: (i, 0)),
            pl.BlockSpec((tm, dh), lambda i: (i, 0)),
        ],
        out_shape=[
            jax.ShapeDtypeStruct((t, dz), F32),
            jax.ShapeDtypeStruct((t, dx), F32),
            jax.ShapeDtypeStruct((t, dh), F32),
        ],
        compiler_params=_params("parallel"),
        name="ssd_in_proj",
    )(x, norm_w, w_z, w_xbc, w_dt)


def _ssd_kernel(xbc_ref, z_ref, dt_ref, cw_ref, cb_ref, dtb_ref, alog_ref, dskip_ref, nw_ref,
                y_ref, ext_ref, state_ref, *, d_inner, n_heads):
    L = SSD_CHUNK
    G = SSD_GROUPS
    N = SSD_STATE
    P = SSD_HEAD_DIM
    R = n_heads // G
    GW = R * P
    halo = SUBLANES

    @pl.when(pl.program_id(1) == 0)
    def _():
        ext_ref[...] = jnp.zeros_like(ext_ref)
        state_ref[...] = jnp.zeros_like(state_ref)

    def conv_silu(c0, width):
        cols = slice(c0, c0 + width)
        cur = xbc_ref[:, cols]
        prev = ext_ref[:, cols]
        row = lax.broadcasted_iota(I32, (halo, width), 0)
        acc = cb_ref[:, cols] + cur * cw_ref[SSD_CONV - 1:SSD_CONV, cols]
        for back in range(1, SSD_CONV):
            rolled = pltpu.roll(cur, back, 0)
            head = jnp.where(row < back, pltpu.roll(prev, back, 0), rolled[0:halo])
            shifted = jnp.concatenate([head, rolled[halo:]], axis=0)
            acc = acc + shifted * cw_ref[SSD_CONV - 1 - back:SSD_CONV - back, cols]
        return _silu(acc)

    dt = _softplus(dt_ref[...] + dtb_ref[...])
    a_neg = -jnp.exp(alog_ref[...])
    da = dt * a_neg
    rr = lax.broadcasted_iota(I32, (L, L), 0)
    cc = lax.broadcasted_iota(I32, (L, L), 1)
    lower = rr >= cc
    cs = _dot_f32_rhs(lower.astype(BF16), da)
    cs_last = cs[L - 1:L, :]
    cs_sq = jnp.concatenate([cs, jnp.zeros((L, L - n_heads), F32)], axis=1).T
    decay_to_end = jnp.exp(cs_last - cs)
    decay_from_start = jnp.exp(cs)
    chunk_decay = jnp.exp(cs_last)

    head_of_lane = lax.broadcasted_iota(I32, (n_heads, d_inner), 1) // P
    head_row = lax.broadcasted_iota(I32, (n_heads, d_inner), 0)
    spread = (head_row == head_of_lane).astype(BF16)
    rows8 = lambda v: jnp.broadcast_to(v, (SUBLANES, n_heads))
    per_head = jnp.concatenate([dt, decay_to_end, decay_from_start, rows8(chunk_decay), rows8(dskip_ref[...])],
                               axis=0)
    wide = _dot_f32_lhs(per_head, spread)
    lane_head = lax.broadcasted_iota(I32, (1, GW), 1) // P

    for g in range(G):
        cols = slice(g * GW, (g + 1) * GW)
        dt_g = wide[0:L, cols]
        to_end_g = wide[L:2 * L, cols]
        from_start_g = wide[2 * L:3 * L, cols]
        chunk_decay_g = wide[3 * L:3 * L + 1, cols]
        dskip_g = wide[3 * L + SUBLANES:3 * L + SUBLANES + 1, cols]

        x_g = conv_silu(g * GW, GW)
        b_g = conv_silu(d_inner + g * N, N)
        c_g = conv_silu(d_inner + G * N + g * N, N)
        xdt = x_g * dt_g
        b_bf = b_g.astype(BF16)
        c_bf = c_g.astype(BF16)

        cb = _dot(c_bf, b_bf, NT_DIMS)
        xdt_bf = xdt.astype(BF16)
        y_g = jnp.zeros((L, GW), F32)
        for r in range(R):
            hd = g * R + r
            seg = cs[:, hd:hd + 1] - cs_sq[hd:hd + 1, :]
            m = jnp.where(lower, cb * jnp.exp(seg), 0.0)
            x_r = jnp.where(lane_head == r, xdt_bf, jnp.zeros_like(xdt_bf))
            y_g = y_g + _dot(m.astype(BF16), x_r)
        prev = state_ref[g]
        y_g = y_g + from_start_g * _dot(c_bf, prev.astype(BF16))
        new_part = _dot(b_bf, (xdt * to_end_g).astype(BF16), TN_DIMS)
        state_ref[g] = prev * chunk_decay_g + new_part
        y_g = y_g + x_g * dskip_g

        yz = y_g * _silu(z_ref[:, g * GW:(g + 1) * GW])
        yn = yz * lax.rsqrt(jnp.mean(yz * yz, axis=-1, keepdims=True) + EPS)
        y_ref[:, g * GW:(g + 1) * GW] = (yn * nw_ref[:, g * GW:(g + 1) * GW]).astype(BF16)

    ext_ref[...] = xbc_ref[L - halo:L, :]


def _ssd_scan(xbc, z, dt, conv_w, conv_b, dt_bias, a_log, d_skip, norm_w, batch, seq):
    t, dx = xbc.shape
    d_inner = z.shape[1]
    n_heads = dt.shape[1]
    L = SSD_CHUNK
    nc = seq // L
    row = lambda n: pl.BlockSpec((1, n), lambda b, c: (0, 0))
    kern = functools.partial(_ssd_kernel, d_inner=d_inner, n_heads=n_heads)
    return pl.pallas_call(
        kern,
        grid=(batch, nc),
        in_specs=[
            pl.BlockSpec((L, dx), lambda b, c: (b * nc + c, 0)),
            pl.BlockSpec((L, d_inner), lambda b, c: (b * nc + c, 0)),
            pl.BlockSpec((L, n_heads), lambda b, c: (b * nc + c, 0)),
            pl.BlockSpec((SSD_CONV, dx), lambda b, c: (0, 0)),
            row(dx), row(n_heads), row(n_heads), row(n_heads), row(d_inner),
        ],
        out_specs=pl.BlockSpec((L, d_inner), lambda b, c: (b * nc + c, 0)),
        out_shape=jax.ShapeDtypeStruct((t, d_inner), BF16),
        scratch_shapes=[
            pltpu.VMEM((SUBLANES, dx), F32),
            pltpu.VMEM((SSD_GROUPS, SSD_STATE, d_inner // SSD_GROUPS), F32),
        ],
        compiler_params=_params("arbitrary", "arbitrary"),
        name="ssd_scan",
    )(xbc, z, dt, conv_w, conv_b, dt_bias, a_log, d_skip, norm_w)


def _rope_tables(seq):
    half = ATTN_HEAD_DIM // 2
    inv = ROPE_THETA ** (-jnp.arange(half, dtype=F32) / half)
    ang = jnp.arange(seq).astype(F32)[:, None] * inv[None, :]
    cos = jnp.cos(ang)
    sin = jnp.sin(ang)
    reps = LANES // ATTN_HEAD_DIM
    cos_t = jnp.tile(jnp.concatenate([cos, cos], axis=1), (1, reps))
    sin_t = jnp.tile(jnp.concatenate([-sin, sin], axis=1), (1, reps))
    return cos_t, sin_t


def kernel(x, mix_norm, ffn_norm, final_norm, attn_w_qkv, attn_w_o, ssd_w_in, ssd_conv_w, ssd_conv_b,
           ssd_dt_bias, ssd_a_log, ssd_d, ssd_norm, ssd_w_out, moe_w_group, moe_b_group, moe_w_router,
           moe_b_router, moe_w_up, moe_w_down):
    batch, seq, d = x.shape
    assert seq % MOBA_BLOCK == 0 and seq % SSD_CHUNK == 0 and d % LANES == 0
    depth = mix_norm.shape[0]
    t = batch * seq
    xt = x.reshape(t, d)
    cos_t, sin_t = _rope_tables(seq)
    final_w = final_norm[None, :]

    for i in range(depth):
        j = i // 2
        nw = mix_norm[i][None, :]
        if i % 2 == 0:
            w_qkv = attn_w_qkv[j].astype(BF16)
            q, k, vt, kmean = _qkv_rope(xt, nw, w_qkv[:, 0:2 * d], w_qkv[:, 2 * d:].T, cos_t, sin_t, seq)
            kmean = kmean.reshape(batch, seq // MOBA_BLOCK, d)
            o = _moba_attention(q, k, vt, kmean, batch, seq)
            xt = _proj_residual(o, attn_w_o[j].astype(BF16), xt)
        else:
            w_in = ssd_w_in[j].astype(BF16)
            d_inner = ssd_norm.shape[1]
            dx = ssd_conv_b.shape[1]
            z, xbc, dt = _ssd_in_proj(xt, nw, w_in[:, 0:d_inner], w_in[:, d_inner:d_inner + dx],
                                      w_in[:, d_inner + dx:])
            y = _ssd_scan(xbc, z, dt, ssd_conv_w[j][:, 0, :], ssd_conv_b[j][None, :], ssd_dt_bias[j][None, :],
                          ssd_a_log[j][None, :], ssd_d[j][None, :], ssd_norm[j][None, :], batch, seq)
            xt = _proj_residual(y, ssd_w_out[j].astype(BF16), xt)
        xt = _hier_moe(xt, ffn_norm[i][None, :], moe_w_group[i], moe_b_group[i], moe_w_router[i],
                       moe_b_router[i], moe_w_up, moe_w_down, i, final_w, final_norm=(i == depth - 1))
    return xt.reshape(batch, seq, d)
```

```python
import functools
import math

import jax
import jax.numpy as jnp
from jax import lax
from jax.experimental import pallas as pl
from jax.experimental.pallas import tpu as pltpu

F32 = jnp.float32
BF16 = jnp.bfloat16
I32 = jnp.int32

EPS = 1e-6
LANES = 128
SUBLANES = 8
VMEM_LIMIT = 56 * 1024 * 1024

ATTN_HEADS = 16
ATTN_HEAD_DIM = 64
MOBA_BLOCK = 256
MOBA_TOPK = 3
ROPE_THETA = 10000.0
HEADS_PER_TILE = LANES // ATTN_HEAD_DIM
SUM_ROWS = 2 * SUBLANES

SSD_HEAD_DIM = 64
SSD_GROUPS = 8
SSD_STATE = 128
SSD_CONV = 4
SSD_CHUNK = 128

MOE_GROUPS = 4
MOE_EXPERTS_PER_GROUP = 8
MOE_EXPERTS = MOE_GROUPS * MOE_EXPERTS_PER_GROUP
MOE_ROW_BLOCK = 512
ROUTER_ROWS = SUBLANES + MOE_EXPERTS
MOE_TILE = 256
RUN_ALIGN = SUBLANES
RUN_SIZES = (16, 8)
MOE_TILE_BUF = 2 * MOE_TILE + MOE_EXPERTS * RUN_ALIGN
PAD_FILL_SIZES = tuple(MOE_ROW_BLOCK >> s for s in range((MOE_ROW_BLOCK // RUN_ALIGN).bit_length()))

NT_DIMS = (((1,), (1,)), ((), ()))
NN_DIMS = (((1,), (0,)), ((), ()))
TN_DIMS = (((0,), (0,)), ((), ()))


def _params(*sem):
    return pltpu.CompilerParams(dimension_semantics=sem, vmem_limit_bytes=VMEM_LIMIT)


def _rmsnorm_rows(x, w):
    var = jnp.mean(x * x, axis=-1, keepdims=True)
    return (x * lax.rsqrt(var + EPS)) * w


def _split3(x):
    hi = x.astype(BF16)
    r = x - hi.astype(F32)
    mid = r.astype(BF16)
    lo = (r - mid.astype(F32)).astype(BF16)
    return hi, mid, lo


def _dot(a, b, dims=NN_DIMS):
    return lax.dot_general(a, b, dims, preferred_element_type=F32)


def _dot_f32_lhs(x, e, dims=NN_DIMS):
    hi, mid, lo = _split3(x)
    return _dot(hi, e, dims) + _dot(mid, e, dims) + _dot(lo, e, dims)


def _dot_f32_rhs(e, x, dims=NN_DIMS):
    hi, mid, lo = _split3(x)
    return _dot(e, hi, dims) + _dot(e, mid, dims) + _dot(e, lo, dims)


def _dot_f32_3pass(a, b, dims=NN_DIMS):
    ah, am, _ = _split3(a)
    bh, bm, _ = _split3(b)
    return _dot(ah, bh, dims) + _dot(ah, bm, dims) + _dot(am, bh, dims)


def _silu(x):
    h = 0.5 * x
    return h + h * jnp.tanh(h)


def _softplus(x):
    return jnp.maximum(x, 0.0) + jnp.log(1.0 + jnp.exp(-jnp.abs(x)))


def _qkv_kernel(x_ref, nw_ref, wqk_ref, wvt_ref, cos_ref, sin_ref, q_ref, k_ref, vt_ref, km_ref, *, d, scale):
    h = _rmsnorm_rows(x_ref[...], nw_ref[...]).astype(BF16)
    cos = cos_ref[...]
    sin = sin_ref[...]
    lane = lax.broadcasted_iota(I32, (1, LANES), 1)
    first_half = (lane % ATTN_HEAD_DIM) < (ATTN_HEAD_DIM // 2)

    def rope_tile(xc):
        lo_partner = pltpu.roll(xc, LANES - ATTN_HEAD_DIM // 2, 1)
        hi_partner = pltpu.roll(xc, ATTN_HEAD_DIM // 2, 1)
        return xc * cos + jnp.where(first_half, lo_partner, hi_partner) * sin

    q = _dot(h, wqk_ref[:, 0:d])
    k = _dot(h, wqk_ref[:, d:2 * d])
    for c in range(d // LANES):
        sl = slice(c * LANES, (c + 1) * LANES)
        q_ref[:, sl] = (rope_tile(q[:, sl]) * scale).astype(BF16)
        kc = rope_tile(k[:, sl])
        k_ref[:, sl] = kc.astype(BF16)
        km_ref[0, :, sl] = jnp.mean(kc, axis=0, keepdims=True)
    vt_ref[0] = _dot(wvt_ref[...], h, NT_DIMS).astype(BF16)


def _qkv_rope(x, norm_w, w_qk, w_vt, cos_t, sin_t, seq):
    t, d = x.shape
    tm = MOBA_BLOCK
    n_tiles = t // tm
    seq_tiles = seq // tm
    kern = functools.partial(_qkv_kernel, d=d, scale=math.log2(math.e) / math.sqrt(ATTN_HEAD_DIM))
    return pl.pallas_call(
        kern,
        grid=(n_tiles,),
        in_specs=[
            pl.BlockSpec((tm, d), lambda i: (i, 0)),
            pl.BlockSpec((1, d), lambda i: (0, 0)),
            pl.BlockSpec((d, 2 * d), lambda i: (0, 0), pipeline_mode=pl.Buffered(1)),
            pl.BlockSpec((d, d), lambda i: (0, 0), pipeline_mode=pl.Buffered(1)),
            pl.BlockSpec((tm, LANES), lambda i: (i % seq_tiles, 0)),
            pl.BlockSpec((tm, LANES), lambda i: (i % seq_tiles, 0)),
        ],
        out_specs=[
            pl.BlockSpec((tm, d), lambda i: (i, 0)),
            pl.BlockSpec((tm, d), lambda i: (i, 0)),
            pl.BlockSpec((1, d, tm), lambda i: (i, 0, 0)),
            pl.BlockSpec((1, 1, d), lambda i: (i, 0, 0)),
        ],
        out_shape=[
            jax.ShapeDtypeStruct((t, d), BF16),
            jax.ShapeDtypeStruct((t, d), BF16),
            jax.ShapeDtypeStruct((n_tiles, d, tm), BF16),
            jax.ShapeDtypeStruct((n_tiles, 1, d), F32),
        ],
        compiler_params=_params("parallel"),
        name="qkv_rope",
    )(x, norm_w, w_qk, w_vt, cos_t, sin_t)


def _attn_kernel(q_ref, k_ref, vt_ref, km_ref, o_ref, sa_ref, sb_ref, sc_ref, p_ref, *, nb):
    qi = pl.program_id(2)
    tq = q_ref.shape[0]
    q2 = q_ref[...]
    km = km_ref[...]
    lane = lax.broadcasted_iota(I32, (1, LANES), 1)
    key = lax.broadcasted_iota(I32, (MOBA_BLOCK, tq), 0)
    qry = lax.broadcasted_iota(I32, (MOBA_BLOCK, tq), 1)
    causal = key <= qry
    blk = lax.broadcasted_iota(I32, (nb, tq), 0)
    neg_inf = jnp.float32(-jnp.inf)
    heads = range(HEADS_PER_TILE)

    qh = [jnp.where((lane // ATTN_HEAD_DIM) == h, q2, jnp.zeros_like(q2)) for h in heads]

    bits = []
    for h in heads:
        gate = _dot_f32_lhs(km, qh[h], NT_DIMS)
        alive = (blk < qi).astype(I32)
        sel = jnp.zeros((nb, tq), I32)
        for _ in range(min(MOBA_TOPK, nb)):
            gm = jnp.where(alive > 0, gate, neg_inf)
            top = jnp.max(gm, axis=0, keepdims=True)
            cand = jnp.where((alive > 0) & (gm == top), blk, nb)
            idx = jnp.min(cand, axis=0, keepdims=True)
            pick = (blk == idx).astype(I32)
            sel = sel | pick
            alive = alive & (1 - pick)
        bits.append(jnp.sum(sel << blk, axis=0, keepdims=True))

    def block_of(k):
        return jnp.where(k == 0, qi, jnp.minimum(k - 1, qi))

    def issue_scores(k, s_out):
        start = pl.multiple_of(block_of(k) * MOBA_BLOCK, MOBA_BLOCK)
        kb = k_ref[pl.ds(start, MOBA_BLOCK), :]
        for h in heads:
            s_out[h] = _dot(kb, qh[h], NT_DIMS)

    ones_rows = jnp.ones((SUM_ROWS, MOBA_BLOCK), BF16)

    def issue_values(k):
        blk_idx = block_of(k)
        out = []
        for h in heads:
            v_ext = jnp.concatenate([vt_ref[blk_idx, h * ATTN_HEAD_DIM:(h + 1) * ATTN_HEAD_DIM, :], ones_rows],
                                    axis=0)
            out.append(_dot(v_ext, p_ref[h]))
        return out

    def softmax(k, s_in, state, diag):
        out = []
        for h in heads:
            m_i = state[h]
            if diag:
                s = jnp.where(causal, s_in[h], neg_inf)
                m_new = jnp.maximum(m_i, jnp.max(s, axis=0, keepdims=True))
                shift = m_new
            else:
                s = s_in[h]
                keep = ((bits[h] >> (k - 1)) & 1) == 1
                m_new = jnp.maximum(m_i, jnp.where(keep, jnp.max(s, axis=0, keepdims=True), neg_inf))
                shift = jnp.where(keep, m_new, -neg_inf)
            alpha = jnp.exp2(m_i - m_new)
            p_ref[h] = jnp.exp2(s - shift).astype(BF16)
            out.append((m_new, alpha))
        return out

    s_refs = (sa_ref, sb_ref, sc_ref)
    depth = len(s_refs)

    def phase(k, slot, carry):
        issue_scores(k + depth - 1, s_refs[(slot + depth - 1) % depth])
        pv = issue_values(k - 1)
        stats = softmax(k, s_refs[slot], [c[0] for c in carry], False)
        return tuple((stats[h][0], carry[h][2] * carry[h][1] + pv[h], stats[h][1]) for h in heads)

    for k in range(depth):
        issue_scores(k, s_refs[k])
    first = softmax(0, s_refs[0], [jnp.full((1, tq), neg_inf, F32) for _ in heads], True)
    carry = tuple((m, jnp.zeros((ATTN_HEAD_DIM + SUM_ROWS, tq), F32), a) for (m, a) in first)

    def phases(u, c):
        for j in range(1, depth + 1):
            c = phase(depth * u + j, j % depth, c)
        return c

    n_iters = (qi + depth - 1) // depth
    carry = lax.fori_loop(0, n_iters, phases, carry)
    pv = issue_values(depth * n_iters)
    outs = []
    for h, (_, acc, alpha) in enumerate(carry):
        total = alpha * acc + pv[h]
        outs.append(total[0:ATTN_HEAD_DIM] / total[ATTN_HEAD_DIM:ATTN_HEAD_DIM + 1])
    o_ref[...] = jnp.concatenate(outs, axis=0).T.astype(BF16)


def _moba_attention(q, k, vt, kmean, batch, seq):
    t, d = q.shape
    nb = seq // MOBA_BLOCK
    tq = MOBA_BLOCK
    nq = seq // tq
    col_tiles = d // LANES
    kern = functools.partial(_attn_kernel, nb=nb)
    return pl.pallas_call(
        kern,
        grid=(batch, col_tiles, nq),
        in_specs=[
            pl.BlockSpec((tq, LANES), lambda b, c, i: (b * nq + i, c)),
            pl.BlockSpec((seq, LANES), lambda b, c, i: (b, c)),
            pl.BlockSpec((nb, LANES, MOBA_BLOCK), lambda b, c, i: (b, c, 0)),
            pl.BlockSpec((None, nb, LANES), lambda b, c, i: (b, 0, c)),
        ],
        out_specs=pl.BlockSpec((tq, LANES), lambda b, c, i: (b * nq + i, c)),
        out_shape=jax.ShapeDtypeStruct((t, d), BF16),
        scratch_shapes=[
            pltpu.VMEM((HEADS_PER_TILE, MOBA_BLOCK, tq), F32),
            pltpu.VMEM((HEADS_PER_TILE, MOBA_BLOCK, tq), F32),
            pltpu.VMEM((HEADS_PER_TILE, MOBA_BLOCK, tq), F32),
            pltpu.VMEM((HEADS_PER_TILE, MOBA_BLOCK, tq), BF16),
        ],
        compiler_params=_params("parallel", "parallel", "arbitrary"),
        name="moba_attention",
    )(q, k, vt, kmean)


def _proj_res_kernel(a_ref, w_ref, res_ref, o_ref):
    o_ref[...] = res_ref[...] + _dot(a_ref[...], w_ref[...])


def _proj_residual(a, w, res, tm=512):
    t, kdim = a.shape
    d = w.shape[1]
    return pl.pallas_call(
        _proj_res_kernel,
        grid=(t // tm,),
        in_specs=[
            pl.BlockSpec((tm, kdim), lambda i: (i, 0)),
            pl.BlockSpec((kdim, d), lambda i: (0, 0), pipeline_mode=pl.Buffered(1)),
            pl.BlockSpec((tm, d), lambda i: (i, 0)),
        ],
        out_specs=pl.BlockSpec((tm, d), lambda i: (i, 0)),
        out_shape=jax.ShapeDtypeStruct((t, d), F32),
        compiler_params=_params("parallel"),
        name="proj_residual",
    )(a, w, res)


def _router_kernel(x_ref, nw_ref, wr_ref, br_ref, gate_ref, tinfo_ref, cnt_ref, carry_ref):
    tm = x_ref.shape[0]

    @pl.when(pl.program_id(0) == 0)
    def _():
        carry_ref[...] = jnp.zeros_like(carry_ref)

    hn = _rmsnorm_rows(x_ref[...], nw_ref[...])
    logits = _dot_f32_3pass(wr_ref[...], hn, NT_DIMS) + br_ref[...]
    sub = lax.broadcasted_iota(I32, (SUBLANES, tm), 0)
    neg_inf = jnp.float32(-jnp.inf)

    gl = logits[0:SUBLANES]
    gmax = jnp.max(gl, axis=0, keepdims=True)
    gidx = jnp.min(jnp.where(gl == gmax, sub, SUBLANES), axis=0, keepdims=True)
    gprob = 1.0 / jnp.sum(jnp.exp(gl - gmax), axis=0, keepdims=True)

    el = jnp.zeros((SUBLANES, tm), F32)
    for g in range(MOE_GROUPS):
        el = jnp.where(gidx == g, logits[SUBLANES * (g + 1):SUBLANES * (g + 2)], el)
    m1 = jnp.max(el, axis=0, keepdims=True)
    i1 = jnp.min(jnp.where(el == m1, sub, SUBLANES), axis=0, keepdims=True)
    el2 = jnp.where(sub == i1, neg_inf, el)
    m2 = jnp.max(el2, axis=0, keepdims=True)
    i2 = jnp.min(jnp.where(el2 == m2, sub, SUBLANES), axis=0, keepdims=True)
    ratio = jnp.exp(m2 - m1)
    den = 1.0 + ratio
    gate_ref[...] = jnp.zeros_like(gate_ref)
    gate_ref[0:1, :] = gprob / den
    gate_ref[1:2, :] = gprob * ratio / den

    e1 = gidx * MOE_EXPERTS_PER_GROUP + i1
    e2 = gidx * MOE_EXPERTS_PER_GROUP + i2

    eiota = lax.broadcasted_iota(I32, (MOE_EXPERTS, tm), 0)
    oh1 = eiota == e1
    oh2 = eiota == e2
    onehot = (oh1 | oh2).astype(BF16)
    src = lax.broadcasted_iota(I32, (tm, tm), 0)
    dst = lax.broadcasted_iota(I32, (tm, tm), 1)
    before = (src < dst).astype(BF16)
    prefix = _dot(onehot, before)
    count = _dot(onehot, jnp.ones((tm, tm), BF16))
    total = jnp.floor((count + (RUN_ALIGN - 1)) * (1.0 / RUN_ALIGN)) * RUN_ALIGN
    erow = lax.broadcasted_iota(I32, (MOE_EXPERTS, MOE_EXPERTS), 0)
    ecol = lax.broadcasted_iota(I32, (MOE_EXPERTS, MOE_EXPERTS), 1)
    run_start = _dot_f32_rhs((ecol < erow).astype(BF16), total)
    slot = run_start + prefix
    gate_ref[2:3, :] = jnp.sum(jnp.where(oh1, slot, 0.0), axis=0, keepdims=True)
    gate_ref[3:4, :] = jnp.sum(jnp.where(oh2, slot, 0.0), axis=0, keepdims=True)

    lane = lax.broadcasted_iota(I32, (MOE_EXPERTS, tm), 1)
    on_lanes = lambda v: jnp.sum(jnp.where(eiota == lane, v, 0.0), axis=0, keepdims=True)[:, 0:LANES].astype(I32)
    tinfo_ref[...] = jnp.zeros_like(tinfo_ref)
    tinfo_ref[0, 0:1, :] = on_lanes(total)
    tinfo_ref[0, 1:2, :] = on_lanes(carry_ref[...])
    tinfo_ref[0, 2:3, :] = on_lanes(run_start)
    carry_ref[...] = carry_ref[...] + total
    cnt_ref[...] = carry_ref[...]


def _router(x, norm_w, wr_t, br):
    t, d = x.shape
    tm = MOE_TILE
    return pl.pallas_call(
        _router_kernel,
        grid=(t // tm,),
        in_specs=[
            pl.BlockSpec((tm, d), lambda i: (i, 0)),
            pl.BlockSpec((1, d), lambda i: (0, 0)),
            pl.BlockSpec((ROUTER_ROWS, d), lambda i: (0, 0)),
            pl.BlockSpec((ROUTER_ROWS, 1), lambda i: (0, 0)),
        ],
        out_specs=[
            pl.BlockSpec((SUBLANES, tm), lambda i: (0, i)),
            pl.BlockSpec((1, SUBLANES, LANES), lambda i: (i, 0, 0)),
            pl.BlockSpec((MOE_EXPERTS, tm), lambda i: (0, 0)),
        ],
        out_shape=[
            jax.ShapeDtypeStruct((SUBLANES, t), F32),
            jax.ShapeDtypeStruct((t // tm, SUBLANES, LANES), I32),
            jax.ShapeDtypeStruct((MOE_EXPERTS, tm), F32),
        ],
        scratch_shapes=[pltpu.VMEM((MOE_EXPERTS, tm), F32)],
        compiler_params=_params("arbitrary"),
        name="moe_router",
    )(x, norm_w, wr_t, br)


U32 = jnp.uint32
HIGH_HALF = 0xFFFF0000


def _pack_halves(a):
    c = a.shape[1] // 2
    left = lax.bitcast_convert_type(a[:, :c], U32)
    right = lax.bitcast_convert_type(a[:, c:], U32)
    return (left & U32(HIGH_HALF)) | (right >> 16)


def _unpack_halves(p):
    left = lax.bitcast_convert_type(p & U32(HIGH_HALF), F32)
    right = lax.bitcast_convert_type(p << 16, F32)
    return left.astype(BF16), right.astype(BF16)


def _for_each_run_piece(tinfo_ref, pstart_ref, fn):
    full = RUN_SIZES[0]
    aligned = lambda v: pl.multiple_of(v, RUN_ALIGN)

    def per_expert(e, c):
        cnt = tinfo_ref[0, 0, e]
        sorted_row = pstart_ref[e] + tinfo_ref[0, 1, e]
        buf_row = tinfo_ref[0, 2, e]

        def whole(k, c2):
            fn(aligned(buf_row + k * full), aligned(sorted_row + k * full), full, 0)
            return c2

        lax.fori_loop(0, cnt // full, whole, 0)
        for cls, size in enumerate(RUN_SIZES[1:], 1):
            done = cnt & ~(2 * size - 1)

            @pl.when((cnt & size) != 0)
            def _():
                fn(aligned(buf_row + done), aligned(sorted_row + done), size, cls)
        return c

    lax.fori_loop(0, MOE_EXPERTS, per_expert, 0)


def _zero_unused_rows(padstart_ref, padgap_ref, nused_ref, rows_ref, zero_ref, sem):
    zero_ref[...] = jnp.zeros_like(zero_ref)
    block = PAD_FILL_SIZES[0]
    n_blocks = rows_ref.shape[0] // block

    def piece(row, n, cls):
        return pltpu.make_async_copy(zero_ref.at[pl.ds(0, n)], rows_ref.at[pl.ds(row, n)], sem.at[cls])

    def for_each_piece(fn):
        def per_expert(e, c):
            gap = padgap_ref[e]
            for cls, size in enumerate(PAD_FILL_SIZES[1:], 1):
                done = gap & ~(2 * size - 1)

                @pl.when((gap & size) != 0)
                def _():
                    fn(pl.multiple_of(padstart_ref[e] + done, RUN_ALIGN), size, cls)
            return c

        lax.fori_loop(0, MOE_EXPERTS, per_expert, 0)

        def unused_block(b, c):
            fn(pl.multiple_of(b * block, block), block, 0)
            return c

        lax.fori_loop(nused_ref[0], n_blocks, unused_block, 0)

    for_each_piece(lambda *a: piece(*a).start())
    for_each_piece(lambda *a: piece(*a).wait())


def _dispatch_kernel(pstart_ref, padstart_ref, padgap_ref, nused_ref, x_ref, nw_ref, slot_ref, tinfo_ref,
                     rows_ref, buf_ref, zero_ref, sem, zero_sem):
    tm = x_ref.shape[0]
    nbuf = buf_ref.shape[0]

    @pl.when(pl.program_id(0) == 0)
    def _():
        _zero_unused_rows(padstart_ref, padgap_ref, nused_ref, rows_ref, zero_ref, zero_sem)

    hn = _rmsnorm_rows(x_ref[...], nw_ref[...]).astype(BF16)
    r = lax.broadcasted_iota(I32, (nbuf, tm), 0)
    slots = slot_ref[...].astype(I32)
    place = ((r == slots[2:3, :]) | (r == slots[3:4, :])).astype(BF16)
    buf_ref[...] = _pack_halves(_dot(place, hn))

    def piece(buf_row, sorted_row, n, cls):
        return pltpu.make_async_copy(buf_ref.at[pl.ds(buf_row, n)], rows_ref.at[pl.ds(sorted_row, n)], sem.at[cls])

    _for_each_run_piece(tinfo_ref, pstart_ref, lambda *a: piece(*a).start())
    _for_each_run_piece(tinfo_ref, pstart_ref, lambda *a: piece(*a).wait())


def _dispatch(pstart, pad_start, pad_gap, n_used, x, norm_w, gates, tinfo, cap):
    t, d = x.shape
    tm = MOE_TILE
    spec = lambda shape, index: pl.BlockSpec(shape, lambda i, *prefetch: index(i))
    grid_spec = pltpu.PrefetchScalarGridSpec(
        num_scalar_prefetch=4,
        grid=(t // tm,),
        in_specs=[
            spec((tm, d), lambda i: (i, 0)),
            spec((1, d), lambda i: (0, 0)),
            spec((SUBLANES, tm), lambda i: (0, i)),
            pl.BlockSpec((1, SUBLANES, LANES), lambda i, *prefetch: (i, 0, 0), memory_space=pltpu.SMEM),
        ],
        out_specs=pl.BlockSpec(memory_space=pl.ANY),
        scratch_shapes=[
            pltpu.VMEM((MOE_TILE_BUF, d // 2), U32),
            pltpu.VMEM((PAD_FILL_SIZES[0], d // 2), U32),
            pltpu.SemaphoreType.DMA((len(RUN_SIZES),)),
            pltpu.SemaphoreType.DMA((len(PAD_FILL_SIZES),)),
        ],
    )
    return pl.pallas_call(
        _dispatch_kernel,
        grid_spec=grid_spec,
        out_shape=jax.ShapeDtypeStruct((cap, d // 2), U32),
        compiler_params=_params("arbitrary"),
        name="moe_dispatch",
    )(pstart, pad_start, pad_gap, n_used, x, norm_w, gates, tinfo)


def _expert_kernel(be_ref, nused_ref, rows_ref, wu_ref, wd_ref, y_ref, wu_bf_ref, wd_bf_ref):
    ff = wd_ref.shape[0]
    i = pl.program_id(0)
    used = i < nused_ref[0]
    new_expert = jnp.logical_or(i == 0, be_ref[i] != be_ref[jnp.maximum(i - 1, 0)])

    @pl.when(jnp.logical_and(used, new_expert))
    def _():
        wu_bf_ref[...] = wu_ref[...].astype(BF16)
        wd_bf_ref[...] = wd_ref[...].astype(BF16)

    @pl.when(used)
    def _():
        x_left, x_right = _unpack_halves(rows_ref[...])
        half = x_left.shape[1]
        gu = _dot(x_left, wu_bf_ref[0:half, :]) + _dot(x_right, wu_bf_ref[half:2 * half, :])
        act = _silu(gu[:, 0:ff]) * gu[:, ff:2 * ff]
        y = _dot(act.astype(BF16), wd_bf_ref[...])
        y_ref[...] = _pack_halves(y.astype(BF16).astype(F32))

    @pl.when(jnp.logical_not(used))
    def _():
        y_ref[...] = jnp.zeros_like(y_ref)


def _experts(block_expert, n_used, rows, w_up, w_down, layer):
    cap, dp = rows.shape
    d = 2 * dp
    rb = MOE_ROW_BLOCK
    ff = w_down.shape[2]
    grid_spec = pltpu.PrefetchScalarGridSpec(
        num_scalar_prefetch=2,
        grid=(cap // rb,),
        in_specs=[
            pl.BlockSpec((rb, dp), lambda i, be, nu: (jnp.minimum(i, nu[0] - 1), 0)),
            pl.BlockSpec((None, None, d, 2 * ff), lambda i, be, nu: (layer, be[i], 0, 0)),
            pl.BlockSpec((None, None, ff, d), lambda i, be, nu: (layer, be[i], 0, 0)),
        ],
        out_specs=pl.BlockSpec((rb, dp), lambda i, be, nu: (i, 0)),
        scratch_shapes=[pltpu.VMEM((d, 2 * ff), BF16), pltpu.VMEM((ff, d), BF16)],
    )
    return pl.pallas_call(
        _expert_kernel,
        grid_spec=grid_spec,
        out_shape=jax.ShapeDtypeStruct((cap, dp), U32),
        compiler_params=_params("arbitrary"),
        name="moe_experts",
    )(block_expert, n_used, rows, w_up, w_down)


def _combine_kernel(pstart_ref, x_ref, gate_ref, tinfo_ref, tnext_ref, fw_ref, y_ref, o_ref, buf_ref, sem,
                    *, final_norm):
    tm = x_ref.shape[0]
    nbuf = buf_ref.shape[1]
    i = pl.program_id(0)
    cur = i % 2

    def piece_into(which):
        def piece(buf_row, sorted_row, n, cls):
            return pltpu.make_async_copy(y_ref.at[pl.ds(sorted_row, n)], buf_ref.at[which, pl.ds(buf_row, n)],
                                         sem.at[which, cls])
        return piece

    @pl.when(i == 0)
    def _():
        buf_ref[...] = jnp.zeros_like(buf_ref)
        _for_each_run_piece(tinfo_ref, pstart_ref, lambda *a: piece_into(cur)(*a).start())

    @pl.when(i + 1 < pl.num_programs(0))
    def _():
        _for_each_run_piece(tnext_ref, pstart_ref, lambda *a: piece_into(1 - cur)(*a).start())

    g = gate_ref[...]
    cols = []
    for c in range(tm // LANES):
        sq = jnp.concatenate([g[:, c * LANES:(c + 1) * LANES], jnp.zeros((LANES - SUBLANES, LANES), F32)], axis=0)
        cols.append(sq.T)
    cols = jnp.concatenate(cols, axis=0)
    r = lax.broadcasted_iota(I32, (tm, nbuf), 1)
    weights = (jnp.where(r == cols[:, 2:3].astype(I32), cols[:, 0:1], 0.0)
               + jnp.where(r == cols[:, 3:4].astype(I32), cols[:, 1:2], 0.0))
    w_hi = weights.astype(BF16)
    w_lo = (weights - w_hi.astype(F32)).astype(BF16)

    _for_each_run_piece(tinfo_ref, pstart_ref, lambda *a: piece_into(cur)(*a).wait())

    y_left, y_right = _unpack_halves(buf_ref[cur])
    moe = jnp.concatenate([_dot(w_hi, y_left) + _dot(w_lo, y_left), _dot(w_hi, y_right) + _dot(w_lo, y_right)],
                          axis=1)
    out = x_ref[...] + moe
    if final_norm:
        out = _rmsnorm_rows(out, fw_ref[...])
    o_ref[...] = out


def _combine(pstart, x, gates, tinfo, y_rows, final_w, final_norm):
    t, d = x.shape
    tm = MOE_TILE
    last = t // tm - 1
    grid_spec = pltpu.PrefetchScalarGridSpec(
        num_scalar_prefetch=1,
        grid=(t // tm,),
        in_specs=[
            pl.BlockSpec((tm, d), lambda i, ps: (i, 0)),
            pl.BlockSpec((SUBLANES, tm), lambda i, ps: (0, i)),
            pl.BlockSpec((1, SUBLANES, LANES), lambda i, ps: (i, 0, 0), memory_space=pltpu.SMEM),
            pl.BlockSpec((1, SUBLANES, LANES), lambda i, ps: (jnp.minimum(i + 1, last), 0, 0),
                         memory_space=pltpu.SMEM),
            pl.BlockSpec((1, d), lambda i, ps: (0, 0)),
            pl.BlockSpec(memory_space=pl.ANY),
        ],
        out_specs=pl.BlockSpec((tm, d), lambda i, ps: (i, 0)),
        scratch_shapes=[pltpu.VMEM((2, MOE_TILE_BUF, d // 2), U32), pltpu.SemaphoreType.DMA((2, len(RUN_SIZES)))],
    )
    return pl.pallas_call(
        functools.partial(_combine_kernel, final_norm=final_norm),
        grid_spec=grid_spec,
        out_shape=jax.ShapeDtypeStruct((t, d), F32),
        compiler_params=_params("arbitrary"),
        name="moe_combine",
    )(pstart, x, gates, tinfo, tinfo, final_w, y_rows)


def _hier_moe(x, norm_w, w_group, b_group, w_router, b_router, w_up, w_down, layer, final_w, final_norm):
    t, d = x.shape
    g, _, epg = w_router.shape
    pad_rows = SUBLANES - g
    wr_t = jnp.concatenate([w_group.T, jnp.zeros((pad_rows, d), F32),
                            w_router.transpose(0, 2, 1).reshape(g * epg, d)], axis=0)
    br = jnp.concatenate([b_group, jnp.full((pad_rows,), -1e30, F32), b_router.reshape(-1)])[:, None]
    gates, tinfo, cnt = _router(x, norm_w, wr_t, br)

    rb = MOE_ROW_BLOCK
    n_exp = g * epg
    counts = cnt[:, 0].astype(I32)
    padded = ((counts + rb - 1) // rb) * rb
    pends = jnp.cumsum(padded)
    pstart = (pends - padded).astype(I32)
    max_rows = 2 * t + (t // MOE_TILE) * n_exp * (RUN_ALIGN - 1) + n_exp * (rb - 1)
    n_blocks = -(-max_rows // rb)
    block_start = jnp.arange(n_blocks, dtype=I32) * rb
    block_expert = jnp.minimum(jnp.sum(block_start[:, None] >= pends[None, :], axis=1), n_exp - 1).astype(I32)
    n_used = (pends[-1:] // rb).astype(I32)

    rows = _dispatch(pstart, (pstart + counts).astype(I32), (padded - counts).astype(I32), n_used,
                     x, norm_w, gates, tinfo, n_blocks * rb)
    y_rows = _experts(block_expert, n_used, rows, w_up, w_down, layer)
    return _combine(pstart, x, gates, tinfo, y_rows, final_w, final_norm)


def _ssd_in_kernel(x_ref, nw_ref, wz_ref, wx_ref, wdt_ref, z_ref, xbc_ref, dt_ref):
    h = _rmsnorm_rows(x_ref[...], nw_ref[...]).astype(BF16)
    z_ref[...] = _dot(h, wz_ref[...])
    xbc_ref[...] = _dot(h, wx_ref[...])
    dt_ref[...] = _dot(h, wdt_ref[...])


def _ssd_in_proj(x, norm_w, w_z, w_xbc, w_dt, tm=256):
    t, d = x.shape
    dz, dx, dh = w_z.shape[1], w_xbc.shape[1], w_dt.shape[1]
    resident = lambda n: pl.BlockSpec((d, n), lambda i: (0, 0), pipeline_mode=pl.Buffered(1))
    return pl.pallas_call(
        _ssd_in_kernel,
        grid=(t // tm,),
        in_specs=[
            pl.BlockSpec((tm, d), lambda i: (i, 0)),
            pl.BlockSpec((1, d), lambda i: (0, 0)),
            resident(dz), resident(dx), resident(dh),
        ],
        out_specs=[
            pl.BlockSpec((tm, dz), lambda i: (i, 0)),
            pl.BlockSpec((tm, dx), lambda i: (i, 0)),
            pl.BlockSpec((tm, dh), lambda i: (i, 0)),
        ],
        out_shape=[
            jax.ShapeDtypeStruct((t, dz), F32),
            jax.ShapeDtypeStruct((t, dx), F32),
            jax.ShapeDtypeStruct((t, dh), F32),
        ],
        compiler_params=_params("parallel"),
        name="ssd_in_proj",
    )(x, norm_w, w_z, w_xbc, w_dt)


def _ssd_kernel(xbc_ref, z_ref, dt_ref, cw_ref, cb_ref, dtb_ref, alog_ref, dskip_ref, nw_ref,
                y_ref, ext_ref, state_ref, *, d_inner, n_heads):
    L = SSD_CHUNK
    G = SSD_GROUPS
    N = SSD_STATE
    P = SSD_HEAD_DIM
    R = n_heads // G
    GW = R * P
    halo = SUBLANES

    @pl.when(pl.program_id(1) == 0)
    def _():
        ext_ref[...] = jnp.zeros_like(ext_ref)
        state_ref[...] = jnp.zeros_like(state_ref)

    def conv_silu(c0, width):
        cols = slice(c0, c0 + width)
        cur = xbc_ref[:, cols]
        prev = ext_ref[:, cols]
        row = lax.broadcasted_iota(I32, (halo, width), 0)
        acc = cb_ref[:, cols] + cur * cw_ref[SSD_CONV - 1:SSD_CONV, cols]
        for back in range(1, SSD_CONV):
            rolled = pltpu.roll(cur, back, 0)
            head = jnp.where(row < back, pltpu.roll(prev, back, 0), rolled[0:halo])
            shifted = jnp.concatenate([head, rolled[halo:]], axis=0)
            acc = acc + shifted * cw_ref[SSD_CONV - 1 - back:SSD_CONV - back, cols]
        return _silu(acc)

    dt = _softplus(dt_ref[...] + dtb_ref[...])
    a_neg = -jnp.exp(alog_ref[...])
    da = dt * a_neg
    rr = lax.broadcasted_iota(I32, (L, L), 0)
    cc = lax.broadcasted_iota(I32, (L, L), 1)
    lower = rr >= cc
    cs = _dot_f32_rhs(lower.astype(BF16), da)
    cs_last = cs[L - 1:L, :]
    cs_sq = jnp.concatenate([cs, jnp.zeros((L, L - n_heads), F32)], axis=1).T
    decay_to_end = jnp.exp(cs_last - cs)
    decay_from_start = jnp.exp(cs)
    chunk_decay = jnp.exp(cs_last)

    head_of_lane = lax.broadcasted_iota(I32, (n_heads, d_inner), 1) // P
    head_row = lax.broadcasted_iota(I32, (n_heads, d_inner), 0)
    spread = (head_row == head_of_lane).astype(BF16)
    rows8 = lambda v: jnp.broadcast_to(v, (SUBLANES, n_heads))
    per_head = jnp.concatenate([dt, decay_to_end, decay_from_start, rows8(chunk_decay), rows8(dskip_ref[...])],
                               axis=0)
    wide = _dot_f32_lhs(per_head, spread)
    lane_head = lax.broadcasted_iota(I32, (1, GW), 1) // P

    for g in range(G):
        cols = slice(g * GW, (g + 1) * GW)
        dt_g = wide[0:L, cols]
        to_end_g = wide[L:2 * L, cols]
        from_start_g = wide[2 * L:3 * L, cols]
        chunk_decay_g = wide[3 * L:3 * L + 1, cols]
        dskip_g = wide[3 * L + SUBLANES:3 * L + SUBLANES + 1, cols]

        x_g = conv_silu(g * GW, GW)
        b_g = conv_silu(d_inner + g * N, N)
        c_g = conv_silu(d_inner + G * N + g * N, N)
        xdt = x_g * dt_g
        b_bf = b_g.astype(BF16)
        c_bf = c_g.astype(BF16)

        cb = _dot(c_bf, b_bf, NT_DIMS)
        xdt_bf = xdt.astype(BF16)
        y_g = jnp.zeros((L, GW), F32)
        for r in range(R):
            hd = g * R + r
            seg = cs[:, hd:hd + 1] - cs_sq[hd:hd + 1, :]
            m = jnp.where(lower, cb * jnp.exp(seg), 0.0)
            x_r = jnp.where(lane_head == r, xdt_bf, jnp.zeros_like(xdt_bf))
            y_g = y_g + _dot(m.astype(BF16), x_r)
        prev = state_ref[g]
        y_g = y_g + from_start_g * _dot(c_bf, prev.astype(BF16))
        new_part = _dot(b_bf, (xdt * to_end_g).astype(BF16), TN_DIMS)
        state_ref[g] = prev * chunk_decay_g + new_part
        y_g = y_g + x_g * dskip_g

        yz = y_g * _silu(z_ref[:, g * GW:(g + 1) * GW])
        yn = yz * lax.rsqrt(jnp.mean(yz * yz, axis=-1, keepdims=True) + EPS)
        y_ref[:, g * GW:(g + 1) * GW] = (yn * nw_ref[:, g * GW:(g + 1) * GW]).astype(BF16)

    ext_ref[...] = xbc_ref[L - halo:L, :]


def _ssd_scan(xbc, z, dt, conv_w, conv_b, dt_bias, a_log, d_skip, norm_w, batch, seq):
    t, dx = xbc.shape
    d_inner = z.shape[1]
    n_heads = dt.shape[1]
    L = SSD_CHUNK
    nc = seq // L
    row = lambda n: pl.BlockSpec((1, n), lambda b, c: (0, 0))
    kern = functools.partial(_ssd_kernel, d_inner=d_inner, n_heads=n_heads)
    return pl.pallas_call(
        kern,
        grid=(batch, nc),
        in_specs=[
            pl.BlockSpec((L, dx), lambda b, c: (b * nc + c, 0)),
            pl.BlockSpec((L, d_inner), lambda b, c: (b * nc + c, 0)),
            pl.BlockSpec((L, n_heads), lambda b, c: (b * nc + c, 0)),
            pl.BlockSpec((SSD_CONV, dx), lambda b, c: (0, 0)),
            row(dx), row(n_heads), row(n_heads), row(n_heads), row(d_inner),
        ],
        out_specs=pl.BlockSpec((L, d_inner), lambda b, c: (b * nc + c, 0)),
        out_shape=jax.ShapeDtypeStruct((t, d_inner), BF16),
        scratch_shapes=[
            pltpu.VMEM((SUBLANES, dx), F32),
            pltpu.VMEM((SSD_GROUPS, SSD_STATE, d_inner // SSD_GROUPS), F32),
        ],
        compiler_params=_params("arbitrary", "arbitrary"),
        name="ssd_scan",
    )(xbc, z, dt, conv_w, conv_b, dt_bias, a_log, d_skip, norm_w)


def _rope_tables(seq):
    half = ATTN_HEAD_DIM // 2
    inv = ROPE_THETA ** (-jnp.arange(half, dtype=F32) / half)
    ang = jnp.arange(seq).astype(F32)[:, None] * inv[None, :]
    cos = jnp.cos(ang)
    sin = jnp.sin(ang)
    reps = LANES // ATTN_HEAD_DIM
    cos_t = jnp.tile(jnp.concatenate([cos, cos], axis=1), (1, reps))
    sin_t = jnp.tile(jnp.concatenate([-sin, sin], axis=1), (1, reps))
    return cos_t, sin_t


def kernel(x, mix_norm, ffn_norm, final_norm, attn_w_qkv, attn_w_o, ssd_w_in, ssd_conv_w, ssd_conv_b,
           ssd_dt_bias, ssd_a_log, ssd_d, ssd_norm, ssd_w_out, moe_w_group, moe_b_group, moe_w_router,
           moe_b_router, moe_w_up, moe_w_down):
    batch, seq, d = x.shape
    assert seq % MOBA_BLOCK == 0 and seq % SSD_CHUNK == 0 and d % LANES == 0
    depth = mix_norm.shape[0]
    t = batch * seq
    xt = x.reshape(t, d)
    cos_t, sin_t = _rope_tables(seq)
    final_w = final_norm[None, :]

    for i in range(depth):
        j = i // 2
        nw = mix_norm[i][None, :]
        if i % 2 == 0:
            w_qkv = attn_w_qkv[j].astype(BF16)
            q, k, vt, kmean = _qkv_rope(xt, nw, w_qkv[:, 0:2 * d], w_qkv[:, 2 * d:].T, cos_t, sin_t, seq)
            kmean = kmean.reshape(batch, seq // MOBA_BLOCK, d)
            o = _moba_attention(q, k, vt, kmean, batch, seq)
            xt = _proj_residual(o, attn_w_o[j].astype(BF16), xt)
        else:
            w_in = ssd_w_in[j].astype(BF16)
            d_inner = ssd_norm.shape[1]
            dx = ssd_conv_b.shape[1]
            z, xbc, dt = _ssd_in_proj(xt, nw, w_in[:, 0:d_inner], w_in[:, d_inner:d_inner + dx],
                                      w_in[:, d_inner + dx:])
            y = _ssd_scan(xbc, z, dt, ssd_conv_w[j][:, 0, :], ssd_conv_b[j][None, :], ssd_dt_bias[j][None, :],
                          ssd_a_log[j][None, :], ssd_d[j][None, :], ssd_norm[j][None, :], batch, seq)
            xt = _proj_residual(y, ssd_w_out[j].astype(BF16), xt)
        xt = _hier_moe(xt, ffn_norm[i][None, :], moe_w_group[i], moe_b_group[i], moe_w_router[i],
                       moe_b_router[i], moe_w_up, moe_w_down, i, final_w, final_norm=(i == depth - 1))
    return xt.reshape(batch, seq, d)
```
